```python
import jax, jax.numpy as jnp
from jax import lax
import numpy as np

D_MODEL = 1024
BATCH = 8
SEQ = 2048
DEPTH = 4

N_META = 16
EXPAND = 2
D_INNER = EXPAND * D_MODEL
N_A = DEPTH // 2
N_B = DEPTH - N_A
GLA_HEADS = 4
GLA_DK_TOTAL = D_INNER // 2
GLA_HEAD_K = GLA_DK_TOTAL // GLA_HEADS
GLA_HEAD_V = D_INNER // GLA_HEADS
GLA_GATE_RANK = 16
GLA_GATE_NORMALIZER = 16.0
GLA_CHUNK = 16
GLA_IN_COLS = 2 * GLA_DK_TOTAL + D_INNER + GLA_GATE_RANK + D_INNER
FOX_HEADS = 16
FOX_HEAD_DIM = D_INNER // FOX_HEADS
FOX_BLOCK = 128
FOX_IN_COLS = 2 * D_INNER
KV_COLS = 2 * D_INNER + FOX_HEADS
RMS_EPS = 1e-6
NEG_INF = -1e30

kernel_name = "gla_fox_yoco_meta_hybrid"


def rmsnorm(x, g):
    xf = x.astype(jnp.float32)
    y = xf * lax.rsqrt(jnp.mean(xf * xf, axis=-1, keepdims=True) + RMS_EPS)
    return (y * g.astype(jnp.float32)).astype(x.dtype)


def gla_chunked(q, k, v, log_a):
    B, L, H, dk = q.shape
    dv = v.shape[-1]
    n_chunks = L // GLA_CHUNK

    def to_chunks(t):
        return t.reshape(B, n_chunks, GLA_CHUNK, H, t.shape[-1]).transpose(1, 0, 3, 2, 4)

    qc, kc, vc, gc = to_chunks(q), to_chunks(k), to_chunks(v), to_chunks(log_a.astype(jnp.float32))
    causal = jnp.tril(jnp.ones((GLA_CHUNK, GLA_CHUNK), dtype=bool))

    def step(S, inp):
        qn, kn, vn, gn = inp
        b = jnp.cumsum(gn, axis=-2)
        b_last = b[..., -1:, :]
        q_dec = qn * jnp.exp(b)
        k_inv = kn * jnp.exp(-b)
        k_rem = kn * jnp.exp(b_last - b)
        attn = jnp.where(causal, jnp.einsum('bhik,bhjk->bhij', q_dec, k_inv), 0.0)
        o = jnp.einsum('bhij,bhjv->bhiv', attn, vn) + jnp.einsum('bhik,bhkv->bhiv', q_dec, S)
        S = jnp.exp(b_last[..., 0, :])[..., None] * S + jnp.einsum('bhjk,bhjv->bhkv', k_rem, vn)
        return S, o

    S0 = jnp.zeros((B, H, dk, dv), jnp.float32)
    _, o = lax.scan(step, S0, (qc, kc, vc, gc))
    return o.transpose(1, 0, 3, 2, 4).reshape(B, L, H, dv).astype(v.dtype)


def gla_layer(h, norm_g, w_in, w_gate_up, b_gate, head_norm_g, w_out):
    B, L, _ = h.shape
    proj = rmsnorm(h, norm_g) @ w_in
    o1 = GLA_DK_TOTAL
    o2 = 2 * GLA_DK_TOTAL
    o3 = o2 + D_INNER
    o4 = o3 + GLA_GATE_RANK
    q, k, v, g_low, r = jnp.split(proj, [o1, o2, o3, o4], axis=-1)
    log_a = jax.nn.log_sigmoid((g_low @ w_gate_up + b_gate).astype(jnp.float32)) / GLA_GATE_NORMALIZER
    q = q.reshape(B, L, GLA_HEADS, GLA_HEAD_K) * (GLA_HEAD_K ** -0.5)
    k = k.reshape(B, L, GLA_HEADS, GLA_HEAD_K)
    v = v.reshape(B, L, GLA_HEADS, GLA_HEAD_V)
    log_a = log_a.reshape(B, L, GLA_HEADS, GLA_HEAD_K)
    o = gla_chunked(q, k, v, log_a)
    o = rmsnorm(o, head_norm_g).reshape(B, L, D_INNER)
    return h + (o * jax.nn.silu(r)) @ w_out


def fox_shared_kv(h, kv_norm_g, w_kv, b_forget):
    B, L, _ = h.shape
    proj = rmsnorm(h, kv_norm_g) @ w_kv
    k, v, f_logit = jnp.split(proj, [D_INNER, 2 * D_INNER], axis=-1)
    k = k.reshape(B, L, FOX_HEADS, FOX_HEAD_DIM).transpose(0, 2, 1, 3)
    v = v.reshape(B, L, FOX_HEADS, FOX_HEAD_DIM).transpose(0, 2, 1, 3)
    log_f = jax.nn.log_sigmoid((f_logit + b_forget).astype(jnp.float32))
    cum = jnp.cumsum(log_f, axis=1).transpose(0, 2, 1)
    return k, v, cum


def fox_layer(h, norm_g, w_in, w_out, k, v, cum):
    B, L, _ = h.shape
    q, r = jnp.split(rmsnorm(h, norm_g) @ w_in, 2, axis=-1)
    q = q.reshape(B, L, FOX_HEADS, FOX_HEAD_DIM).transpose(0, 2, 1, 3) * (FOX_HEAD_DIM ** -0.5)
    bounds = [0, N_META] + list(range(N_META + FOX_BLOCK, L + 1, FOX_BLOCK))
    outs = []
    for start, end in zip(bounds[:-1], bounds[1:]):
        s = jnp.einsum('bhqd,bhkd->bhqk', q[:, :, start:end], k[:, :, :end]).astype(jnp.float32)
        s = s + cum[:, :, start:end, None] - cum[:, :, None, :end]
        causal = jnp.arange(start, end)[:, None] >= jnp.arange(end)[None, :]
        p = jax.nn.softmax(jnp.where(causal, s, NEG_INF), axis=-1)
        outs.append(jnp.einsum('bhqk,bhkd->bhqd', p.astype(v.dtype), v[:, :, :end]))
    o = jnp.concatenate(outs, axis=2).transpose(0, 2, 1, 3).reshape(B, L, D_INNER)
    return h + (o * jax.nn.silu(r)) @ w_out


def setup_inputs(seed: int = 0) -> dict:
    key = jax.random.key(seed)
    ks = jax.random.split(key, 16)
    f32 = jnp.float32
    nrm = lambda k, shape, scale: jax.random.normal(k, shape, f32) * scale
    return {
        "x": nrm(ks[0], (BATCH, SEQ, D_MODEL), 1.0),
        "meta_tokens": nrm(ks[1], (N_META, D_MODEL), 1.0),
        "norm_g": 1.0 + nrm(ks[2], (DEPTH, D_MODEL), 0.02),
        "gla_w_in": nrm(ks[3], (N_A, D_MODEL, GLA_IN_COLS), D_MODEL ** -0.5),
        "gla_w_gate_up": nrm(ks[4], (N_A, GLA_GATE_RANK, GLA_DK_TOTAL), GLA_GATE_RANK ** -0.5),
        "gla_b_gate": nrm(ks[5], (N_A, GLA_DK_TOTAL), 0.02),
        "gla_head_norm_g": 1.0 + nrm(ks[6], (N_A, GLA_HEAD_V), 0.02),
        "gla_w_out": nrm(ks[7], (N_A, D_INNER, D_MODEL), 0.5 * D_INNER ** -0.5),
        "kv_norm_g": 1.0 + nrm(ks[8], (D_MODEL,), 0.02),
        "fox_w_kv": nrm(ks[9], (D_MODEL, KV_COLS), D_MODEL ** -0.5),
        "fox_b_forget": nrm(ks[10], (FOX_HEADS,), 0.02),
        "fox_w_in": nrm(ks[11], (N_B, D_MODEL, FOX_IN_COLS), D_MODEL ** -0.5),
        "fox_w_out": nrm(ks[12], (N_B, D_INNER, D_MODEL), 0.5 * D_INNER ** -0.5),
        "final_norm_g": 1.0 + nrm(ks[13], (D_MODEL,), 0.02),
    }


def reference(x, meta_tokens, norm_g, gla_w_in, gla_w_gate_up, gla_b_gate, gla_head_norm_g,
              gla_w_out, kv_norm_g, fox_w_kv, fox_b_forget, fox_w_in, fox_w_out, final_norm_g):
    B = x.shape[0]
    meta = jnp.broadcast_to(meta_tokens[None].astype(x.dtype), (B, N_META, D_MODEL))
    h = jnp.concatenate([meta, x], axis=1)
    k_sh = v_sh = cum_sh = None
    for layer in range(DEPTH):
        if layer < N_A:
            h = gla_layer(h, norm_g[layer], gla_w_in[layer], gla_w_gate_up[layer], gla_b_gate[layer],
                          gla_head_norm_g[layer], gla_w_out[layer])
        else:
            if layer == N_A:
                k_sh, v_sh, cum_sh = fox_shared_kv(h, kv_norm_g, fox_w_kv, fox_b_forget)
            j = layer - N_A
            h = fox_layer(h, norm_g[layer], fox_w_in[j], fox_w_out[j], k_sh, v_sh, cum_sh)
    h = rmsnorm(h, final_norm_g)
    return h[:, N_META:]
```

```python
import functools

import jax
import jax.numpy as jnp
from jax import lax
from jax.experimental import pallas as pl
from jax.experimental.pallas import tpu as pltpu

f32 = jnp.float32
bf16 = jnp.bfloat16

D_MODEL = 1024
BATCH = 8
SEQ = 2048
N_META = 16
D_INNER = 2048
GLA_HEADS = 4
GLA_HEAD_K = 256
GLA_HEAD_V = 512
GLA_DK_TOTAL = GLA_HEADS * GLA_HEAD_K
GLA_GATE_RANK = 16
GLA_GATE_NORMALIZER = 16.0
FOX_HEADS = 16
FOX_HEAD_DIM = 128
RMS_EPS = 1e-6
NEG_INF = -1e30

LANES = 128
ROWS_X = BATCH * SEQ
ROWS = ROWS_X + LANES
META_BLOCK = BATCH
VMEM_LIMIT = 56 * 1024 * 1024

GLA_CHUNK = 128
GLA_BASE = 32
FOX_TQ = 512
FOX_TK = 512
ROW_TILE = 688
OUT_TILE = 512


def _seq_block(t):
    return jnp.where(t == 0, META_BLOCK, t - 1)


def _log_sigmoid(x):
    return jnp.minimum(x, 0.0) - jnp.log1p(jnp.exp(-jnp.abs(x)))


def _silu(x):
    return x * (1.0 / (1.0 + jnp.exp(-x)))


def _split_dot(a, b, pieces, split_rhs):
    x = b if split_rhs else a
    acc = None
    for _ in range(pieces):
        p = x.astype(bf16)
        term = (jnp.dot(a, p, preferred_element_type=f32) if split_rhs
                else jnp.dot(p, b, preferred_element_type=f32))
        acc = term if acc is None else acc + term
        x = x - p.astype(f32)
    return acc


def _norm_matmul_kernel(x_ref, g_ref, w_ref, ws_ref, o_ref, os_ref, xn_ref):
    @pl.when(pl.program_id(1) == 0)
    def _():
        x = x_ref[...]
        var = jnp.mean(x * x, axis=-1, keepdims=True)
        xn = (x * lax.rsqrt(var + RMS_EPS) * g_ref[...]).astype(bf16)
        xn_ref[...] = xn
        os_ref[...] = jnp.dot(xn, ws_ref[...], preferred_element_type=f32)

    o_ref[...] = jnp.dot(xn_ref[...], w_ref[...], preferred_element_type=f32).astype(o_ref.dtype)


def _norm_matmul(h, g, w, w_small, tn):
    rows, d = h.shape
    n = w.shape[1]
    assert rows % ROW_TILE == 0 and n % tn == 0
    return pl.pallas_call(
        _norm_matmul_kernel,
        grid=(rows // ROW_TILE, n // tn),
        in_specs=[
            pl.BlockSpec((ROW_TILE, d), lambda i, j: (i, 0)),
            pl.BlockSpec((1, d), lambda i, j: (0, 0)),
            pl.BlockSpec((d, tn), lambda i, j: (0, j)),
            pl.BlockSpec((d, LANES), lambda i, j: (0, 0)),
        ],
        out_specs=[
            pl.BlockSpec((ROW_TILE, tn), lambda i, j: (i, j)),
            pl.BlockSpec((ROW_TILE, LANES), lambda i, j: (i, 0)),
        ],
        out_shape=[
            jax.ShapeDtypeStruct((rows, n), bf16),
            jax.ShapeDtypeStruct((rows, LANES), f32),
        ],
        scratch_shapes=[pltpu.VMEM((ROW_TILE, d), bf16)],
        compiler_params=pltpu.CompilerParams(
            dimension_semantics=("arbitrary", "arbitrary"), vmem_limit_bytes=VMEM_LIMIT),
        name="norm_matmul",
    )(h, g.reshape(1, d), w, w_small)


def _out_proj_kernel(y_ref, w_ref, h_ref, o_ref):
    o_ref[...] = h_ref[...] + jnp.dot(y_ref[...], w_ref[...], preferred_element_type=f32)


def _out_proj_final_kernel(y_ref, w_ref, h_ref, g_ref, o_ref):
    x = h_ref[...] + jnp.dot(y_ref[...], w_ref[...], preferred_element_type=f32)
    var = jnp.mean(x * x, axis=-1, keepdims=True)
    o_ref[...] = x * lax.rsqrt(var + RMS_EPS) * g_ref[...]


def _out_proj(y, w, h, final_g=None):
    k, d = w.shape
    if final_g is None:
        rows, tile, body, extra, extra_specs = h.shape[0], ROW_TILE, _out_proj_kernel, (), []
    else:
        rows, tile, body = ROWS_X, OUT_TILE, _out_proj_final_kernel
        extra, extra_specs = (final_g.reshape(1, d),), [pl.BlockSpec((1, d), lambda i: (0, 0))]
    return pl.pallas_call(
        body,
        grid=(rows // tile,),
        in_specs=[
            pl.BlockSpec((tile, k), lambda i: (i, 0)),
            pl.BlockSpec((k, d), lambda i: (0, 0)),
            pl.BlockSpec((tile, d), lambda i: (i, 0)),
        ] + extra_specs,
        out_specs=pl.BlockSpec((tile, d), lambda i: (i, 0)),
        out_shape=jax.ShapeDtypeStruct((rows, d), f32),
        compiler_params=pltpu.CompilerParams(
            dimension_semantics=("arbitrary",), vmem_limit_bytes=VMEM_LIMIT),
        name="out_proj" if final_g is None else "out_proj_final",
    )(y, w, h, *extra)


def _gla_exponent_matrix(c):
    i = lax.broadcasted_iota(jnp.int32, (c, c), 0)
    j = lax.broadcasted_iota(jnp.int32, (c, c), 1)
    tril = (j <= i).astype(f32)
    mats = [tril]
    size = min(c, GLA_BASE)
    while size <= c:
        anchor = (i & -size) + (size // 2 - 1)
        mats.append(tril - (j <= anchor).astype(f32))
        size *= 2
    return jnp.concatenate(mats, axis=0).astype(bf16)


def _gla_chunk(c, expo, q, k, v, log_a, state_t):
    e = _split_dot(expo, log_a, 2, split_rhs=True)
    b = e[0:c]
    b_last = b[c - 1:c]
    i = lax.broadcasted_iota(jnp.int32, (c, c), 0)
    j = lax.broadcasted_iota(jnp.int32, (c, c), 1)
    row = lax.broadcasted_iota(jnp.int32, (c, 1), 0)
    nt = (((1,), (1,)), ((), ()))

    base = min(c, GLA_BASE)
    e0 = e[c:2 * c]
    q0 = (q * jnp.exp(e0)).astype(bf16)
    k0 = (k * jnp.exp(-e0)).astype(bf16)
    p = lax.dot_general(q0, k0, nt, preferred_element_type=f32)
    attn = jnp.where(((i & -base) == (j & -base)) & (j <= i), p, 0.0)
    size, level = 2 * base, 2
    while size <= c:
        half = size // 2
        el = e[level * c:(level + 1) * c]
        upper = (row & half) != 0
        qk = (jnp.where(upper, q, k) * jnp.exp(-jnp.abs(el))).astype(bf16)
        p = lax.dot_general(qk, qk, nt, preferred_element_type=f32)
        use = ((i & -size) == (j & -size)) & ((i & half) != 0) & ((j & half) == 0)
        attn = jnp.where(use, p, attn)
        size, level = 2 * size, level + 1

    q_dec = (q * jnp.exp(b)).astype(bf16)
    k_rem = (k * jnp.exp(b_last - b)).astype(bf16)
    o = jnp.dot(attn.astype(bf16), v, preferred_element_type=f32)
    o = o + lax.dot_general(q_dec, state_t.astype(bf16), nt, preferred_element_type=f32)
    tn = (((0,), (0,)), ((), ()))
    new_state_t = state_t * jnp.exp(b_last) + lax.dot_general(v, k_rem, tn, preferred_element_type=f32)
    return o, new_state_t


def _gla_kernel(q_ref, k_ref, v_ref, r_ref, g_ref, wup_ref, bg_ref, hn_ref, y_ref, st_ref, st0_ref):
    t = pl.program_id(1)
    wup = wup_ref[...]
    bg = bg_ref[...]
    hn = hn_ref[...]

    def run_chunk(rows, c, expo, state_t):
        q = q_ref[rows, :].astype(f32)
        k = k_ref[rows, :].astype(f32)
        logits = jnp.dot(g_ref[rows, :].astype(bf16), wup, preferred_element_type=f32) + bg
        log_a = _log_sigmoid(logits) * (1.0 / GLA_GATE_NORMALIZER)
        o, new_state_t = _gla_chunk(c, expo, q, k, v_ref[rows, :], log_a, state_t)
        var = jnp.mean(o * o, axis=-1, keepdims=True)
        y = o * lax.rsqrt(var + RMS_EPS) * hn * _silu(r_ref[rows, :].astype(f32))
        y_ref[rows, :] = y.astype(y_ref.dtype)
        return new_state_t

    @pl.when(t == 0)
    def _():
        expo = _gla_exponent_matrix(N_META)
        zero_state = jnp.zeros((GLA_HEAD_V, GLA_HEAD_K), f32)
        st0_ref[...] = run_chunk(pl.ds(0, N_META), N_META, expo, zero_state)
        y_ref[N_META:LANES, :] = jnp.zeros((LANES - N_META, GLA_HEAD_V), y_ref.dtype)

    @pl.when(t > 0)
    def _():
        expo = _gla_exponent_matrix(GLA_CHUNK)
        st_ref[...] = st0_ref[...]

        def body(ci, carry):
            rows = pl.ds(pl.multiple_of(ci * GLA_CHUNK, GLA_CHUNK), GLA_CHUNK)
            st_ref[...] = run_chunk(rows, GLA_CHUNK, expo, st_ref[...])
            return carry

        lax.fori_loop(0, SEQ // GLA_CHUNK, body, 0)


def _gla_mix(proj, g_low, w_up, b_gate, head_norm_g):
    kq = GLA_DK_TOTAL // GLA_HEAD_K
    kv = 2 * GLA_DK_TOTAL // GLA_HEAD_V
    kr = kv + D_INNER // GLA_HEAD_V
    return pl.pallas_call(
        _gla_kernel,
        grid=(GLA_HEADS, BATCH + 1),
        in_specs=[
            pl.BlockSpec((SEQ, GLA_HEAD_K), lambda h, t: (_seq_block(t), h)),
            pl.BlockSpec((SEQ, GLA_HEAD_K), lambda h, t: (_seq_block(t), kq + h)),
            pl.BlockSpec((SEQ, GLA_HEAD_V), lambda h, t: (_seq_block(t), kv + h)),
            pl.BlockSpec((SEQ, GLA_HEAD_V), lambda h, t: (_seq_block(t), kr + h)),
            pl.BlockSpec((SEQ, LANES), lambda h, t: (_seq_block(t), 0)),
            pl.BlockSpec((LANES, GLA_HEAD_K), lambda h, t: (0, h)),
            pl.BlockSpec((1, GLA_HEAD_K), lambda h, t: (0, h)),
            pl.BlockSpec((1, GLA_HEAD_V), lambda h, t: (0, 0)),
        ],
        out_specs=pl.BlockSpec((SEQ, GLA_HEAD_V), lambda h, t: (_seq_block(t), h)),
        out_shape=jax.ShapeDtypeStruct((ROWS, D_INNER), bf16),
        scratch_shapes=[pltpu.VMEM((GLA_HEAD_V, GLA_HEAD_K), f32),
                        pltpu.VMEM((GLA_HEAD_V, GLA_HEAD_K), f32)],
        compiler_params=pltpu.CompilerParams(
            dimension_semantics=("arbitrary", "arbitrary"), vmem_limit_bytes=VMEM_LIMIT),
        name="gla_mix",
    )(proj, proj, proj, proj, g_low, w_up, b_gate, head_norm_g)


def _cum_kernel(f_ref, b_ref, o_ref, carry_ref):
    t = pl.program_id(0)
    i = lax.broadcasted_iota(jnp.int32, (LANES, LANES), 0)
    j = lax.broadcasted_iota(jnp.int32, (LANES, LANES), 1)
    upper = (i <= j).astype(bf16)

    def block_cumsum(c, carry):
        log_f = _log_sigmoid(f_ref[c * LANES:(c + 1) * LANES, :] + b_ref[...])
        cs = _split_dot(log_f.T, upper, 3, split_rhs=False) + carry
        o_ref[:, 0, c * LANES:(c + 1) * LANES] = cs[0:FOX_HEADS, :]
        return cs

    @pl.when(t == 0)
    def _():
        cs = block_cumsum(0, jnp.zeros((LANES, 1), f32))
        carry_ref[...] = jnp.broadcast_to(cs[:, N_META - 1:N_META], (LANES, LANES))

    @pl.when(t > 0)
    def _():
        carry = carry_ref[:, 0:1]
        for c in range(SEQ // LANES):
            carry = block_cumsum(c, carry)[:, LANES - 1:LANES]


def _forget_cumsum(f_logit, b_forget):
    return pl.pallas_call(
        _cum_kernel,
        grid=(BATCH + 1,),
        in_specs=[
            pl.BlockSpec((SEQ, LANES), lambda t: (_seq_block(t), 0)),
            pl.BlockSpec((1, LANES), lambda t: (0, 0)),
        ],
        out_specs=pl.BlockSpec((FOX_HEADS, 1, SEQ), lambda t: (0, 0, _seq_block(t))),
        out_shape=jax.ShapeDtypeStruct((FOX_HEADS, 1, ROWS), f32),
        scratch_shapes=[pltpu.VMEM((LANES, LANES), f32)],
        compiler_params=pltpu.CompilerParams(dimension_semantics=("arbitrary",)),
        name="forget_cumsum",
    )(f_logit, b_forget)


def _softmax_step(q, k, v, bias, mask, m, l, acc):
    s = lax.dot_general(q, k, (((1,), (1,)), ((), ())), preferred_element_type=f32) + bias
    if mask is not None:
        s = jnp.where(mask, s, NEG_INF)
    if m is None:
        m_new = jnp.max(s, axis=-1, keepdims=True)
        p = jnp.exp(s - m_new)
        return m_new, jnp.sum(p, axis=-1, keepdims=True), jnp.dot(p.astype(bf16), v, preferred_element_type=f32)
    m_new = jnp.maximum(m, jnp.max(s, axis=-1, keepdims=True))
    alpha = jnp.exp(m - m_new)
    p = jnp.exp(s - m_new)
    l = alpha * l + jnp.sum(p, axis=-1, keepdims=True)
    acc = alpha * acc + jnp.dot(p.astype(bf16), v, preferred_element_type=f32)
    return m_new, l, acc


def _fox_kernel(q_ref, r_ref, k_ref, v_ref, c_ref, y_ref, km_ref, vm_ref, bm_ref):
    t = pl.program_id(1)

    def finish(rows, l, acc):
        y = acc * (1.0 / l) * _silu(r_ref[rows, :].astype(f32))
        y_ref[rows, :] = y.astype(y_ref.dtype)

    @pl.when(t == 0)
    def _():
        km_ref[...] = k_ref[0:LANES, :]
        vm_ref[...] = v_ref[0:LANES, :]
        lane = lax.broadcasted_iota(jnp.int32, (1, LANES), 1)
        bm_ref[...] = jnp.where(lane < N_META, -c_ref[0, :, 0:LANES], NEG_INF)
        i = lax.broadcasted_iota(jnp.int32, (LANES, LANES), 0)
        j = lax.broadcasted_iota(jnp.int32, (LANES, LANES), 1)
        mask = (j <= i) | (i >= N_META)
        _, l, acc = _softmax_step(q_ref[0:LANES, :], km_ref[...], vm_ref[...], bm_ref[...], mask,
                                  None, None, None)
        finish(pl.ds(0, LANES), l, acc)

    @pl.when(t > 0)
    def _():
        i = lax.broadcasted_iota(jnp.int32, (FOX_TQ, FOX_TK), 0)
        j = lax.broadcasted_iota(jnp.int32, (FOX_TQ, FOX_TK), 1)
        causal = j <= i
        for qi in range(SEQ // FOX_TQ):
            rows = pl.ds(qi * FOX_TQ, FOX_TQ)
            q = q_ref[rows, :]
            state = _softmax_step(q, km_ref[...], vm_ref[...], bm_ref[...], None, None, None, None)

            def body(kj, state):
                cols = pl.ds(pl.multiple_of(kj * FOX_TK, FOX_TK), FOX_TK)
                return _softmax_step(q, k_ref[cols, :], v_ref[cols, :], -c_ref[0, :, cols], None, *state)

            state = lax.fori_loop(0, qi, body, state)
            cols = pl.ds(qi * FOX_TK, FOX_TK)
            _, l, acc = _softmax_step(q, k_ref[cols, :], v_ref[cols, :], -c_ref[0, :, cols], causal, *state)
            finish(rows, l, acc)


def _fox_mix(qr, kv, cum_t):
    nh = FOX_HEADS
    return pl.pallas_call(
        _fox_kernel,
        grid=(FOX_HEADS, BATCH + 1),
        in_specs=[
            pl.BlockSpec((SEQ, FOX_HEAD_DIM), lambda h, t: (_seq_block(t), h)),
            pl.BlockSpec((SEQ, FOX_HEAD_DIM), lambda h, t: (_seq_block(t), nh + h)),
            pl.BlockSpec((SEQ, FOX_HEAD_DIM), lambda h, t: (_seq_block(t), h)),
            pl.BlockSpec((SEQ, FOX_HEAD_DIM), lambda h, t: (_seq_block(t), nh + h)),
            pl.BlockSpec((1, 1, SEQ), lambda h, t: (h, 0, _seq_block(t))),
        ],
        out_specs=pl.BlockSpec((SEQ, FOX_HEAD_DIM), lambda h, t: (_seq_block(t), h)),
        out_shape=jax.ShapeDtypeStruct((ROWS, D_INNER), bf16),
        scratch_shapes=[pltpu.VMEM((LANES, FOX_HEAD_DIM), bf16),
                        pltpu.VMEM((LANES, FOX_HEAD_DIM), bf16),
                        pltpu.VMEM((1, LANES), f32)],
        compiler_params=pltpu.CompilerParams(
            dimension_semantics=("arbitrary", "arbitrary"), vmem_limit_bytes=VMEM_LIMIT),
        name="fox_mix",
    )(qr, qr, kv, kv, cum_t)


def _pad_cols(w):
    return jnp.pad(w, ((0, 0), (0, LANES - w.shape[1])))


def kernel(x, meta_tokens, norm_g, gla_w_in, gla_w_gate_up, gla_b_gate, gla_head_norm_g, gla_w_out,
           kv_norm_g, fox_w_kv, fox_b_forget, fox_w_in, fox_w_out, final_norm_g):
    n_gla = gla_w_in.shape[0]
    n_fox = fox_w_in.shape[0]
    h = jnp.concatenate([x.reshape(ROWS_X, D_MODEL), meta_tokens,
                         jnp.zeros((ROWS - ROWS_X - N_META, D_MODEL), f32)], axis=0)

    o_g = 2 * GLA_DK_TOTAL + D_INNER
    for layer in range(n_gla):
        w = gla_w_in[layer]
        q_scale = jnp.concatenate([jnp.full((GLA_DK_TOTAL,), GLA_HEAD_K ** -0.5, f32),
                                   jnp.ones((o_g - GLA_DK_TOTAL + D_INNER,), f32)])
        w_main = (jnp.concatenate([w[:, :o_g], w[:, o_g + GLA_GATE_RANK:]], axis=1) * q_scale).astype(bf16)
        w_low = _pad_cols(w[:, o_g:o_g + GLA_GATE_RANK]).astype(bf16)
        proj, g_low = _norm_matmul(h, norm_g[layer], w_main, w_low, 1536)
        w_up = jnp.pad(gla_w_gate_up[layer], ((0, LANES - GLA_GATE_RANK), (0, 0))).astype(bf16)
        y = _gla_mix(proj, g_low, w_up, gla_b_gate[layer].reshape(1, -1),
                     gla_head_norm_g[layer].reshape(1, -1))
        h = _out_proj(y, gla_w_out[layer].astype(bf16), h)

    kv, f_logit = _norm_matmul(h, kv_norm_g, fox_w_kv[:, :2 * D_INNER].astype(bf16),
                               _pad_cols(fox_w_kv[:, 2 * D_INNER:]).astype(bf16), 1024)
    cum_t = _forget_cumsum(f_logit, _pad_cols(fox_b_forget.reshape(1, -1)))

    zero_small = jnp.zeros((D_MODEL, LANES), bf16)
    for j in range(n_fox):
        layer = n_gla + j
        q_scale = jnp.concatenate([jnp.full((D_INNER,), FOX_HEAD_DIM ** -0.5, f32), jnp.ones((D_INNER,), f32)])
        qr, _ = _norm_matmul(h, norm_g[layer], (fox_w_in[j] * q_scale).astype(bf16), zero_small, 1024)
        y = _fox_mix(qr, kv, cum_t)
        if j + 1 < n_fox:
            h = _out_proj(y, fox_w_out[j].astype(bf16), h)
        else:
            h = _out_proj(y, fox_w_out[j].astype(bf16), h, final_g=final_norm_g)
    return h.reshape(BATCH, SEQ, D_MODEL)
```

```python
import functools

import jax
import jax.numpy as jnp
from jax import lax
from jax.experimental import pallas as pl
from jax.experimental.pallas import tpu as pltpu

f32 = jnp.float32
bf16 = jnp.bfloat16

D_MODEL = 1024
BATCH = 8
SEQ = 2048
N_META = 16
D_INNER = 2048
GLA_HEADS = 4
GLA_HEAD_K = 256
GLA_HEAD_V = 512
GLA_DK_TOTAL = GLA_HEADS * GLA_HEAD_K
GLA_GATE_RANK = 16
GLA_GATE_NORMALIZER = 16.0
FOX_HEADS = 16
FOX_HEAD_DIM = 128
RMS_EPS = 1e-6
NEG_INF = -1e30
LOG2E = 1.4426950408889634

LANES = 128
ROWS_X = BATCH * SEQ
ROWS = ROWS_X + LANES
META_BLOCK = BATCH
VMEM_LIMIT = 56 * 1024 * 1024

GLA_CHUNK = 128
GLA_BASE = 32
GLA_GROUP = 4
FOX_TQ = 256
KEYS = 2304
ROW_TILE = 688
OUT_TILE = 512


def _seq_block(t):
    return jnp.where(t == 0, META_BLOCK, t - 1)


def _log_sigmoid(x):
    return jnp.minimum(x, 0.0) - jnp.log1p(jnp.exp(-jnp.abs(x)))


def _silu(x):
    return x * (1.0 / (1.0 + jnp.exp(-x)))


def _split_dot(a, b, pieces, split_rhs):
    x = b if split_rhs else a
    acc = None
    for _ in range(pieces):
        p = x.astype(bf16)
        term = (jnp.dot(a, p, preferred_element_type=f32) if split_rhs
                else jnp.dot(p, b, preferred_element_type=f32))
        acc = term if acc is None else acc + term
        x = x - p.astype(f32)
    return acc


def _norm_matmul_kernel(x_ref, g_ref, w_ref, ws_ref, o_ref, os_ref, xn_ref):
    @pl.when(pl.program_id(1) == 0)
    def _():
        x = x_ref[...]
        var = jnp.mean(x * x, axis=-1, keepdims=True)
        xn = (x * lax.rsqrt(var + RMS_EPS) * g_ref[...]).astype(bf16)
        xn_ref[...] = xn
        os_ref[...] = jnp.dot(xn, ws_ref[...], preferred_element_type=f32)

    o_ref[...] = jnp.dot(xn_ref[...], w_ref[...], preferred_element_type=f32).astype(o_ref.dtype)


def _norm_matmul(h, g, w, w_small, tn):
    rows, d = h.shape
    n = w.shape[1]
    assert rows % ROW_TILE == 0 and n % tn == 0
    return pl.pallas_call(
        _norm_matmul_kernel,
        grid=(rows // ROW_TILE, n // tn),
        in_specs=[
            pl.BlockSpec((ROW_TILE, d), lambda i, j: (i, 0)),
            pl.BlockSpec((1, d), lambda i, j: (0, 0)),
            pl.BlockSpec((d, tn), lambda i, j: (0, j)),
            pl.BlockSpec((d, LANES), lambda i, j: (0, 0)),
        ],
        out_specs=[
            pl.BlockSpec((ROW_TILE, tn), lambda i, j: (i, j)),
            pl.BlockSpec((ROW_TILE, LANES), lambda i, j: (i, 0)),
        ],
        out_shape=[
            jax.ShapeDtypeStruct((rows, n), bf16),
            jax.ShapeDtypeStruct((rows, LANES), f32),
        ],
        scratch_shapes=[pltpu.VMEM((ROW_TILE, d), bf16)],
        compiler_params=pltpu.CompilerParams(
            dimension_semantics=("arbitrary", "arbitrary"), vmem_limit_bytes=VMEM_LIMIT),
        name="norm_matmul",
    )(h, g.reshape(1, d), w, w_small)


def _out_proj_kernel(y_ref, w_ref, h_ref, o_ref):
    o_ref[...] = h_ref[...] + jnp.dot(y_ref[...], w_ref[...], preferred_element_type=f32)


def _out_proj_final_kernel(y_ref, w_ref, h_ref, g_ref, o_ref):
    x = h_ref[...] + jnp.dot(y_ref[...], w_ref[...], preferred_element_type=f32)
    var = jnp.mean(x * x, axis=-1, keepdims=True)
    o_ref[...] = x * lax.rsqrt(var + RMS_EPS) * g_ref[...]


def _out_proj(y, w, h, final_g=None):
    k, d = w.shape
    if final_g is None:
        rows, tile, body, extra, extra_specs = h.shape[0], ROW_TILE, _out_proj_kernel, (), []
    else:
        rows, tile, body = ROWS_X, OUT_TILE, _out_proj_final_kernel
        extra, extra_specs = (final_g.reshape(1, d),), [pl.BlockSpec((1, d), lambda i: (0, 0))]
    return pl.pallas_call(
        body,
        grid=(rows // tile,),
        in_specs=[
            pl.BlockSpec((tile, k), lambda i: (i, 0)),
            pl.BlockSpec((k, d), lambda i: (0, 0)),
            pl.BlockSpec((tile, d), lambda i: (i, 0)),
        ] + extra_specs,
        out_specs=pl.BlockSpec((tile, d), lambda i: (i, 0)),
        out_shape=jax.ShapeDtypeStruct((rows, d), f32),
        compiler_params=pltpu.CompilerParams(
            dimension_semantics=("arbitrary",), vmem_limit_bytes=VMEM_LIMIT),
        name="out_proj" if final_g is None else "out_proj_final",
    )(y, w, h, *extra)


def _gla_out_kernel(o_ref, r_ref, hn_ref, w_ref, h_ref, out_ref):
    acc = h_ref[...]
    hn = hn_ref[...]
    for hd in range(GLA_HEADS):
        cols = slice(hd * GLA_HEAD_V, (hd + 1) * GLA_HEAD_V)
        o = o_ref[:, cols].astype(f32)
        var = jnp.mean(o * o, axis=-1, keepdims=True)
        y = o * lax.rsqrt(var + RMS_EPS) * hn * _silu(r_ref[:, cols].astype(f32))
        acc = acc + jnp.dot(y.astype(bf16), w_ref[cols, :], preferred_element_type=f32)
    out_ref[...] = acc


def _gla_out_proj(o, proj, head_norm_g, w, h):
    k, d = w.shape
    rows = h.shape[0]
    return pl.pallas_call(
        _gla_out_kernel,
        grid=(rows // ROW_TILE,),
        in_specs=[
            pl.BlockSpec((ROW_TILE, k), lambda i: (i, 0)),
            pl.BlockSpec((ROW_TILE, k), lambda i: (i, proj.shape[1] // k - 1)),
            pl.BlockSpec((1, GLA_HEAD_V), lambda i: (0, 0)),
            pl.BlockSpec((k, d), lambda i: (0, 0)),
            pl.BlockSpec((ROW_TILE, d), lambda i: (i, 0)),
        ],
        out_specs=pl.BlockSpec((ROW_TILE, d), lambda i: (i, 0)),
        out_shape=jax.ShapeDtypeStruct((rows, d), f32),
        compiler_params=pltpu.CompilerParams(
            dimension_semantics=("arbitrary",), vmem_limit_bytes=VMEM_LIMIT),
        name="gla_out_proj",
    )(o, proj, head_norm_g, w, h)


def _gla_group(nc, c, q, k, v, log_a, state_t, emit):
    n, dk = q.shape
    nt = (((1,), (1,)), ((), ()))
    tn = (((0,), (0,)), ((), ()))
    i = lax.broadcasted_iota(jnp.int32, (c, c), 0)
    j = lax.broadcasted_iota(jnp.int32, (c, c), 1)
    tril = (j <= i).astype(bf16)
    row = lax.broadcasted_iota(jnp.int32, (n, 1), 0)

    b = jnp.concatenate([_split_dot(tril, log_a[ci * c:(ci + 1) * c], 2, split_rhs=True)
                         for ci in range(nc)], axis=0)

    def minus_anchor(size):
        b3 = b.reshape(n // size, size, dk)
        return (b3 - b3[:, size // 2 - 1:size // 2, :]).reshape(n, dk)

    base = min(c, GLA_BASE)
    e0 = minus_anchor(base)
    operands = [((q * jnp.exp(e0)).astype(bf16), (k * jnp.exp(-e0)).astype(bf16),
                 ((i & -base) == (j & -base)) & (j <= i))]
    size = 2 * base
    while size <= c:
        half = size // 2
        qk = (jnp.where((row & half) != 0, q, k) * jnp.exp(-jnp.abs(minus_anchor(size)))).astype(bf16)
        operands.append((qk, qk, ((i & -size) == (j & -size)) & ((i & half) != 0) & ((j & half) == 0)))
        size *= 2

    b3 = b.reshape(nc, c, dk)
    b_last = b3[:, c - 1:c, :]
    q_dec = (q * jnp.exp(b)).astype(bf16)
    k_rem = (k.reshape(nc, c, dk) * jnp.exp(b_last - b3)).reshape(n, dk).astype(bf16)
    decay = jnp.exp(b_last)

    o_intra, update = [], []
    for ci in range(nc):
        rows = slice(ci * c, (ci + 1) * c)
        attn = None
        for lhs, rhs, use in operands:
            p = lax.dot_general(lhs[rows], rhs[rows], nt, preferred_element_type=f32)
            attn = jnp.where(use, p, 0.0 if attn is None else attn)
        o_intra.append(jnp.dot(attn.astype(bf16), v[rows], preferred_element_type=f32))
        update.append(lax.dot_general(v[rows], k_rem[rows], tn, preferred_element_type=f32))

    for ci in range(nc):
        rows = slice(ci * c, (ci + 1) * c)
        emit(ci, o_intra[ci] + lax.dot_general(q_dec[rows], state_t.astype(bf16), nt,
                                               preferred_element_type=f32))
        state_t = state_t * decay[ci] + update[ci]
    return state_t


def _gla_kernel(q_ref, k_ref, v_ref, g_ref, wup_ref, bg_ref, o_ref, st_ref, st0_ref):
    t = pl.program_id(1)
    wup = wup_ref[...]
    bg = bg_ref[...]

    def run_group(row0, nc, c, state_t):
        rows = pl.ds(row0, nc * c)
        logits = jnp.dot(g_ref[rows, :].astype(bf16), wup, preferred_element_type=f32) + bg
        log_a = _log_sigmoid(logits) * (1.0 / GLA_GATE_NORMALIZER)

        def emit(ci, o):
            o_ref[pl.ds(row0 + ci * c, c), :] = o.astype(o_ref.dtype)

        return _gla_group(nc, c, q_ref[rows, :].astype(f32), k_ref[rows, :].astype(f32),
                          v_ref[rows, :], log_a, state_t, emit)

    @pl.when(t == 0)
    def _():
        zero_state = jnp.zeros((GLA_HEAD_V, GLA_HEAD_K), f32)
        st0_ref[...] = run_group(0, 1, N_META, zero_state)
        o_ref[N_META:LANES, :] = jnp.zeros((LANES - N_META, GLA_HEAD_V), o_ref.dtype)

    @pl.when(t > 0)
    def _():
        st_ref[...] = st0_ref[...]
        group_rows = GLA_GROUP * GLA_CHUNK

        def body(gi, carry):
            row0 = pl.multiple_of(gi * group_rows, group_rows)
            st_ref[...] = run_group(row0, GLA_GROUP, GLA_CHUNK, st_ref[...])
            return carry

        lax.fori_loop(0, SEQ // group_rows, body, 0)


def _gla_mix(proj, g_low, w_up, b_gate):
    kq = GLA_DK_TOTAL // GLA_HEAD_K
    kv = 2 * GLA_DK_TOTAL // GLA_HEAD_V
    return pl.pallas_call(
        _gla_kernel,
        grid=(GLA_HEADS, BATCH + 1),
        in_specs=[
            pl.BlockSpec((SEQ, GLA_HEAD_K), lambda h, t: (_seq_block(t), h)),
            pl.BlockSpec((SEQ, GLA_HEAD_K), lambda h, t: (_seq_block(t), kq + h)),
            pl.BlockSpec((SEQ, GLA_HEAD_V), lambda h, t: (_seq_block(t), kv + h)),
            pl.BlockSpec((SEQ, LANES), lambda h, t: (_seq_block(t), 0)),
            pl.BlockSpec((LANES, GLA_HEAD_K), lambda h, t: (0, h)),
            pl.BlockSpec((1, GLA_HEAD_K), lambda h, t: (0, h)),
        ],
        out_specs=pl.BlockSpec((SEQ, GLA_HEAD_V), lambda h, t: (_seq_block(t), h)),
        out_shape=jax.ShapeDtypeStruct((ROWS, D_INNER), bf16),
        scratch_shapes=[pltpu.VMEM((GLA_HEAD_V, GLA_HEAD_K), f32),
                        pltpu.VMEM((GLA_HEAD_V, GLA_HEAD_K), f32)],
        compiler_params=pltpu.CompilerParams(
            dimension_semantics=("arbitrary", "arbitrary"), vmem_limit_bytes=VMEM_LIMIT),
        name="gla_mix",
    )(proj, proj, proj, g_low, w_up, b_gate)


def _bias_kernel(f_ref, b_ref, o_ref, meta_ref):
    t = pl.program_id(0)
    i = lax.broadcasted_iota(jnp.int32, (LANES, LANES), 0)
    j = lax.broadcasted_iota(jnp.int32, (LANES, LANES), 1)
    upper = (i <= j).astype(bf16)

    def block_cumsum(c, carry):
        log_f = _log_sigmoid(f_ref[c * LANES:(c + 1) * LANES, :] + b_ref[...])
        return _split_dot(log_f.T, upper, 3, split_rhs=False) + carry

    @pl.when(t == 0)
    def _():
        meta_ref[...] = block_cumsum(0, jnp.zeros((LANES, 1), f32))

    @pl.when(t > 0)
    def _():
        meta = meta_ref[...]
        o_ref[:, 0, 0, 0:N_META] = meta[0:FOX_HEADS, 0:N_META] * -LOG2E
        carry = meta[:, N_META - 1:N_META]
        for c in range(SEQ // LANES):
            cs = block_cumsum(c, carry)
            o_ref[:, 0, 0, N_META + c * LANES:N_META + (c + 1) * LANES] = cs[0:FOX_HEADS, :] * -LOG2E
            carry = cs[:, LANES - 1:LANES]
        o_ref[:, 0, 0, N_META + SEQ:KEYS] = jnp.zeros((FOX_HEADS, KEYS - N_META - SEQ), f32)


def _forget_bias(f_logit, b_forget):
    return pl.pallas_call(
        _bias_kernel,
        grid=(BATCH + 1,),
        in_specs=[
            pl.BlockSpec((SEQ, LANES), lambda t: (_seq_block(t), 0)),
            pl.BlockSpec((1, LANES), lambda t: (0, 0)),
        ],
        out_specs=pl.BlockSpec((FOX_HEADS, 1, 1, KEYS), lambda t: (0, jnp.maximum(t - 1, 0), 0, 0)),
        out_shape=jax.ShapeDtypeStruct((FOX_HEADS, BATCH, 1, KEYS), f32),
        scratch_shapes=[pltpu.VMEM((LANES, LANES), f32)],
        compiler_params=pltpu.CompilerParams(dimension_semantics=("arbitrary",)),
        name="forget_bias",
    )(f_logit, b_forget)


def _scores(q, k, bias, mask):
    s = lax.dot_general(q, k, (((1,), (1,)), ((), ())), preferred_element_type=f32) + bias
    return s if mask is None else jnp.where(mask, s, NEG_INF)


def _softmax_pv(scores, values):
    m = functools.reduce(jnp.maximum, [jnp.max(s, axis=-1, keepdims=True) for s in scores])
    acc = None
    for s, v in zip(scores, values):
        pv = jnp.dot(jnp.exp2(s - m).astype(bf16), v, preferred_element_type=f32)
        acc = pv if acc is None else acc + pv
    return acc[:, 0:FOX_HEAD_DIM] * (1.0 / acc[:, FOX_HEAD_DIM:])


def _fox_kernel(q_ref, r_ref, k_ref, v_ref, b_ref, y_ref, kc_ref, vc_ref):
    t = pl.program_id(1)
    dh = FOX_HEAD_DIM

    def lower_tri(nq, nk, shift):
        i = lax.broadcasted_iota(jnp.int32, (nq, nk), 0)
        j = lax.broadcasted_iota(jnp.int32, (nq, nk), 1)
        return j <= i + shift

    def finish(rows, o):
        y_ref[rows, :] = (o * _silu(r_ref[rows, :].astype(f32))).astype(y_ref.dtype)

    @pl.when(t == 0)
    def _():
        vc_ref[:, dh:] = jnp.ones((KEYS, dh), bf16)
        kc_ref[0:N_META, :] = k_ref[0:N_META, :]
        vc_ref[0:N_META, 0:dh] = v_ref[0:N_META, :]
        s = _scores(q_ref[0:N_META, :], kc_ref[0:N_META, :], b_ref[0, 0, :, 0:N_META],
                    lower_tri(N_META, N_META, 0))
        finish(pl.ds(0, N_META), _softmax_pv([s], [vc_ref[0:N_META, :]]))
        y_ref[N_META:LANES, :] = jnp.zeros((LANES - N_META, dh), y_ref.dtype)

    @pl.when(t > 0)
    def _():
        kc_ref[N_META:N_META + SEQ, :] = k_ref[...]
        vc_ref[N_META:N_META + SEQ, 0:dh] = v_ref[...]
        n_keys = N_META + SEQ

        def tile_scores(ti):
            k0 = ti * FOX_TQ
            k1 = min(k0 + FOX_TQ, n_keys)
            q0 = max(k0 - N_META, 0)
            q1 = k1 - N_META
            q = q_ref[q0:q1, :]
            scores, values = [], []
            if k0 > 0:
                scores.append(_scores(q, kc_ref[0:k0, :], b_ref[0, 0, :, 0:k0], None))
                values.append(vc_ref[0:k0, :])
            shift = q0 + N_META - k0
            scores.append(_scores(q, kc_ref[k0:k1, :], b_ref[0, 0, :, k0:k1],
                                  lower_tri(q1 - q0, k1 - k0, shift)))
            values.append(vc_ref[k0:k1, :])
            return pl.ds(q0, q1 - q0), scores, values

        n_tiles = -(-n_keys // FOX_TQ)
        pending = tile_scores(0)
        for ti in range(n_tiles):
            rows, scores, values = pending
            if ti + 1 < n_tiles:
                pending = tile_scores(ti + 1)
            finish(rows, _softmax_pv(scores, values))


def _fox_mix(qr, kv, bias):
    nh = FOX_HEADS
    return pl.pallas_call(
        _fox_kernel,
        grid=(FOX_HEADS, BATCH + 1),
        in_specs=[
            pl.BlockSpec((SEQ, FOX_HEAD_DIM), lambda h, t: (_seq_block(t), h)),
            pl.BlockSpec((SEQ, FOX_HEAD_DIM), lambda h, t: (_seq_block(t), nh + h)),
            pl.BlockSpec((SEQ, FOX_HEAD_DIM), lambda h, t: (_seq_block(t), h)),
            pl.BlockSpec((SEQ, FOX_HEAD_DIM), lambda h, t: (_seq_block(t), nh + h)),
            pl.BlockSpec((1, 1, 1, KEYS), lambda h, t: (h, jnp.maximum(t - 1, 0), 0, 0)),
        ],
        out_specs=pl.BlockSpec((SEQ, FOX_HEAD_DIM), lambda h, t: (_seq_block(t), h)),
        out_shape=jax.ShapeDtypeStruct((ROWS, D_INNER), bf16),
        scratch_shapes=[pltpu.VMEM((KEYS, FOX_HEAD_DIM), bf16),
                        pltpu.VMEM((KEYS, 2 * FOX_HEAD_DIM), bf16)],
        compiler_params=pltpu.CompilerParams(
            dimension_semantics=("arbitrary", "arbitrary"), vmem_limit_bytes=VMEM_LIMIT),
        name="fox_mix",
    )(qr, qr, kv, kv, bias)


def _pad_cols(w):
    return jnp.pad(w, ((0, 0), (0, LANES - w.shape[1])))


def kernel(x, meta_tokens, norm_g, gla_w_in, gla_w_gate_up, gla_b_gate, gla_head_norm_g, gla_w_out,
           kv_norm_g, fox_w_kv, fox_b_forget, fox_w_in, fox_w_out, final_norm_g):
    n_gla = gla_w_in.shape[0]
    n_fox = fox_w_in.shape[0]
    h = jnp.concatenate([x.reshape(ROWS_X, D_MODEL), meta_tokens,
                         jnp.zeros((ROWS - ROWS_X - N_META, D_MODEL), f32)], axis=0)

    o_g = 2 * GLA_DK_TOTAL + D_INNER
    for layer in range(n_gla):
        w = gla_w_in[layer]
        q_scale = jnp.concatenate([jnp.full((GLA_DK_TOTAL,), GLA_HEAD_K ** -0.5, f32),
                                   jnp.ones((o_g - GLA_DK_TOTAL + D_INNER,), f32)])
        w_main = (jnp.concatenate([w[:, :o_g], w[:, o_g + GLA_GATE_RANK:]], axis=1) * q_scale).astype(bf16)
        w_low = _pad_cols(w[:, o_g:o_g + GLA_GATE_RANK]).astype(bf16)
        proj, g_low = _norm_matmul(h, norm_g[layer], w_main, w_low, 1536)
        w_up = jnp.pad(gla_w_gate_up[layer], ((0, LANES - GLA_GATE_RANK), (0, 0))).astype(bf16)
        o = _gla_mix(proj, g_low, w_up, gla_b_gate[layer].reshape(1, -1))
        h = _gla_out_proj(o, proj, gla_head_norm_g[layer].reshape(1, -1), gla_w_out[layer].astype(bf16), h)

    kv, f_logit = _norm_matmul(h, kv_norm_g, fox_w_kv[:, :2 * D_INNER].astype(bf16),
                               _pad_cols(fox_w_kv[:, 2 * D_INNER:]).astype(bf16), 1024)
    bias = _forget_bias(f_logit, _pad_cols(fox_b_forget.reshape(1, -1)))

    zero_small = jnp.zeros((D_MODEL, LANES), bf16)
    for j in range(n_fox):
        layer = n_gla + j
        q_scale = jnp.concatenate([jnp.full((D_INNER,), FOX_HEAD_DIM ** -0.5 * LOG2E, f32), jnp.ones((D_INNER,), f32)])
        qr, _ = _norm_matmul(h, norm_g[layer], (fox_w_in[j] * q_scale).astype(bf16), zero_small, 1024)
        y = _fox_mix(qr, kv, bias)
        if j + 1 < n_fox:
            h = _out_proj(y, fox_w_out[j].astype(bf16), h)
        else:
            h = _out_proj(y, fox_w_out[j].astype(bf16), h, final_g=final_norm_g)
    return h.reshape(BATCH, SEQ, D_MODEL)
```

```python
import functools

import jax
import jax.numpy as jnp
from jax import lax
from jax.experimental import pallas as pl
from jax.experimental.pallas import tpu as pltpu

f32 = jnp.float32
bf16 = jnp.bfloat16

D_MODEL = 1024
BATCH = 8
SEQ = 2048
N_META = 16
D_INNER = 2048
GLA_HEADS = 4
GLA_HEAD_K = 256
GLA_HEAD_V = 512
GLA_DK_TOTAL = GLA_HEADS * GLA_HEAD_K
GLA_GATE_RANK = 16
GLA_GATE_NORMALIZER = 16.0
FOX_HEADS = 16
FOX_HEAD_DIM = 128
RMS_EPS = 1e-6
NEG_INF = -1e30
LOG2E = 1.4426950408889634

LANES = 128
ROWS_X = BATCH * SEQ
ROWS = ROWS_X + LANES
META_BLOCK = BATCH
VMEM_LIMIT = 56 * 1024 * 1024

GLA_CHUNK = 128
GLA_BASE = 32
GLA_GROUP = 4
MXU_COLS = 256
FOX_TQ = 256
KEYS = 2304
ROW_TILE = 688
PROJ_TILE = 1376
PROJ_COLS = 2048
OUT_TILE = 512
EMBED_TILE = 512


def _seq_block(t):
    return jnp.where(t == 0, META_BLOCK, t - 1)


def _log_sigmoid(x):
    return jnp.minimum(x, 0.0) - jnp.log1p(jnp.exp(-jnp.abs(x)))


def _silu(x):
    return x * (1.0 / (1.0 + jnp.exp(-x)))


def _split_dot(a, b, pieces, split_rhs):
    x = b if split_rhs else a
    acc = None
    for _ in range(pieces):
        p = x.astype(bf16)
        term = (jnp.dot(a, p, preferred_element_type=f32) if split_rhs
                else jnp.dot(p, b, preferred_element_type=f32))
        acc = term if acc is None else acc + term
        x = x - p.astype(f32)
    return acc


def _emit_stream(h, h_ref, hb_ref, rs_ref):
    h_ref[...] = h
    hb_ref[...] = h.astype(bf16)
    rs_ref[...] = lax.rsqrt(jnp.mean(h * h, axis=-1, keepdims=True) + RMS_EPS)


def _stream_out(rows, d, tile):
    specs = [pl.BlockSpec((tile, d), lambda i: (i, 0)),
             pl.BlockSpec((tile, d), lambda i: (i, 0)),
             pl.BlockSpec((tile, 1), lambda i: (i, 0))]
    shapes = [jax.ShapeDtypeStruct((rows, d), f32),
              jax.ShapeDtypeStruct((rows, d), bf16),
              jax.ShapeDtypeStruct((rows, 1), f32)]
    return specs, shapes


def _embed_kernel(x_ref, meta_ref, h_ref, hb_ref, rs_ref):
    i = pl.program_id(0)

    @pl.when(i < ROWS_X // EMBED_TILE)
    def _():
        _emit_stream(x_ref[...], h_ref, hb_ref, rs_ref)

    @pl.when(i == ROWS_X // EMBED_TILE)
    def _():
        pad = jnp.zeros((EMBED_TILE - N_META, D_MODEL), f32)
        _emit_stream(jnp.concatenate([meta_ref[...], pad], axis=0), h_ref, hb_ref, rs_ref)


def _embed(x2d, meta_tokens):
    n_x = ROWS_X // EMBED_TILE
    specs, shapes = _stream_out(ROWS, D_MODEL, EMBED_TILE)
    return pl.pallas_call(
        _embed_kernel,
        grid=(n_x + 1,),
        in_specs=[
            pl.BlockSpec((EMBED_TILE, D_MODEL), lambda i: (jnp.minimum(i, n_x - 1), 0)),
            pl.BlockSpec((N_META, D_MODEL), lambda i: (0, 0)),
        ],
        out_specs=specs,
        out_shape=shapes,
        compiler_params=pltpu.CompilerParams(dimension_semantics=("arbitrary",), vmem_limit_bytes=VMEM_LIMIT),
        name="embed",
    )(x2d, meta_tokens)


def _proj_kernel(xb_ref, rs_ref, w_ref, o_ref):
    acc = jnp.dot(xb_ref[...], w_ref[...], preferred_element_type=f32)
    o_ref[...] = (acc * rs_ref[...]).astype(o_ref.dtype)


def _proj_small_kernel(xb_ref, rs_ref, w_ref, ws_ref, o_ref, os_ref):
    @pl.when(pl.program_id(1) == 0)
    def _():
        os_ref[...] = jnp.dot(xb_ref[...], ws_ref[...], preferred_element_type=f32) * rs_ref[...]

    _proj_kernel(xb_ref, rs_ref, w_ref, o_ref)


def _proj(hb, rs, w, layer, w_small=None):
    rows, d = hb.shape
    n = w.shape[2]
    tm, tn = PROJ_TILE, PROJ_COLS
    assert rows % tm == 0 and n % tn == 0
    in_specs = [
        pl.BlockSpec((tm, d), lambda i, j: (i, 0)),
        pl.BlockSpec((tm, 1), lambda i, j: (i, 0)),
        pl.BlockSpec((None, d, tn), lambda i, j: (layer, 0, j)),
    ]
    out_specs = [pl.BlockSpec((tm, tn), lambda i, j: (i, j))]
    out_shape = [jax.ShapeDtypeStruct((rows, n), bf16)]
    args = [hb, rs, w]
    if w_small is not None:
        in_specs.append(pl.BlockSpec((None, d, LANES), lambda i, j: (layer, 0, 0)))
        out_specs.append(pl.BlockSpec((tm, LANES), lambda i, j: (i, 0)))
        out_shape.append(jax.ShapeDtypeStruct((rows, LANES), f32))
        args.append(w_small)
    out = pl.pallas_call(
        _proj_kernel if w_small is None else _proj_small_kernel,
        grid=(rows // tm, n // tn),
        in_specs=in_specs,
        out_specs=out_specs,
        out_shape=out_shape,
        compiler_params=pltpu.CompilerParams(
            dimension_semantics=("arbitrary", "arbitrary"), vmem_limit_bytes=VMEM_LIMIT),
        name="proj",
    )(*args)
    return out[0] if w_small is None else out


def _out_proj_kernel(y_ref, w_ref, h_ref, o_ref, ob_ref, rs_ref):
    _emit_stream(h_ref[...] + jnp.dot(y_ref[...], w_ref[...], preferred_element_type=f32), o_ref, ob_ref, rs_ref)


def _out_proj_final_kernel(y_ref, w_ref, h_ref, g_ref, o_ref):
    x = h_ref[...] + jnp.dot(y_ref[...], w_ref[...], preferred_element_type=f32)
    var = jnp.mean(x * x, axis=-1, keepdims=True)
    o_ref[...] = x * lax.rsqrt(var + RMS_EPS) * g_ref[...]


def _out_proj(y, w, layer, h, final_g=None):
    _, k, d = w.shape
    if final_g is None:
        rows, tile, body, extra, extra_specs = h.shape[0], ROW_TILE, _out_proj_kernel, (), []
        out_specs, out_shape = _stream_out(rows, d, tile)
    else:
        rows, tile, body = ROWS_X, OUT_TILE, _out_proj_final_kernel
        extra, extra_specs = (final_g.reshape(1, d),), [pl.BlockSpec((1, d), lambda i: (0, 0))]
        out_specs, out_shape = pl.BlockSpec((tile, d), lambda i: (i, 0)), jax.ShapeDtypeStruct((rows, d), f32)
    return pl.pallas_call(
        body,
        grid=(rows // tile,),
        in_specs=[
            pl.BlockSpec((tile, k), lambda i: (i, 0)),
            pl.BlockSpec((None, k, d), lambda i: (layer, 0, 0)),
            pl.BlockSpec((tile, d), lambda i: (i, 0)),
        ] + extra_specs,
        out_specs=out_specs,
        out_shape=out_shape,
        compiler_params=pltpu.CompilerParams(
            dimension_semantics=("arbitrary",), vmem_limit_bytes=VMEM_LIMIT),
        name="out_proj" if final_g is None else "out_proj_final",
    )(y, w, h, *extra)


def _gla_out_kernel(o_ref, r_ref, hn_ref, w_ref, h_ref, out_ref, outb_ref, rs_ref):
    acc = h_ref[...]
    hn = hn_ref[...]
    for hd in range(GLA_HEADS):
        cols = slice(hd * GLA_HEAD_V, (hd + 1) * GLA_HEAD_V)
        o = o_ref[:, cols].astype(f32)
        var = jnp.mean(o * o, axis=-1, keepdims=True)
        y = o * lax.rsqrt(var + RMS_EPS) * hn * _silu(r_ref[:, cols].astype(f32))
        acc = acc + jnp.dot(y.astype(bf16), w_ref[cols, :], preferred_element_type=f32)
    _emit_stream(acc, out_ref, outb_ref, rs_ref)


def _gla_out_proj(o, proj, head_norm_g, w, layer, h):
    _, k, d = w.shape
    rows = h.shape[0]
    out_specs, out_shape = _stream_out(rows, d, ROW_TILE)
    return pl.pallas_call(
        _gla_out_kernel,
        grid=(rows // ROW_TILE,),
        in_specs=[
            pl.BlockSpec((ROW_TILE, k), lambda i: (i, 0)),
            pl.BlockSpec((ROW_TILE, k), lambda i: (i, proj.shape[1] // k - 1)),
            pl.BlockSpec((1, GLA_HEAD_V), lambda i: (0, 0)),
            pl.BlockSpec((None, k, d), lambda i: (layer, 0, 0)),
            pl.BlockSpec((ROW_TILE, d), lambda i: (i, 0)),
        ],
        out_specs=out_specs,
        out_shape=out_shape,
        compiler_params=pltpu.CompilerParams(
            dimension_semantics=("arbitrary",), vmem_limit_bytes=VMEM_LIMIT),
        name="gla_out_proj",
    )(o, proj, head_norm_g, w, h)


def _gla_masks(c):
    i = lax.broadcasted_iota(jnp.int32, (c, c), 0)
    j = lax.broadcasted_iota(jnp.int32, (c, c), 1)
    base = min(c, GLA_BASE)
    uses = [((i & -base) == (j & -base)) & (j <= i)]
    size = 2 * base
    while size <= c:
        half = size // 2
        uses.append(((i & -size) == (j & -size)) & ((i & half) != 0) & ((j & half) == 0))
        size *= 2
    return (j <= i).astype(bf16), uses


def _gla_group(nc, c, masks, q, k, v, log2_a, state_t, emit):
    n, dk = q.shape
    nt = (((1,), (1,)), ((), ()))
    tn = (((0,), (0,)), ((), ()))
    tril, uses = masks
    row = lax.broadcasted_iota(jnp.int32, (n, 1), 0)

    b = jnp.concatenate([_split_dot(tril, log2_a[ci * c:(ci + 1) * c], 2, split_rhs=True)
                         for ci in range(nc)], axis=0)

    def minus_anchor(size):
        b3 = b.reshape(n // size, size, dk)
        return (b3 - b3[:, size // 2 - 1:size // 2, :]).reshape(n, dk)

    base = min(c, GLA_BASE)
    e0 = minus_anchor(base)
    operands = [((q * jnp.exp2(e0)).astype(bf16), (k * jnp.exp2(-e0)).astype(bf16))]
    size = 2 * base
    while size <= c:
        qk = (jnp.where((row & (size // 2)) != 0, q, k) * jnp.exp2(-jnp.abs(minus_anchor(size)))).astype(bf16)
        operands.append((qk, qk))
        size *= 2

    b3 = b.reshape(nc, c, dk)
    b_last = b3[:, c - 1:c, :]
    q_dec = (q * jnp.exp2(b)).astype(bf16)
    k_rem = (k.reshape(nc, c, dk) * jnp.exp2(b_last - b3)).reshape(n, dk).astype(bf16)
    decay = jnp.exp2(b_last)

    o_intra, update = [], []
    for ci in range(nc):
        rows = slice(ci * c, (ci + 1) * c)
        attn = None
        for (lhs, rhs), use in zip(operands, uses):
            p = lax.dot_general(lhs[rows], rhs[rows], nt, preferred_element_type=f32)
            attn = jnp.where(use, p, 0.0 if attn is None else attn)
        o_intra.append(jnp.dot(attn.astype(bf16), v[rows], preferred_element_type=f32))
        update.append(lax.dot_general(v[rows], k_rem[rows], tn, preferred_element_type=f32))

    for ci in range(nc):
        rows = slice(ci * c, (ci + 1) * c)
        emit(ci, o_intra[ci] + lax.dot_general(q_dec[rows], state_t.astype(bf16), nt,
                                               preferred_element_type=f32))
        state_t = state_t * decay[ci] + update[ci]
    return state_t


def _gla_kernel(q_ref, k_ref, v_ref, g_ref, wup_ref, bg_ref, o_ref, st_ref, st0_ref):
    t = pl.program_id(1)
    wup = wup_ref[...]
    bg = bg_ref[...]

    def run_group(row0, nc, c, masks, state_t):
        rows = pl.ds(row0, nc * c)
        logits = jnp.dot(g_ref[rows, :].astype(bf16), wup, preferred_element_type=f32) + bg
        log2_a = _log_sigmoid(logits) * (LOG2E / GLA_GATE_NORMALIZER)

        def emit(ci, o):
            o_ref[pl.ds(row0 + ci * c, c), :] = o.astype(o_ref.dtype)

        return _gla_group(nc, c, masks, q_ref[rows, :].astype(f32), k_ref[rows, :].astype(f32),
                          v_ref[rows, :], log2_a, state_t, emit)

    @pl.when(t == 0)
    def _():
        zero_state = jnp.zeros((GLA_HEAD_V, GLA_HEAD_K), f32)
        st0_ref[...] = run_group(0, 1, N_META, _gla_masks(N_META), zero_state)
        o_ref[N_META:LANES, :] = jnp.zeros((LANES - N_META, GLA_HEAD_V), o_ref.dtype)

    @pl.when(t > 0)
    def _():
        st_ref[...] = st0_ref[...]
        group_rows = GLA_GROUP * GLA_CHUNK
        masks = _gla_masks(GLA_CHUNK)

        def body(gi, carry):
            row0 = pl.multiple_of(gi * group_rows, group_rows)
            st_ref[...] = run_group(row0, GLA_GROUP, GLA_CHUNK, masks, st_ref[...])
            return carry

        lax.fori_loop(0, SEQ // group_rows, body, 0)


def _gla_mix(proj, g_low, w_up, b_gate, layer):
    kq = GLA_DK_TOTAL // GLA_HEAD_K
    kv = 2 * GLA_DK_TOTAL // GLA_HEAD_V
    return pl.pallas_call(
        _gla_kernel,
        grid=(GLA_HEADS, BATCH + 1),
        in_specs=[
            pl.BlockSpec((SEQ, GLA_HEAD_K), lambda h, t: (_seq_block(t), h)),
            pl.BlockSpec((SEQ, GLA_HEAD_K), lambda h, t: (_seq_block(t), kq + h)),
            pl.BlockSpec((SEQ, GLA_HEAD_V), lambda h, t: (_seq_block(t), kv + h)),
            pl.BlockSpec((SEQ, LANES), lambda h, t: (_seq_block(t), 0)),
            pl.BlockSpec((None, LANES, GLA_HEAD_K), lambda h, t: (layer, 0, h)),
            pl.BlockSpec((None, 1, GLA_HEAD_K), lambda h, t: (layer, 0, h)),
        ],
        out_specs=pl.BlockSpec((SEQ, GLA_HEAD_V), lambda h, t: (_seq_block(t), h)),
        out_shape=jax.ShapeDtypeStruct((ROWS, D_INNER), bf16),
        scratch_shapes=[pltpu.VMEM((GLA_HEAD_V, GLA_HEAD_K), f32),
                        pltpu.VMEM((GLA_HEAD_V, GLA_HEAD_K), f32)],
        compiler_params=pltpu.CompilerParams(
            dimension_semantics=("arbitrary", "arbitrary"), vmem_limit_bytes=VMEM_LIMIT),
        name="gla_mix",
    )(proj, proj, proj, g_low, w_up, b_gate)


def _bias_kernel(f_ref, b_ref, o_ref, meta_ref):
    t = pl.program_id(0)
    i = lax.broadcasted_iota(jnp.int32, (LANES, LANES), 0)
    j = lax.broadcasted_iota(jnp.int32, (LANES, LANES), 1)
    upper = (i <= j).astype(bf16)

    def block_cumsum(c, carry):
        log_f = _log_sigmoid(f_ref[c * LANES:(c + 1) * LANES, :] + b_ref[...])
        return _split_dot(log_f.T, upper, 3, split_rhs=False) + carry

    @pl.when(t == 0)
    def _():
        meta_ref[...] = block_cumsum(0, jnp.zeros((LANES, 1), f32))

    @pl.when(t > 0)
    def _():
        meta = meta_ref[...]
        o_ref[:, 0, 0, 0:N_META] = meta[0:FOX_HEADS, 0:N_META] * -LOG2E
        carry = meta[:, N_META - 1:N_META]
        for c in range(SEQ // LANES):
            cs = block_cumsum(c, carry)
            o_ref[:, 0, 0, N_META + c * LANES:N_META + (c + 1) * LANES] = cs[0:FOX_HEADS, :] * -LOG2E
            carry = cs[:, LANES - 1:LANES]
        o_ref[:, 0, 0, N_META + SEQ:KEYS] = jnp.zeros((FOX_HEADS, KEYS - N_META - SEQ), f32)


def _forget_bias(f_logit, b_forget):
    return pl.pallas_call(
        _bias_kernel,
        grid=(BATCH + 1,),
        in_specs=[
            pl.BlockSpec((SEQ, LANES), lambda t: (_seq_block(t), 0)),
            pl.BlockSpec((1, LANES), lambda t: (0, 0)),
        ],
        out_specs=pl.BlockSpec((FOX_HEADS, 1, 1, KEYS), lambda t: (0, jnp.maximum(t - 1, 0), 0, 0)),
        out_shape=jax.ShapeDtypeStruct((FOX_HEADS, BATCH, 1, KEYS), f32),
        scratch_shapes=[pltpu.VMEM((LANES, LANES), f32)],
        compiler_params=pltpu.CompilerParams(dimension_semantics=("arbitrary",)),
        name="forget_bias",
    )(f_logit, b_forget)


def _scores(q, k, bias, mask):
    s = lax.dot_general(q, k, (((1,), (1,)), ((), ())), preferred_element_type=f32) + bias
    return s if mask is None else jnp.where(mask, s, NEG_INF)


def _softmax_pv(scores, values):
    m = functools.reduce(jnp.maximum, [jnp.max(s, axis=-1, keepdims=True) for s in scores])
    acc = None
    for s, v in zip(scores, values):
        pv = jnp.dot(jnp.exp2(s - m).astype(bf16), v, preferred_element_type=f32)
        acc = pv if acc is None else acc + pv
    return acc[:, 0:FOX_HEAD_DIM] * (1.0 / acc[:, FOX_HEAD_DIM:])


def _fox_kernel(q_ref, r_ref, k_ref, v_ref, b_ref, y_ref, kc_ref, vc_ref):
    t = pl.program_id(1)
    dh = FOX_HEAD_DIM

    def lower_tri(nq, nk, shift):
        i = lax.broadcasted_iota(jnp.int32, (nq, nk), 0)
        j = lax.broadcasted_iota(jnp.int32, (nq, nk), 1)
        return j <= i + shift

    def finish(rows, o):
        y_ref[rows, :] = (o * _silu(r_ref[rows, :].astype(f32))).astype(y_ref.dtype)

    @pl.when(t == 0)
    def _():
        vc_ref[:, dh:] = jnp.ones((KEYS, dh), bf16)
        kc_ref[0:N_META, :] = k_ref[0:N_META, :]
        vc_ref[0:N_META, 0:dh] = v_ref[0:N_META, :]
        s = _scores(q_ref[0:N_META, :], kc_ref[0:N_META, :], b_ref[0, 0, :, 0:N_META],
                    lower_tri(N_META, N_META, 0))
        finish(pl.ds(0, N_META), _softmax_pv([s], [vc_ref[0:N_META, :]]))
        y_ref[N_META:LANES, :] = jnp.zeros((LANES - N_META, dh), y_ref.dtype)

    @pl.when(t > 0)
    def _():
        kc_ref[N_META:N_META + SEQ, :] = k_ref[...]
        vc_ref[N_META:N_META + SEQ, 0:dh] = v_ref[...]
        n_keys = N_META + SEQ

        def tile_scores(ti):
            p0 = ti * FOX_TQ
            k1 = min(p0 + FOX_TQ, n_keys)
            q0 = max(p0 - N_META, 0)
            q1 = k1 - N_META
            q = q_ref[q0:q1, :]
            k0 = p0 // MXU_COLS * MXU_COLS
            scores, values = [], []
            if k0 > 0:
                scores.append(_scores(q, kc_ref[0:k0, :], b_ref[0, 0, :, 0:k0], None))
                values.append(vc_ref[0:k0, :])
            shift = q0 + N_META - k0
            scores.append(_scores(q, kc_ref[k0:k1, :], b_ref[0, 0, :, k0:k1],
                                  lower_tri(q1 - q0, k1 - k0, shift)))
            values.append(vc_ref[k0:k1, :])
            return pl.ds(q0, q1 - q0), scores, values

        order = list(range(-(-n_keys // FOX_TQ)))
        pending = tile_scores(order[0])
        for pos in range(len(order)):
            rows, scores, values = pending
            if pos + 1 < len(order):
                pending = tile_scores(order[pos + 1])
            finish(rows, _softmax_pv(scores, values))


def _fox_mix(qr, kv, bias):
    nh = FOX_HEADS
    return pl.pallas_call(
        _fox_kernel,
        grid=(FOX_HEADS, BATCH + 1),
        in_specs=[
            pl.BlockSpec((SEQ, FOX_HEAD_DIM), lambda h, t: (_seq_block(t), h)),
            pl.BlockSpec((SEQ, FOX_HEAD_DIM), lambda h, t: (_seq_block(t), nh + h)),
            pl.BlockSpec((SEQ, FOX_HEAD_DIM), lambda h, t: (_seq_block(t), h)),
            pl.BlockSpec((SEQ, FOX_HEAD_DIM), lambda h, t: (_seq_block(t), nh + h)),
            pl.BlockSpec((1, 1, 1, KEYS), lambda h, t: (h, jnp.maximum(t - 1, 0), 0, 0)),
        ],
        out_specs=pl.BlockSpec((SEQ, FOX_HEAD_DIM), lambda h, t: (_seq_block(t), h)),
        out_shape=jax.ShapeDtypeStruct((ROWS, D_INNER), bf16),
        scratch_shapes=[pltpu.VMEM((KEYS, FOX_HEAD_DIM), bf16),
                        pltpu.VMEM((KEYS, 2 * FOX_HEAD_DIM), bf16)],
        compiler_params=pltpu.CompilerParams(
            dimension_semantics=("arbitrary", "arbitrary"), vmem_limit_bytes=VMEM_LIMIT),
        name="fox_mix",
    )(qr, qr, kv, kv, bias)


def _pad_last(w, axis):
    pad = [(0, 0)] * w.ndim
    pad[axis] = (0, LANES - w.shape[axis])
    return jnp.pad(w, pad)


def kernel(x, meta_tokens, norm_g, gla_w_in, gla_w_gate_up, gla_b_gate, gla_head_norm_g, gla_w_out,
           kv_norm_g, fox_w_kv, fox_b_forget, fox_w_in, fox_w_out, final_norm_g):
    n_gla = gla_w_in.shape[0]
    n_fox = fox_w_in.shape[0]

    o_g = 2 * GLA_DK_TOTAL + D_INNER
    gla_g = norm_g[:n_gla, :, None]
    gla_scale = jnp.concatenate([jnp.full((GLA_DK_TOTAL,), GLA_HEAD_K ** -0.5, f32),
                                 jnp.ones((o_g - GLA_DK_TOTAL + D_INNER,), f32)])
    gla_w_main = (jnp.concatenate([gla_w_in[:, :, :o_g], gla_w_in[:, :, o_g + GLA_GATE_RANK:]], axis=2)
                  * gla_scale * gla_g).astype(bf16)
    gla_w_low = _pad_last(gla_w_in[:, :, o_g:o_g + GLA_GATE_RANK] * gla_g, 2).astype(bf16)
    gla_w_up = _pad_last(gla_w_gate_up, 1).astype(bf16)
    gla_w_o = gla_w_out.astype(bf16)
    kv_g = kv_norm_g[None, :, None]
    kv_w = (fox_w_kv[None, :, :2 * D_INNER] * kv_g).astype(bf16)
    kv_w_f = _pad_last(fox_w_kv[None, :, 2 * D_INNER:] * kv_g, 2).astype(bf16)
    fox_scale = jnp.concatenate([jnp.full((D_INNER,), FOX_HEAD_DIM ** -0.5 * LOG2E, f32),
                                 jnp.ones((D_INNER,), f32)])
    fox_w_qr = (fox_w_in * fox_scale * norm_g[n_gla:, :, None]).astype(bf16)
    fox_w_o = fox_w_out.astype(bf16)

    h, hb, rs = _embed(x.reshape(ROWS_X, D_MODEL), meta_tokens)
    for layer in range(n_gla):
        proj, g_low = _proj(hb, rs, gla_w_main, layer, w_small=gla_w_low)
        o = _gla_mix(proj, g_low, gla_w_up, gla_b_gate.reshape(n_gla, 1, -1), layer)
        h, hb, rs = _gla_out_proj(o, proj, gla_head_norm_g[layer].reshape(1, -1), gla_w_o, layer, h)

    kv, f_logit = _proj(hb, rs, kv_w, 0, w_small=kv_w_f)
    bias = _forget_bias(f_logit, _pad_last(fox_b_forget.reshape(1, -1), 1))

    for j in range(n_fox):
        qr = _proj(hb, rs, fox_w_qr, j)
        y = _fox_mix(qr, kv, bias)
        if j + 1 < n_fox:
            h, hb, rs = _out_proj(y, fox_w_o, j, h)
        else:
            out = _out_proj(y, fox_w_o, j, h, final_g=final_norm_g)
    return out.reshape(BATCH, SEQ, D_MODEL)
```

```python
import functools

import jax
import jax.numpy as jnp
from jax import lax
from jax.experimental import pallas as pl
from jax.experimental.pallas import tpu as pltpu

f32 = jnp.float32
bf16 = jnp.bfloat16

D_MODEL = 1024
BATCH = 8
SEQ = 2048
N_META = 16
D_INNER = 2048
GLA_HEADS = 4
GLA_HEAD_K = 256
GLA_HEAD_V = 512
GLA_DK_TOTAL = GLA_HEADS * GLA_HEAD_K
GLA_GATE_RANK = 16
GLA_GATE_NORMALIZER = 16.0
FOX_HEADS = 16
FOX_HEAD_DIM = 128
RMS_EPS = 1e-6
NEG_INF = -1e30
LOG2E = 1.4426950408889634

LANES = 128
ROWS_X = BATCH * SEQ
ROWS = ROWS_X + LANES
META_BLOCK = BATCH
VMEM_LIMIT = 56 * 1024 * 1024

GLA_CHUNK = 128
GLA_BASE = 32
GLA_GROUP = 16
MXU_COLS = 256
FOX_TQ = 256
KEYS = 2304
FOX_STREAMS = 2
ROW_TILE = 688
PROJ_TILE = 1376
PROJ_COLS = 2048
OUT_TILE = 512
EMBED_TILE = 512


def _seq_block(t):
    return jnp.where(t == 0, META_BLOCK, t - 1)


def _log_sigmoid(x):
    return jnp.minimum(x, 0.0) - jnp.log1p(jnp.exp(-jnp.abs(x)))


def _silu(x):
    return x * (1.0 / (1.0 + jnp.exp(-x)))


def _split_dot(a, b, pieces, split_rhs):
    x = b if split_rhs else a
    acc = None
    for _ in range(pieces):
        p = x.astype(bf16)
        term = (jnp.dot(a, p, preferred_element_type=f32) if split_rhs
                else jnp.dot(p, b, preferred_element_type=f32))
        acc = term if acc is None else acc + term
        x = x - p.astype(f32)
    return acc


def _emit_stream(h, h_ref, hb_ref, rs_ref):
    h_ref[...] = h
    hb_ref[...] = h.astype(bf16)
    rs_ref[...] = lax.rsqrt(jnp.mean(h * h, axis=-1, keepdims=True) + RMS_EPS)


def _stream_out(rows, d, tile):
    specs = [pl.BlockSpec((tile, d), lambda i: (i, 0)),
             pl.BlockSpec((tile, d), lambda i: (i, 0)),
             pl.BlockSpec((tile, 1), lambda i: (i, 0))]
    shapes = [jax.ShapeDtypeStruct((rows, d), f32),
              jax.ShapeDtypeStruct((rows, d), bf16),
              jax.ShapeDtypeStruct((rows, 1), f32)]
    return specs, shapes


def _embed_kernel(x_ref, meta_ref, h_ref, hb_ref, rs_ref):
    i = pl.program_id(0)

    @pl.when(i < ROWS_X // EMBED_TILE)
    def _():
        _emit_stream(x_ref[...], h_ref, hb_ref, rs_ref)

    @pl.when(i == ROWS_X // EMBED_TILE)
    def _():
        pad = jnp.zeros((EMBED_TILE - N_META, D_MODEL), f32)
        _emit_stream(jnp.concatenate([meta_ref[...], pad], axis=0), h_ref, hb_ref, rs_ref)


def _embed(x2d, meta_tokens):
    n_x = ROWS_X // EMBED_TILE
    specs, shapes = _stream_out(ROWS, D_MODEL, EMBED_TILE)
    return pl.pallas_call(
        _embed_kernel,
        grid=(n_x + 1,),
        in_specs=[
            pl.BlockSpec((EMBED_TILE, D_MODEL), lambda i: (jnp.minimum(i, n_x - 1), 0)),
            pl.BlockSpec((N_META, D_MODEL), lambda i: (0, 0)),
        ],
        out_specs=specs,
        out_shape=shapes,
        compiler_params=pltpu.CompilerParams(dimension_semantics=("arbitrary",), vmem_limit_bytes=VMEM_LIMIT),
        name="embed",
    )(x2d, meta_tokens)


def _proj_kernel(xb_ref, rs_ref, w_ref, o_ref):
    acc = jnp.dot(xb_ref[...], w_ref[...], preferred_element_type=f32)
    o_ref[...] = (acc * rs_ref[...]).astype(o_ref.dtype)


def _proj_small(xb_ref, rs_ref, ws_ref, os_ref):
    @pl.when(pl.program_id(1) == 0)
    def _():
        os_ref[...] = jnp.dot(xb_ref[...], ws_ref[...], preferred_element_type=f32) * rs_ref[...]


def _proj_small_kernel(xb_ref, rs_ref, w_ref, ws_ref, o_ref, os_ref):
    _proj_small(xb_ref, rs_ref, ws_ref, os_ref)
    _proj_kernel(xb_ref, rs_ref, w_ref, o_ref)


def _proj_tail_small_kernel(xb_ref, rs_ref, w_ref, wt_ref, ws_ref, o_ref, os_ref):
    _proj_small(xb_ref, rs_ref, ws_ref, os_ref)
    last = pl.num_programs(1) - 1

    @pl.when(pl.program_id(1) < last)
    def _():
        _proj_kernel(xb_ref, rs_ref, w_ref, o_ref)

    @pl.when(pl.program_id(1) == last)
    def _():
        _proj_kernel(xb_ref, rs_ref, wt_ref, o_ref)


def _proj(hb, rs, w, layer, n, w_tail=None, w_small=None):
    rows, d = hb.shape
    tm, tn = PROJ_TILE, PROJ_COLS
    assert rows % tm == 0 and n % tn == 0 and (w_tail is None or w_small is not None)
    n_w = n // tn - (w_tail is not None)
    in_specs = [
        pl.BlockSpec((tm, d), lambda i, j: (i, 0)),
        pl.BlockSpec((tm, 1), lambda i, j: (i, 0)),
        pl.BlockSpec((None, d, tn), lambda i, j: (layer, 0, jnp.minimum(j, n_w - 1))),
    ]
    out_specs = [pl.BlockSpec((tm, tn), lambda i, j: (i, j))]
    out_shape = [jax.ShapeDtypeStruct((rows, n), bf16)]
    args = [hb, rs, w]
    body = _proj_kernel
    if w_tail is not None:
        in_specs.append(pl.BlockSpec((None, d, tn), lambda i, j: (layer, 0, 0)))
        args.append(w_tail)
        body = _proj_tail_small_kernel
    elif w_small is not None:
        body = _proj_small_kernel
    if w_small is not None:
        in_specs.append(pl.BlockSpec((None, d, LANES), lambda i, j: (layer, 0, 0)))
        out_specs.append(pl.BlockSpec((tm, LANES), lambda i, j: (i, 0)))
        out_shape.append(jax.ShapeDtypeStruct((rows, LANES), f32))
        args.append(w_small)
    out = pl.pallas_call(
        body,
        grid=(rows // tm, n // tn),
        in_specs=in_specs,
        out_specs=out_specs,
        out_shape=out_shape,
        compiler_params=pltpu.CompilerParams(
            dimension_semantics=("arbitrary", "arbitrary"), vmem_limit_bytes=VMEM_LIMIT),
        name="proj",
    )(*args)
    return out[0] if w_small is None else out


def _out_proj_kernel(y_ref, w_ref, h_ref, o_ref, ob_ref, rs_ref):
    _emit_stream(h_ref[...] + jnp.dot(y_ref[...], w_ref[...], preferred_element_type=f32), o_ref, ob_ref, rs_ref)


def _out_proj_final_kernel(y_ref, w_ref, h_ref, g_ref, o_ref):
    x = h_ref[...] + jnp.dot(y_ref[...], w_ref[...], preferred_element_type=f32)
    var = jnp.mean(x * x, axis=-1, keepdims=True)
    o_ref[...] = x * lax.rsqrt(var + RMS_EPS) * g_ref[...]


def _out_proj(y, w, layer, h, final_g=None):
    _, k, d = w.shape
    if final_g is None:
        rows, tile, body, extra, extra_specs = h.shape[0], ROW_TILE, _out_proj_kernel, (), []
        out_specs, out_shape = _stream_out(rows, d, tile)
    else:
        rows, tile, body = ROWS_X, OUT_TILE, _out_proj_final_kernel
        extra, extra_specs = (final_g.reshape(1, d),), [pl.BlockSpec((1, d), lambda i: (0, 0))]
        out_specs, out_shape = pl.BlockSpec((tile, d), lambda i: (i, 0)), jax.ShapeDtypeStruct((rows, d), f32)
    return pl.pallas_call(
        body,
        grid=(rows // tile,),
        in_specs=[
            pl.BlockSpec((tile, k), lambda i: (i, 0)),
            pl.BlockSpec((None, k, d), lambda i: (layer, 0, 0)),
            pl.BlockSpec((tile, d), lambda i: (i, 0)),
        ] + extra_specs,
        out_specs=out_specs,
        out_shape=out_shape,
        compiler_params=pltpu.CompilerParams(
            dimension_semantics=("arbitrary",), vmem_limit_bytes=VMEM_LIMIT),
        name="out_proj" if final_g is None else "out_proj_final",
    )(y, w, h, *extra)


def _gla_out_kernel(o_ref, r_ref, w_ref, h_ref, out_ref, outb_ref, rs_ref):
    acc = h_ref[...]
    for hd in range(GLA_HEADS):
        cols = slice(hd * GLA_HEAD_V, (hd + 1) * GLA_HEAD_V)
        o = o_ref[:, cols].astype(f32)
        var = jnp.mean(o * o, axis=-1, keepdims=True)
        y = o * lax.rsqrt(var + RMS_EPS) * _silu(r_ref[:, cols].astype(f32))
        acc = acc + jnp.dot(y.astype(bf16), w_ref[cols, :], preferred_element_type=f32)
    _emit_stream(acc, out_ref, outb_ref, rs_ref)


def _gla_out_proj(o, proj, w, layer, h):
    _, k, d = w.shape
    rows = h.shape[0]
    out_specs, out_shape = _stream_out(rows, d, ROW_TILE)
    return pl.pallas_call(
        _gla_out_kernel,
        grid=(rows // ROW_TILE,),
        in_specs=[
            pl.BlockSpec((ROW_TILE, k), lambda i: (i, 0)),
            pl.BlockSpec((ROW_TILE, k), lambda i: (i, proj.shape[1] // k - 1)),
            pl.BlockSpec((None, k, d), lambda i: (layer, 0, 0)),
            pl.BlockSpec((ROW_TILE, d), lambda i: (i, 0)),
        ],
        out_specs=out_specs,
        out_shape=out_shape,
        compiler_params=pltpu.CompilerParams(
            dimension_semantics=("arbitrary",), vmem_limit_bytes=VMEM_LIMIT),
        name="gla_out_proj",
    )(o, proj, w, h)


def _gla_masks(c):
    i = lax.broadcasted_iota(jnp.int32, (c, c), 0)
    j = lax.broadcasted_iota(jnp.int32, (c, c), 1)
    base = min(c, GLA_BASE)
    uses = [((i & -base) == (j & -base)) & (j <= i)]
    size = 2 * base
    while size <= c:
        half = size // 2
        uses.append(((i & -size) == (j & -size)) & ((i & half) != 0) & ((j & half) == 0))
        size *= 2
    return (j <= i).astype(bf16), uses


def _gla_group(nc, c, masks, q, k, v, log2_a, state_t, emit):
    n, dk = q.shape
    nt = (((1,), (1,)), ((), ()))
    tn = (((0,), (0,)), ((), ()))
    tril, uses = masks
    row = lax.broadcasted_iota(jnp.int32, (n, 1), 0)

    b = jnp.concatenate([_split_dot(tril, log2_a[ci * c:(ci + 1) * c], 2, split_rhs=True)
                         for ci in range(nc)], axis=0)

    def minus_anchor(size):
        b3 = b.reshape(n // size, size, dk)
        return (b3 - b3[:, size // 2 - 1:size // 2, :]).reshape(n, dk)

    base = min(c, GLA_BASE)
    e0 = minus_anchor(base)
    operands = [((q * jnp.exp2(e0)).astype(bf16), (k * jnp.exp2(-e0)).astype(bf16))]
    size = 2 * base
    while size <= c:
        qk = (jnp.where((row & (size // 2)) != 0, q, k) * jnp.exp2(-jnp.abs(minus_anchor(size)))).astype(bf16)
        operands.append((qk, qk))
        size *= 2

    b3 = b.reshape(nc, c, dk)
    b_last = b3[:, c - 1:c, :]
    q_dec = (q * jnp.exp2(b)).astype(bf16)
    k_rem = (k.reshape(nc, c, dk) * jnp.exp2(b_last - b3)).reshape(n, dk).astype(bf16)
    decay = jnp.exp2(b_last)

    o_intra, update = [], []
    for ci in range(nc):
        rows = slice(ci * c, (ci + 1) * c)
        attn = None
        for (lhs, rhs), use in zip(operands, uses):
            p = lax.dot_general(lhs[rows], rhs[rows], nt, preferred_element_type=f32)
            attn = jnp.where(use, p, 0.0 if attn is None else attn)
        o_intra.append(jnp.dot(attn.astype(bf16), v[rows], preferred_element_type=f32))
        update.append(lax.dot_general(v[rows], k_rem[rows], tn, preferred_element_type=f32))

    for ci in range(nc):
        rows = slice(ci * c, (ci + 1) * c)
        emit(ci, o_intra[ci] + lax.dot_general(q_dec[rows], state_t.astype(bf16), nt,
                                               preferred_element_type=f32))
        state_t = state_t * decay[ci] + update[ci]
    return state_t


def _gla_kernel(q_ref, k_ref, v_ref, g_ref, wup_ref, bg_ref, o_ref, st_ref, st0_ref):
    t = pl.program_id(1)
    wup = wup_ref[...]
    bg = bg_ref[...]

    def run_group(row0, nc, c, masks, state_t):
        rows = pl.ds(row0, nc * c)
        logits = jnp.dot(g_ref[rows, :].astype(bf16), wup, preferred_element_type=f32) + bg
        log2_a = _log_sigmoid(logits) * (LOG2E / GLA_GATE_NORMALIZER)

        def emit(ci, o):
            o_ref[pl.ds(row0 + ci * c, c), :] = o.astype(o_ref.dtype)

        return _gla_group(nc, c, masks, q_ref[rows, :].astype(f32), k_ref[rows, :].astype(f32),
                          v_ref[rows, :], log2_a, state_t, emit)

    @pl.when(t == 0)
    def _():
        zero_state = jnp.zeros((GLA_HEAD_V, GLA_HEAD_K), f32)
        st0_ref[...] = run_group(0, 1, N_META, _gla_masks(N_META), zero_state)
        o_ref[N_META:LANES, :] = jnp.zeros((LANES - N_META, GLA_HEAD_V), o_ref.dtype)

    @pl.when(t > 0)
    def _():
        st_ref[...] = st0_ref[...]
        group_rows = GLA_GROUP * GLA_CHUNK
        masks = _gla_masks(GLA_CHUNK)

        def body(gi, carry):
            row0 = pl.multiple_of(gi * group_rows, group_rows)
            st_ref[...] = run_group(row0, GLA_GROUP, GLA_CHUNK, masks, st_ref[...])
            return carry

        lax.fori_loop(0, SEQ // group_rows, body, 0)


def _gla_mix(proj, g_low, w_up, b_gate, layer):
    kq = GLA_DK_TOTAL // GLA_HEAD_K
    kv = 2 * GLA_DK_TOTAL // GLA_HEAD_V
    return pl.pallas_call(
        _gla_kernel,
        grid=(GLA_HEADS, BATCH + 1),
        in_specs=[
            pl.BlockSpec((SEQ, GLA_HEAD_K), lambda h, t: (_seq_block(t), h)),
            pl.BlockSpec((SEQ, GLA_HEAD_K), lambda h, t: (_seq_block(t), kq + h)),
            pl.BlockSpec((SEQ, GLA_HEAD_V), lambda h, t: (_seq_block(t), kv + h)),
            pl.BlockSpec((SEQ, LANES), lambda h, t: (_seq_block(t), 0)),
            pl.BlockSpec((None, LANES, GLA_HEAD_K), lambda h, t: (layer, 0, h)),
            pl.BlockSpec((None, 1, GLA_HEAD_K), lambda h, t: (layer, 0, h)),
        ],
        out_specs=pl.BlockSpec((SEQ, GLA_HEAD_V), lambda h, t: (_seq_block(t), h)),
        out_shape=jax.ShapeDtypeStruct((ROWS, D_INNER), bf16),
        scratch_shapes=[pltpu.VMEM((GLA_HEAD_V, GLA_HEAD_K), f32),
                        pltpu.VMEM((GLA_HEAD_V, GLA_HEAD_K), f32)],
        compiler_params=pltpu.CompilerParams(
            dimension_semantics=("arbitrary", "arbitrary"), vmem_limit_bytes=VMEM_LIMIT),
        name="gla_mix",
    )(proj, proj, proj, g_low, w_up, b_gate)


def _bias_kernel(f_ref, b_ref, o_ref, meta_ref):
    t = pl.program_id(0)
    i = lax.broadcasted_iota(jnp.int32, (LANES, LANES), 0)
    j = lax.broadcasted_iota(jnp.int32, (LANES, LANES), 1)
    upper = (i <= j).astype(bf16)

    def block_cumsum(c, carry):
        log_f = _log_sigmoid(f_ref[c * LANES:(c + 1) * LANES, :] + b_ref[...])
        return _split_dot(log_f.T, upper, 3, split_rhs=False) + carry

    @pl.when(t == 0)
    def _():
        meta_ref[...] = block_cumsum(0, jnp.zeros((LANES, 1), f32))

    @pl.when(t > 0)
    def _():
        meta = meta_ref[...]
        o_ref[:, 0, 0, 0:N_META] = meta[0:FOX_HEADS, 0:N_META] * -LOG2E
        carry = meta[:, N_META - 1:N_META]
        for c in range(SEQ // LANES):
            cs = block_cumsum(c, carry)
            o_ref[:, 0, 0, N_META + c * LANES:N_META + (c + 1) * LANES] = cs[0:FOX_HEADS, :] * -LOG2E
            carry = cs[:, LANES - 1:LANES]
        o_ref[:, 0, 0, N_META + SEQ:KEYS] = jnp.zeros((FOX_HEADS, KEYS - N_META - SEQ), f32)


def _forget_bias(f_logit, b_forget):
    return pl.pallas_call(
        _bias_kernel,
        grid=(BATCH + 1,),
        in_specs=[
            pl.BlockSpec((SEQ, LANES), lambda t: (_seq_block(t), 0)),
            pl.BlockSpec((1, LANES), lambda t: (0, 0)),
        ],
        out_specs=pl.BlockSpec((FOX_HEADS, 1, 1, KEYS), lambda t: (0, jnp.maximum(t - 1, 0), 0, 0)),
        out_shape=jax.ShapeDtypeStruct((FOX_HEADS, BATCH, 1, KEYS), f32),
        scratch_shapes=[pltpu.VMEM((LANES, LANES), f32)],
        compiler_params=pltpu.CompilerParams(dimension_semantics=("arbitrary",)),
        name="forget_bias",
    )(f_logit, b_forget)


def _scores(q, k, bias, mask):
    s = lax.dot_general(q, k, (((1,), (1,)), ((), ())), preferred_element_type=f32) + bias
    return s if mask is None else jnp.where(mask, s, NEG_INF)


def _softmax_pv(scores, values):
    m = functools.reduce(jnp.maximum, [jnp.max(s, axis=-1, keepdims=True) for s in scores])
    acc = None
    for s, v in zip(scores, values):
        pv = jnp.dot(jnp.exp2(s - m).astype(bf16), v, preferred_element_type=f32)
        acc = pv if acc is None else acc + pv
    return acc[:, 0:FOX_HEAD_DIM] * (1.0 / acc[:, FOX_HEAD_DIM:])


def _fox_kernel(q_ref, r_ref, k_ref, v_ref, b_ref, y_ref, kc_ref, vc_ref):
    t = pl.program_id(1)
    dh = FOX_HEAD_DIM

    def lower_tri(nq, nk, shift):
        i = lax.broadcasted_iota(jnp.int32, (nq, nk), 0)
        j = lax.broadcasted_iota(jnp.int32, (nq, nk), 1)
        return j <= i + shift

    def finish(rows, o):
        y_ref[rows, :] = (o * _silu(r_ref[rows, :].astype(f32))).astype(y_ref.dtype)

    @pl.when(t == 0)
    def _():
        for s in range(FOX_STREAMS):
            vc_ref[s, :, dh:] = jnp.ones((KEYS, dh), bf16)
            kc_ref[s, 0:N_META, :] = k_ref[0:N_META, :]
            vc_ref[s, 0:N_META, 0:dh] = v_ref[0:N_META, :]
        sc = _scores(q_ref[0:N_META, :], kc_ref[0, 0:N_META, :], b_ref[0, 0, :, 0:N_META],
                     lower_tri(N_META, N_META, 0))
        finish(pl.ds(0, N_META), _softmax_pv([sc], [vc_ref[0, 0:N_META, :]]))
        y_ref[N_META:LANES, :] = jnp.zeros((LANES - N_META, dh), y_ref.dtype)

    @pl.when(t > 0)
    def _():
        n_keys = N_META + SEQ
        for s in range(FOX_STREAMS):
            kc_ref[s, N_META:n_keys, :] = k_ref[s * SEQ:(s + 1) * SEQ, :]
            vc_ref[s, N_META:n_keys, 0:dh] = v_ref[s * SEQ:(s + 1) * SEQ, :]

        def tile_scores(s, ti):
            p0 = ti * FOX_TQ
            k1 = min(p0 + FOX_TQ, n_keys)
            q0 = max(p0 - N_META, 0)
            q1 = k1 - N_META
            q = q_ref[s * SEQ + q0:s * SEQ + q1, :]
            k0 = p0 // MXU_COLS * MXU_COLS
            scores, values = [], []
            if k0 > 0:
                scores.append(_scores(q, kc_ref[s, 0:k0, :], b_ref[0, s, :, 0:k0], None))
                values.append(vc_ref[s, 0:k0, :])
            shift = q0 + N_META - k0
            scores.append(_scores(q, kc_ref[s, k0:k1, :], b_ref[0, s, :, k0:k1],
                                  lower_tri(q1 - q0, k1 - k0, shift)))
            values.append(vc_ref[s, k0:k1, :])
            return pl.ds(s * SEQ + q0, q1 - q0), scores, values

        n_tiles = -(-n_keys // FOX_TQ)
        order = [(s, ti if s % 2 == 0 else n_tiles - 1 - ti) for ti in range(n_tiles) for s in range(FOX_STREAMS)]
        pending = tile_scores(*order[0])
        for pos in range(len(order)):
            rows, scores, values = pending
            if pos + 1 < len(order):
                pending = tile_scores(*order[pos + 1])
            finish(rows, _softmax_pv(scores, values))


def _fox_mix(qr, kv, bias):
    nh = FOX_HEADS
    rows = FOX_STREAMS * SEQ
    n_steps = BATCH // FOX_STREAMS

    def row_block(t):
        return jnp.where(t == 0, n_steps, t - 1)

    return pl.pallas_call(
        _fox_kernel,
        grid=(FOX_HEADS, n_steps + 1),
        in_specs=[
            pl.BlockSpec((rows, FOX_HEAD_DIM), lambda h, t: (row_block(t), h)),
            pl.BlockSpec((rows, FOX_HEAD_DIM), lambda h, t: (row_block(t), nh + h)),
            pl.BlockSpec((rows, FOX_HEAD_DIM), lambda h, t: (row_block(t), h)),
            pl.BlockSpec((rows, FOX_HEAD_DIM), lambda h, t: (row_block(t), nh + h)),
            pl.BlockSpec((1, FOX_STREAMS, 1, KEYS), lambda h, t: (h, jnp.maximum(t - 1, 0), 0, 0)),
        ],
        out_specs=pl.BlockSpec((rows, FOX_HEAD_DIM), lambda h, t: (row_block(t), h)),
        out_shape=jax.ShapeDtypeStruct((ROWS, D_INNER), bf16),
        scratch_shapes=[pltpu.VMEM((FOX_STREAMS, KEYS, FOX_HEAD_DIM), bf16),
                        pltpu.VMEM((FOX_STREAMS, KEYS, 2 * FOX_HEAD_DIM), bf16)],
        compiler_params=pltpu.CompilerParams(
            dimension_semantics=("arbitrary", "arbitrary"), vmem_limit_bytes=VMEM_LIMIT),
        name="fox_mix",
    )(qr, qr, kv, kv, bias)


def _pad_last(w, axis):
    pad = [(0, 0)] * w.ndim
    pad[axis] = (0, LANES - w.shape[axis])
    return jnp.pad(w, pad)


def kernel(x, meta_tokens, norm_g, gla_w_in, gla_w_gate_up, gla_b_gate, gla_head_norm_g, gla_w_out,
           kv_norm_g, fox_w_kv, fox_b_forget, fox_w_in, fox_w_out, final_norm_g):
    n_gla = gla_w_in.shape[0]
    n_fox = fox_w_in.shape[0]

    o_g = 2 * GLA_DK_TOTAL + D_INNER
    o_r = o_g + GLA_GATE_RANK
    gla_scale = jnp.concatenate([jnp.full((GLA_DK_TOTAL,), GLA_HEAD_K ** -0.5, f32),
                                 jnp.ones((gla_w_in.shape[2] - GLA_DK_TOTAL,), f32)])
    gla_w_all = (gla_w_in * gla_scale * norm_g[:n_gla, :, None]).astype(bf16)
    gla_w_r = gla_w_all[:, :, o_r:]
    gla_w_low = _pad_last(gla_w_all[:, :, o_g:o_r], 2)
    gla_w_up = _pad_last(gla_w_gate_up, 1).astype(bf16)
    gla_w_o = (gla_w_out * jnp.tile(gla_head_norm_g, (1, GLA_HEADS))[:, :, None]).astype(bf16)
    kv_w_all = (fox_w_kv[None] * kv_norm_g[None, :, None]).astype(bf16)
    kv_w_f = _pad_last(kv_w_all[:, :, 2 * D_INNER:], 2)
    fox_scale = jnp.concatenate([jnp.full((D_INNER,), FOX_HEAD_DIM ** -0.5 * LOG2E, f32),
                                 jnp.ones((D_INNER,), f32)])
    fox_w_qr = (fox_w_in * fox_scale * norm_g[n_gla:, :, None]).astype(bf16)
    fox_w_o = fox_w_out.astype(bf16)

    h, hb, rs = _embed(x.reshape(ROWS_X, D_MODEL), meta_tokens)
    for layer in range(n_gla):
        proj, g_low = _proj(hb, rs, gla_w_all, layer, o_g + D_INNER, w_tail=gla_w_r, w_small=gla_w_low)
        o = _gla_mix(proj, g_low, gla_w_up, gla_b_gate.reshape(n_gla, 1, -1), layer)
        h, hb, rs = _gla_out_proj(o, proj, gla_w_o, layer, h)

    kv, f_logit = _proj(hb, rs, kv_w_all, 0, 2 * D_INNER, w_small=kv_w_f)
    bias = _forget_bias(f_logit, _pad_last(fox_b_forget.reshape(1, -1), 1))

    for j in range(n_fox):
        qr = _proj(hb, rs, fox_w_qr, j, 2 * D_INNER)
        y = _fox_mix(qr, kv, bias)
        if j + 1 < n_fox:
            h, hb, rs = _out_proj(y, fox_w_o, j, h)
        else:
            out = _out_proj(y, fox_w_o, j, h, final_g=final_norm_g)
    return out.reshape(BATCH, SEQ, D_MODEL)
```

```python
import functools

import jax
import jax.numpy as jnp
from jax import lax
from jax.experimental import pallas as pl
from jax.experimental.pallas import tpu as pltpu

f32 = jnp.float32
bf16 = jnp.bfloat16

D_MODEL = 1024
BATCH = 8
SEQ = 2048
N_META = 16
D_INNER = 2048
GLA_HEADS = 4
GLA_HEAD_K = 256
GLA_HEAD_V = 512
GLA_DK_TOTAL = GLA_HEADS * GLA_HEAD_K
GLA_GATE_RANK = 16
GLA_GATE_NORMALIZER = 16.0
FOX_HEADS = 16
FOX_HEAD_DIM = 128
RMS_EPS = 1e-6
NEG_INF = -1e30
LOG2E = 1.4426950408889634

LANES = 128
ROWS_X = BATCH * SEQ
ROWS = ROWS_X + LANES
META_BLOCK = BATCH
VMEM_LIMIT = 56 * 1024 * 1024

GLA_CHUNK = 128
GLA_BASE = 32
GLA_GROUP = 16
MXU_COLS = 256
FOX_TQ = 256
KEYS = 2304
FOX_STREAMS = 2
ROW_TILE = 688
PROJ_TILE = 1376
PROJ_COLS = 2048
OUT_TILE = 512
EMBED_TILE = 512
PREP_TILE = 256


def _seq_block(t):
    return jnp.where(t == 0, META_BLOCK, t - 1)


def _log_sigmoid(x):
    return jnp.minimum(x, 0.0) - jnp.log1p(jnp.exp(-jnp.abs(x)))


def _silu(x):
    return x * (1.0 / (1.0 + jnp.exp(-x)))


def _split_dot(a, b, pieces, split_rhs):
    x = b if split_rhs else a
    acc = None
    for _ in range(pieces):
        p = x.astype(bf16)
        term = (jnp.dot(a, p, preferred_element_type=f32) if split_rhs
                else jnp.dot(p, b, preferred_element_type=f32))
        acc = term if acc is None else acc + term
        x = x - p.astype(f32)
    return acc


def _emit_stream(h, h_ref, hb_ref, rs_ref):
    h_ref[...] = h
    hb_ref[...] = h.astype(bf16)
    rs_ref[...] = lax.rsqrt(jnp.mean(h * h, axis=-1, keepdims=True) + RMS_EPS)


def _stream_out(rows, d, tile):
    specs = [pl.BlockSpec((tile, d), lambda i: (i, 0)),
             pl.BlockSpec((tile, d), lambda i: (i, 0)),
             pl.BlockSpec((tile, 1), lambda i: (i, 0))]
    shapes = [jax.ShapeDtypeStruct((rows, d), f32),
              jax.ShapeDtypeStruct((rows, d), bf16),
              jax.ShapeDtypeStruct((rows, 1), f32)]
    return specs, shapes


def _embed_kernel(x_ref, meta_ref, h_ref, hb_ref, rs_ref):
    i = pl.program_id(0)

    @pl.when(i < ROWS_X // EMBED_TILE)
    def _():
        _emit_stream(x_ref[...], h_ref, hb_ref, rs_ref)

    @pl.when(i == ROWS_X // EMBED_TILE)
    def _():
        pad = jnp.zeros((EMBED_TILE - N_META, D_MODEL), f32)
        _emit_stream(jnp.concatenate([meta_ref[...], pad], axis=0), h_ref, hb_ref, rs_ref)


def _embed(x2d, meta_tokens):
    n_x = ROWS_X // EMBED_TILE
    specs, shapes = _stream_out(ROWS, D_MODEL, EMBED_TILE)
    return pl.pallas_call(
        _embed_kernel,
        grid=(n_x + 1,),
        in_specs=[
            pl.BlockSpec((EMBED_TILE, D_MODEL), lambda i: (jnp.minimum(i, n_x - 1), 0)),
            pl.BlockSpec((N_META, D_MODEL), lambda i: (0, 0)),
        ],
        out_specs=specs,
        out_shape=shapes,
        compiler_params=pltpu.CompilerParams(dimension_semantics=("arbitrary",), vmem_limit_bytes=VMEM_LIMIT),
        name="embed",
    )(x2d, meta_tokens)


def _proj_kernel(xb_ref, rs_ref, w_ref, o_ref):
    acc = jnp.dot(xb_ref[...], w_ref[...], preferred_element_type=f32)
    o_ref[...] = (acc * rs_ref[...]).astype(o_ref.dtype)


def _proj_small(xb_ref, rs_ref, ws_ref, os_ref):
    @pl.when(pl.program_id(1) == 0)
    def _():
        os_ref[...] = jnp.dot(xb_ref[...], ws_ref[...], preferred_element_type=f32) * rs_ref[...]


def _proj_small_kernel(xb_ref, rs_ref, w_ref, ws_ref, o_ref, os_ref):
    _proj_small(xb_ref, rs_ref, ws_ref, os_ref)
    _proj_kernel(xb_ref, rs_ref, w_ref, o_ref)


def _proj_tail_small_kernel(xb_ref, rs_ref, w_ref, wt_ref, ws_ref, o_ref, os_ref):
    _proj_small(xb_ref, rs_ref, ws_ref, os_ref)
    last = pl.num_programs(1) - 1

    @pl.when(pl.program_id(1) < last)
    def _():
        _proj_kernel(xb_ref, rs_ref, w_ref, o_ref)

    @pl.when(pl.program_id(1) == last)
    def _():
        _proj_kernel(xb_ref, rs_ref, wt_ref, o_ref)


def _proj(hb, rs, w, layer, n, w_tail=None, w_small=None):
    rows, d = hb.shape
    tm, tn = PROJ_TILE, PROJ_COLS
    assert rows % tm == 0 and n % tn == 0 and (w_tail is None or w_small is not None)
    n_w = n // tn - (w_tail is not None)
    in_specs = [
        pl.BlockSpec((tm, d), lambda i, j: (i, 0)),
        pl.BlockSpec((tm, 1), lambda i, j: (i, 0)),
        pl.BlockSpec((None, d, tn), lambda i, j: (layer, 0, jnp.minimum(j, n_w - 1))),
    ]
    out_specs = [pl.BlockSpec((tm, tn), lambda i, j: (i, j))]
    out_shape = [jax.ShapeDtypeStruct((rows, n), bf16)]
    args = [hb, rs, w]
    body = _proj_kernel
    if w_tail is not None:
        in_specs.append(pl.BlockSpec((None, d, tn), lambda i, j: (layer, 0, 0)))
        args.append(w_tail)
        body = _proj_tail_small_kernel
    elif w_small is not None:
        body = _proj_small_kernel
    if w_small is not None:
        in_specs.append(pl.BlockSpec((None, d, LANES), lambda i, j: (layer, 0, 0)))
        out_specs.append(pl.BlockSpec((tm, LANES), lambda i, j: (i, 0)))
        out_shape.append(jax.ShapeDtypeStruct((rows, LANES), f32))
        args.append(w_small)
    out = pl.pallas_call(
        body,
        grid=(rows // tm, n // tn),
        in_specs=in_specs,
        out_specs=out_specs,
        out_shape=out_shape,
        compiler_params=pltpu.CompilerParams(
            dimension_semantics=("arbitrary", "arbitrary"), vmem_limit_bytes=VMEM_LIMIT),
        name="proj",
    )(*args)
    return out[0] if w_small is None else out


def _out_proj_kernel(y_ref, w_ref, h_ref, o_ref, ob_ref, rs_ref):
    _emit_stream(h_ref[...] + jnp.dot(y_ref[...], w_ref[...], preferred_element_type=f32), o_ref, ob_ref, rs_ref)


def _out_proj_final_kernel(y_ref, w_ref, h_ref, g_ref, o_ref):
    x = h_ref[...] + jnp.dot(y_ref[...], w_ref[...], preferred_element_type=f32)
    var = jnp.mean(x * x, axis=-1, keepdims=True)
    o_ref[...] = x * lax.rsqrt(var + RMS_EPS) * g_ref[...]


def _out_proj(y, w, layer, h, final_g=None):
    _, k, d = w.shape
    if final_g is None:
        rows, tile, body, extra, extra_specs = h.shape[0], ROW_TILE, _out_proj_kernel, (), []
        out_specs, out_shape = _stream_out(rows, d, tile)
    else:
        rows, tile, body = ROWS_X, OUT_TILE, _out_proj_final_kernel
        extra, extra_specs = (final_g.reshape(1, d),), [pl.BlockSpec((1, d), lambda i: (0, 0))]
        out_specs, out_shape = pl.BlockSpec((tile, d), lambda i: (i, 0)), jax.ShapeDtypeStruct((rows, d), f32)
    return pl.pallas_call(
        body,
        grid=(rows // tile,),
        in_specs=[
            pl.BlockSpec((tile, k), lambda i: (i, 0)),
            pl.BlockSpec((None, k, d), lambda i: (layer, 0, 0)),
            pl.BlockSpec((tile, d), lambda i: (i, 0)),
        ] + extra_specs,
        out_specs=out_specs,
        out_shape=out_shape,
        compiler_params=pltpu.CompilerParams(
            dimension_semantics=("arbitrary",), vmem_limit_bytes=VMEM_LIMIT),
        name="out_proj" if final_g is None else "out_proj_final",
    )(y, w, h, *extra)


def _gla_out_kernel(o_ref, r_ref, w_ref, h_ref, out_ref, outb_ref, rs_ref):
    acc = h_ref[...]
    for hd in range(GLA_HEADS):
        cols = slice(hd * GLA_HEAD_V, (hd + 1) * GLA_HEAD_V)
        o = o_ref[:, cols].astype(f32)
        var = jnp.mean(o * o, axis=-1, keepdims=True)
        y = o * lax.rsqrt(var + RMS_EPS) * _silu(r_ref[:, cols].astype(f32))
        acc = acc + jnp.dot(y.astype(bf16), w_ref[cols, :], preferred_element_type=f32)
    _emit_stream(acc, out_ref, outb_ref, rs_ref)


def _gla_out_proj(o, proj, w, layer, h):
    _, k, d = w.shape
    rows = h.shape[0]
    out_specs, out_shape = _stream_out(rows, d, ROW_TILE)
    return pl.pallas_call(
        _gla_out_kernel,
        grid=(rows // ROW_TILE,),
        in_specs=[
            pl.BlockSpec((ROW_TILE, k), lambda i: (i, 0)),
            pl.BlockSpec((ROW_TILE, k), lambda i: (i, proj.shape[1] // k - 1)),
            pl.BlockSpec((None, k, d), lambda i: (layer, 0, 0)),
            pl.BlockSpec((ROW_TILE, d), lambda i: (i, 0)),
        ],
        out_specs=out_specs,
        out_shape=out_shape,
        compiler_params=pltpu.CompilerParams(
            dimension_semantics=("arbitrary",), vmem_limit_bytes=VMEM_LIMIT),
        name="gla_out_proj",
    )(o, proj, w, h)


def _gla_masks(c):
    i = lax.broadcasted_iota(jnp.int32, (c, c), 0)
    j = lax.broadcasted_iota(jnp.int32, (c, c), 1)
    base = min(c, GLA_BASE)
    uses = [((i & -base) == (j & -base)) & (j <= i)]
    size = 2 * base
    while size <= c:
        half = size // 2
        uses.append(((i & -size) == (j & -size)) & ((i & half) != 0) & ((j & half) == 0))
        size *= 2
    return (j <= i).astype(bf16), uses


def _gla_group(nc, c, masks, q, k, v, log2_a, state_t, emit):
    n, dk = q.shape
    nt = (((1,), (1,)), ((), ()))
    tn = (((0,), (0,)), ((), ()))
    tril, uses = masks
    row = lax.broadcasted_iota(jnp.int32, (n, 1), 0)

    b = jnp.concatenate([_split_dot(tril, log2_a[ci * c:(ci + 1) * c], 2, split_rhs=True)
                         for ci in range(nc)], axis=0)

    def minus_anchor(size):
        b3 = b.reshape(n // size, size, dk)
        return (b3 - b3[:, size // 2 - 1:size // 2, :]).reshape(n, dk)

    base = min(c, GLA_BASE)
    e0 = minus_anchor(base)
    operands = [((q * jnp.exp2(e0)).astype(bf16), (k * jnp.exp2(-e0)).astype(bf16))]
    size = 2 * base
    while size <= c:
        qk = (jnp.where((row & (size // 2)) != 0, q, k) * jnp.exp2(-jnp.abs(minus_anchor(size)))).astype(bf16)
        operands.append((qk, qk))
        size *= 2

    b3 = b.reshape(nc, c, dk)
    b_last = b3[:, c - 1:c, :]
    q_dec = (q * jnp.exp2(b)).astype(bf16)
    k_rem = (k.reshape(nc, c, dk) * jnp.exp2(b_last - b3)).reshape(n, dk).astype(bf16)
    decay = jnp.exp2(b_last)

    o_intra, update = [], []
    for ci in range(nc):
        rows = slice(ci * c, (ci + 1) * c)
        attn = None
        for (lhs, rhs), use in zip(operands, uses):
            p = lax.dot_general(lhs[rows], rhs[rows], nt, preferred_element_type=f32)
            attn = jnp.where(use, p, 0.0 if attn is None else attn)
        o_intra.append(jnp.dot(attn.astype(bf16), v[rows], preferred_element_type=f32))
        update.append(lax.dot_general(v[rows], k_rem[rows], tn, preferred_element_type=f32))

    for ci in range(nc):
        rows = slice(ci * c, (ci + 1) * c)
        emit(ci, o_intra[ci] + lax.dot_general(q_dec[rows], state_t.astype(bf16), nt,
                                               preferred_element_type=f32))
        state_t = state_t * decay[ci] + update[ci]
    return state_t


def _gla_kernel(q_ref, k_ref, v_ref, g_ref, wup_ref, bg_ref, o_ref, st_ref, st0_ref):
    t = pl.program_id(1)
    wup = wup_ref[...]
    bg = bg_ref[...]

    def run_group(row0, nc, c, masks, state_t):
        rows = pl.ds(row0, nc * c)
        logits = jnp.dot(g_ref[rows, :].astype(bf16), wup, preferred_element_type=f32) + bg
        log2_a = _log_sigmoid(logits) * (LOG2E / GLA_GATE_NORMALIZER)

        def emit(ci, o):
            o_ref[pl.ds(row0 + ci * c, c), :] = o.astype(o_ref.dtype)

        return _gla_group(nc, c, masks, q_ref[rows, :].astype(f32), k_ref[rows, :].astype(f32),
                          v_ref[rows, :], log2_a, state_t, emit)

    @pl.when(t == 0)
    def _():
        zero_state = jnp.zeros((GLA_HEAD_V, GLA_HEAD_K), f32)
        st0_ref[...] = run_group(0, 1, N_META, _gla_masks(N_META), zero_state)
        o_ref[N_META:LANES, :] = jnp.zeros((LANES - N_META, GLA_HEAD_V), o_ref.dtype)

    @pl.when(t > 0)
    def _():
        st_ref[...] = st0_ref[...]
        group_rows = GLA_GROUP * GLA_CHUNK
        masks = _gla_masks(GLA_CHUNK)

        def body(gi, carry):
            row0 = pl.multiple_of(gi * group_rows, group_rows)
            st_ref[...] = run_group(row0, GLA_GROUP, GLA_CHUNK, masks, st_ref[...])
            return carry

        lax.fori_loop(0, SEQ // group_rows, body, 0)


def _gla_mix(proj, g_low, w_up, b_gate, layer):
    kq = GLA_DK_TOTAL // GLA_HEAD_K
    kv = 2 * GLA_DK_TOTAL // GLA_HEAD_V
    return pl.pallas_call(
        _gla_kernel,
        grid=(GLA_HEADS, BATCH + 1),
        in_specs=[
            pl.BlockSpec((SEQ, GLA_HEAD_K), lambda h, t: (_seq_block(t), h)),
            pl.BlockSpec((SEQ, GLA_HEAD_K), lambda h, t: (_seq_block(t), kq + h)),
            pl.BlockSpec((SEQ, GLA_HEAD_V), lambda h, t: (_seq_block(t), kv + h)),
            pl.BlockSpec((SEQ, LANES), lambda h, t: (_seq_block(t), 0)),
            pl.BlockSpec((None, LANES, GLA_HEAD_K), lambda h, t: (layer, 0, h)),
            pl.BlockSpec((None, 1, GLA_HEAD_K), lambda h, t: (layer, 0, h)),
        ],
        out_specs=pl.BlockSpec((SEQ, GLA_HEAD_V), lambda h, t: (_seq_block(t), h)),
        out_shape=jax.ShapeDtypeStruct((ROWS, D_INNER), bf16),
        scratch_shapes=[pltpu.VMEM((GLA_HEAD_V, GLA_HEAD_K), f32),
                        pltpu.VMEM((GLA_HEAD_V, GLA_HEAD_K), f32)],
        compiler_params=pltpu.CompilerParams(
            dimension_semantics=("arbitrary", "arbitrary"), vmem_limit_bytes=VMEM_LIMIT),
        name="gla_mix",
    )(proj, proj, proj, g_low, w_up, b_gate)


def _bias_kernel(f_ref, b_ref, o_ref, meta_ref):
    t = pl.program_id(0)
    i = lax.broadcasted_iota(jnp.int32, (LANES, LANES), 0)
    j = lax.broadcasted_iota(jnp.int32, (LANES, LANES), 1)
    upper = (i <= j).astype(bf16)

    def block_cumsum(c, carry):
        log_f = _log_sigmoid(f_ref[c * LANES:(c + 1) * LANES, :] + b_ref[...])
        return _split_dot(log_f.T, upper, 3, split_rhs=False) + carry

    @pl.when(t == 0)
    def _():
        meta_ref[...] = block_cumsum(0, jnp.zeros((LANES, 1), f32))

    @pl.when(t > 0)
    def _():
        meta = meta_ref[...]
        o_ref[:, 0, 0, 0:N_META] = meta[0:FOX_HEADS, 0:N_META] * -LOG2E
        carry = meta[:, N_META - 1:N_META]
        for c in range(SEQ // LANES):
            cs = block_cumsum(c, carry)
            o_ref[:, 0, 0, N_META + c * LANES:N_META + (c + 1) * LANES] = cs[0:FOX_HEADS, :] * -LOG2E
            carry = cs[:, LANES - 1:LANES]
        o_ref[:, 0, 0, N_META + SEQ:KEYS] = jnp.zeros((FOX_HEADS, KEYS - N_META - SEQ), f32)


def _forget_bias(f_logit, b_forget):
    return pl.pallas_call(
        _bias_kernel,
        grid=(BATCH + 1,),
        in_specs=[
            pl.BlockSpec((SEQ, LANES), lambda t: (_seq_block(t), 0)),
            pl.BlockSpec((1, LANES), lambda t: (0, 0)),
        ],
        out_specs=pl.BlockSpec((FOX_HEADS, 1, 1, KEYS), lambda t: (0, jnp.maximum(t - 1, 0), 0, 0)),
        out_shape=jax.ShapeDtypeStruct((FOX_HEADS, BATCH, 1, KEYS), f32),
        scratch_shapes=[pltpu.VMEM((LANES, LANES), f32)],
        compiler_params=pltpu.CompilerParams(dimension_semantics=("arbitrary",)),
        name="forget_bias",
    )(f_logit, b_forget)


def _scores(q, k, bias, mask):
    s = lax.dot_general(q, k, (((1,), (1,)), ((), ())), preferred_element_type=f32) + bias
    return s if mask is None else jnp.where(mask, s, NEG_INF)


def _softmax_pv(scores, values):
    m = functools.reduce(jnp.maximum, [jnp.max(s, axis=-1, keepdims=True) for s in scores])
    acc = None
    for s, v in zip(scores, values):
        pv = jnp.dot(jnp.exp2(s - m).astype(bf16), v, preferred_element_type=f32)
        acc = pv if acc is None else acc + pv
    return acc[:, 0:FOX_HEAD_DIM] * (1.0 / acc[:, FOX_HEAD_DIM:])


def _fox_kernel(q_ref, r_ref, k_ref, v_ref, b_ref, y_ref, kc_ref, vc_ref):
    t = pl.program_id(1)
    dh = FOX_HEAD_DIM

    def lower_tri(nq, nk, shift):
        i = lax.broadcasted_iota(jnp.int32, (nq, nk), 0)
        j = lax.broadcasted_iota(jnp.int32, (nq, nk), 1)
        return j <= i + shift

    def finish(rows, o):
        y_ref[rows, :] = (o * _silu(r_ref[rows, :].astype(f32))).astype(y_ref.dtype)

    @pl.when(t == 0)
    def _():
        for s in range(FOX_STREAMS):
            vc_ref[s, :, dh:] = jnp.ones((KEYS, dh), bf16)
            kc_ref[s, 0:N_META, :] = k_ref[0:N_META, :]
            vc_ref[s, 0:N_META, 0:dh] = v_ref[0:N_META, :]
        sc = _scores(q_ref[0:N_META, :], kc_ref[0, 0:N_META, :], b_ref[0, 0, :, 0:N_META],
                     lower_tri(N_META, N_META, 0))
        finish(pl.ds(0, N_META), _softmax_pv([sc], [vc_ref[0, 0:N_META, :]]))
        y_ref[N_META:LANES, :] = jnp.zeros((LANES - N_META, dh), y_ref.dtype)

    @pl.when(t > 0)
    def _():
        n_keys = N_META + SEQ
        for s in range(FOX_STREAMS):
            kc_ref[s, N_META:n_keys, :] = k_ref[s * SEQ:(s + 1) * SEQ, :]
            vc_ref[s, N_META:n_keys, 0:dh] = v_ref[s * SEQ:(s + 1) * SEQ, :]

        def tile_scores(s, ti):
            p0 = ti * FOX_TQ
            k1 = min(p0 + FOX_TQ, n_keys)
            q0 = max(p0 - N_META, 0)
            q1 = k1 - N_META
            q = q_ref[s * SEQ + q0:s * SEQ + q1, :]
            k0 = p0 // MXU_COLS * MXU_COLS
            scores, values = [], []
            if k0 > 0:
                scores.append(_scores(q, kc_ref[s, 0:k0, :], b_ref[0, s, :, 0:k0], None))
                values.append(vc_ref[s, 0:k0, :])
            shift = q0 + N_META - k0
            scores.append(_scores(q, kc_ref[s, k0:k1, :], b_ref[0, s, :, k0:k1],
                                  lower_tri(q1 - q0, k1 - k0, shift)))
            values.append(vc_ref[s, k0:k1, :])
            return pl.ds(s * SEQ + q0, q1 - q0), scores, values

        n_tiles = -(-n_keys // FOX_TQ)
        order = [(s, ti if s % 2 == 0 else n_tiles - 1 - ti) for ti in range(n_tiles) for s in range(FOX_STREAMS)]
        pending = tile_scores(*order[0])
        for pos in range(len(order)):
            rows, scores, values = pending
            if pos + 1 < len(order):
                pending = tile_scores(*order[pos + 1])
            finish(rows, _softmax_pv(scores, values))


def _fox_mix(qr, kv, bias):
    nh = FOX_HEADS
    rows = FOX_STREAMS * SEQ
    n_steps = BATCH // FOX_STREAMS

    def row_block(t):
        return jnp.where(t == 0, n_steps, t - 1)

    return pl.pallas_call(
        _fox_kernel,
        grid=(FOX_HEADS, n_steps + 1),
        in_specs=[
            pl.BlockSpec((rows, FOX_HEAD_DIM), lambda h, t: (row_block(t), h)),
            pl.BlockSpec((rows, FOX_HEAD_DIM), lambda h, t: (row_block(t), nh + h)),
            pl.BlockSpec((rows, FOX_HEAD_DIM), lambda h, t: (row_block(t), h)),
            pl.BlockSpec((rows, FOX_HEAD_DIM), lambda h, t: (row_block(t), nh + h)),
            pl.BlockSpec((1, FOX_STREAMS, 1, KEYS), lambda h, t: (h, jnp.maximum(t - 1, 0), 0, 0)),
        ],
        out_specs=pl.BlockSpec((rows, FOX_HEAD_DIM), lambda h, t: (row_block(t), h)),
        out_shape=jax.ShapeDtypeStruct((ROWS, D_INNER), bf16),
        scratch_shapes=[pltpu.VMEM((FOX_STREAMS, KEYS, FOX_HEAD_DIM), bf16),
                        pltpu.VMEM((FOX_STREAMS, KEYS, 2 * FOX_HEAD_DIM), bf16)],
        compiler_params=pltpu.CompilerParams(
            dimension_semantics=("arbitrary", "arbitrary"), vmem_limit_bytes=VMEM_LIMIT),
        name="fox_mix",
    )(qr, qr, kv, kv, bias)


def _prep_kernel(outputs, w_ref, g_ref, *out_refs):
    g = g_ref[...]
    for (width, pieces), o_ref in zip(outputs, out_refs):
        if sum(p[1] for p in pieces) < width:
            o_ref[...] = jnp.zeros(o_ref.shape, o_ref.dtype)
        for src, cols, dst, scale in pieces:
            o_ref[:, dst:dst + cols] = (w_ref[:, src:src + cols] * (g * scale)).astype(o_ref.dtype)


def _prep_weight(w, row_gain, outputs):
    layers, k, n = w.shape
    return pl.pallas_call(
        functools.partial(_prep_kernel, outputs),
        grid=(layers, k // PREP_TILE),
        in_specs=[
            pl.BlockSpec((None, PREP_TILE, n), lambda l, i: (l, i, 0)),
            pl.BlockSpec((None, PREP_TILE, 1), lambda l, i: (l, i, 0)),
        ],
        out_specs=[pl.BlockSpec((None, PREP_TILE, width), lambda l, i: (l, i, 0)) for width, _ in outputs],
        out_shape=[jax.ShapeDtypeStruct((layers, k, width), bf16) for width, _ in outputs],
        compiler_params=pltpu.CompilerParams(
            dimension_semantics=("arbitrary", "arbitrary"), vmem_limit_bytes=VMEM_LIMIT),
        name="prep_weight",
    )(w, row_gain.reshape(layers, k, 1))


def _pad_last(w, axis):
    pad = [(0, 0)] * w.ndim
    pad[axis] = (0, LANES - w.shape[axis])
    return jnp.pad(w, pad)


def kernel(x, meta_tokens, norm_g, gla_w_in, gla_w_gate_up, gla_b_gate, gla_head_norm_g, gla_w_out,
           kv_norm_g, fox_w_kv, fox_b_forget, fox_w_in, fox_w_out, final_norm_g):
    n_gla = gla_w_in.shape[0]
    n_fox = fox_w_in.shape[0]

    o_g = 2 * GLA_DK_TOTAL + D_INNER
    o_r = o_g + GLA_GATE_RANK
    gla_w_qkv, gla_w_r, gla_w_low = _prep_weight(gla_w_in, norm_g[:n_gla], [
        (o_g, [(0, GLA_DK_TOTAL, 0, GLA_HEAD_K ** -0.5), (GLA_DK_TOTAL, o_g - GLA_DK_TOTAL, GLA_DK_TOTAL, 1.0)]),
        (D_INNER, [(o_r, D_INNER, 0, 1.0)]),
        (LANES, [(o_g, GLA_GATE_RANK, 0, 1.0)])])
    gla_w_up = _pad_last(gla_w_gate_up, 1).astype(bf16)
    gla_w_o, = _prep_weight(gla_w_out, jnp.tile(gla_head_norm_g, (1, GLA_HEADS)), [(D_MODEL, [(0, D_MODEL, 0, 1.0)])])
    kv_w, kv_w_f = _prep_weight(fox_w_kv[None], kv_norm_g[None], [
        (2 * D_INNER, [(0, 2 * D_INNER, 0, 1.0)]),
        (LANES, [(2 * D_INNER, FOX_HEADS, 0, 1.0)])])
    fox_w_qr, = _prep_weight(fox_w_in, norm_g[n_gla:], [
        (2 * D_INNER, [(0, D_INNER, 0, FOX_HEAD_DIM ** -0.5 * LOG2E), (D_INNER, D_INNER, D_INNER, 1.0)])])
    fox_w_o, = _prep_weight(fox_w_out, jnp.ones(fox_w_out.shape[:2], f32), [(D_MODEL, [(0, D_MODEL, 0, 1.0)])])

    h, hb, rs = _embed(x.reshape(ROWS_X, D_MODEL), meta_tokens)
    for layer in range(n_gla):
        proj, g_low = _proj(hb, rs, gla_w_qkv, layer, o_g + D_INNER, w_tail=gla_w_r, w_small=gla_w_low)
        o = _gla_mix(proj, g_low, gla_w_up, gla_b_gate.reshape(n_gla, 1, -1), layer)
        h, hb, rs = _gla_out_proj(o, proj, gla_w_o, layer, h)

    kv, f_logit = _proj(hb, rs, kv_w, 0, 2 * D_INNER, w_small=kv_w_f)
    bias = _forget_bias(f_logit, _pad_last(fox_b_forget.reshape(1, -1), 1))

    for j in range(n_fox):
        qr = _proj(hb, rs, fox_w_qr, j, 2 * D_INNER)
        y = _fox_mix(qr, kv, bias)
        if j + 1 < n_fox:
            h, hb, rs = _out_proj(y, fox_w_o, j, h)
        else:
            out = _out_proj(y, fox_w_o, j, h, final_g=final_norm_g)
    return out.reshape(BATCH, SEQ, D_MODEL)
```

```python
import functools

import jax
import jax.numpy as jnp
from jax import lax
from jax.experimental import pallas as pl
from jax.experimental.pallas import tpu as pltpu

f32 = jnp.float32
bf16 = jnp.bfloat16

D_MODEL = 1024
BATCH = 8
SEQ = 2048
N_META = 16
D_INNER = 2048
GLA_HEADS = 4
GLA_HEAD_K = 256
GLA_HEAD_V = 512
GLA_DK_TOTAL = GLA_HEADS * GLA_HEAD_K
GLA_GATE_RANK = 16
GLA_GATE_NORMALIZER = 16.0
FOX_HEADS = 16
FOX_HEAD_DIM = 128
RMS_EPS = 1e-6
NEG_INF = -1e30
LOG2E = 1.4426950408889634

LANES = 128
ROWS_X = BATCH * SEQ
ROWS = ROWS_X + LANES
META_BLOCK = BATCH
VMEM_LIMIT = 56 * 1024 * 1024

GLA_CHUNK = 128
GLA_BASE = 32
GLA_GROUP = 16
MXU_COLS = 256
FOX_TQ = 256
KEYS = 2304
FOX_STREAMS = 2
ROW_TILE = 688
PROJ_TILE = 1376
PROJ_COLS = 2048
OUT_TILE = 512
EMBED_TILE = 512
PREP_TILE = 256


def _seq_block(t):
    return jnp.where(t == 0, META_BLOCK, t - 1)


def _log_sigmoid(x):
    return jnp.minimum(x, 0.0) - jnp.log1p(jnp.exp(-jnp.abs(x)))


def _silu(x):
    return x * (1.0 / (1.0 + jnp.exp(-x)))


def _split_dot(a, b, pieces, split_rhs):
    x = b if split_rhs else a
    acc = None
    for _ in range(pieces):
        p = x.astype(bf16)
        term = (jnp.dot(a, p, preferred_element_type=f32) if split_rhs
                else jnp.dot(p, b, preferred_element_type=f32))
        acc = term if acc is None else acc + term
        x = x - p.astype(f32)
    return acc


def _emit_stream(h, h_ref, hb_ref, rs_ref):
    h_ref[...] = h
    hb_ref[...] = h.astype(bf16)
    rs_ref[...] = lax.rsqrt(jnp.mean(h * h, axis=-1, keepdims=True) + RMS_EPS)


def _stream_out(rows, d, tile):
    specs = [pl.BlockSpec((tile, d), lambda i: (i, 0)),
             pl.BlockSpec((tile, d), lambda i: (i, 0)),
             pl.BlockSpec((tile, 1), lambda i: (i, 0))]
    shapes = [jax.ShapeDtypeStruct((rows, d), f32),
              jax.ShapeDtypeStruct((rows, d), bf16),
              jax.ShapeDtypeStruct((rows, 1), f32)]
    return specs, shapes


def _embed_kernel(x_ref, meta_ref, h_ref, hb_ref, rs_ref):
    i = pl.program_id(0)

    @pl.when(i < ROWS_X // EMBED_TILE)
    def _():
        _emit_stream(x_ref[...], h_ref, hb_ref, rs_ref)

    @pl.when(i == ROWS_X // EMBED_TILE)
    def _():
        pad = jnp.zeros((EMBED_TILE - N_META, D_MODEL), f32)
        _emit_stream(jnp.concatenate([meta_ref[...], pad], axis=0), h_ref, hb_ref, rs_ref)


def _embed(x2d, meta_tokens):
    n_x = ROWS_X // EMBED_TILE
    specs, shapes = _stream_out(ROWS, D_MODEL, EMBED_TILE)
    return pl.pallas_call(
        _embed_kernel,
        grid=(n_x + 1,),
        in_specs=[
            pl.BlockSpec((EMBED_TILE, D_MODEL), lambda i: (jnp.minimum(i, n_x - 1), 0)),
            pl.BlockSpec((N_META, D_MODEL), lambda i: (0, 0)),
        ],
        out_specs=specs,
        out_shape=shapes,
        compiler_params=pltpu.CompilerParams(dimension_semantics=("arbitrary",), vmem_limit_bytes=VMEM_LIMIT),
        name="embed",
    )(x2d, meta_tokens)


def _proj_dot(transposed, x, w):
    dims = (((1,), (1 if transposed else 0,)), ((), ()))
    return lax.dot_general(x, w, dims, preferred_element_type=f32)


def _proj_store(transposed, xb_ref, rs_ref, w_ref, o_ref):
    o_ref[...] = (_proj_dot(transposed, xb_ref[...], w_ref[...]) * rs_ref[...]).astype(o_ref.dtype)


def _proj_kernel(transposed, xb_ref, rs_ref, w_ref, o_ref):
    _proj_store(transposed, xb_ref, rs_ref, w_ref, o_ref)


def _proj_small(transposed, xb_ref, rs_ref, ws_ref, os_ref):
    @pl.when(pl.program_id(1) == 0)
    def _():
        os_ref[...] = _proj_dot(transposed, xb_ref[...], ws_ref[...]) * rs_ref[...]


def _proj_small_kernel(transposed, xb_ref, rs_ref, w_ref, ws_ref, o_ref, os_ref):
    _proj_small(transposed, xb_ref, rs_ref, ws_ref, os_ref)
    _proj_store(transposed, xb_ref, rs_ref, w_ref, o_ref)


def _proj_tail_small_kernel(transposed, xb_ref, rs_ref, w_ref, wt_ref, ws_ref, o_ref, os_ref):
    _proj_small(transposed, xb_ref, rs_ref, ws_ref, os_ref)
    last = pl.num_programs(1) - 1

    @pl.when(pl.program_id(1) < last)
    def _():
        _proj_store(transposed, xb_ref, rs_ref, w_ref, o_ref)

    @pl.when(pl.program_id(1) == last)
    def _():
        _proj_store(transposed, xb_ref, rs_ref, wt_ref, o_ref)


def _proj(hb, rs, w, layer, n, w_tail=None, w_small=None, transposed=False):
    rows, d = hb.shape
    tm, tn = PROJ_TILE, PROJ_COLS
    assert rows % tm == 0 and n % tn == 0 and (w_tail is None or w_small is not None)
    n_w = n // tn - (w_tail is not None)

    def w_spec(cols, col_block):
        if transposed:
            return pl.BlockSpec((None, cols, d), lambda i, j: (layer, col_block(j), 0))
        return pl.BlockSpec((None, d, cols), lambda i, j: (layer, 0, col_block(j)))

    in_specs = [
        pl.BlockSpec((tm, d), lambda i, j: (i, 0)),
        pl.BlockSpec((tm, 1), lambda i, j: (i, 0)),
        w_spec(tn, lambda j: jnp.minimum(j, n_w - 1)),
    ]
    out_specs = [pl.BlockSpec((tm, tn), lambda i, j: (i, j))]
    out_shape = [jax.ShapeDtypeStruct((rows, n), bf16)]
    args = [hb, rs, w]
    body = _proj_kernel
    if w_tail is not None:
        in_specs.append(w_spec(tn, lambda j: 0))
        args.append(w_tail)
        body = _proj_tail_small_kernel
    elif w_small is not None:
        body = _proj_small_kernel
    if w_small is not None:
        in_specs.append(w_spec(LANES, lambda j: 0))
        out_specs.append(pl.BlockSpec((tm, LANES), lambda i, j: (i, 0)))
        out_shape.append(jax.ShapeDtypeStruct((rows, LANES), f32))
        args.append(w_small)
    out = pl.pallas_call(
        functools.partial(body, transposed),
        grid=(rows // tm, n // tn),
        in_specs=in_specs,
        out_specs=out_specs,
        out_shape=out_shape,
        compiler_params=pltpu.CompilerParams(
            dimension_semantics=("arbitrary", "arbitrary"), vmem_limit_bytes=VMEM_LIMIT),
        name="proj",
    )(*args)
    return out[0] if w_small is None else out


def _out_proj_kernel(y_ref, w_ref, h_ref, o_ref, ob_ref, rs_ref):
    _emit_stream(h_ref[...] + jnp.dot(y_ref[...], w_ref[...], preferred_element_type=f32), o_ref, ob_ref, rs_ref)


def _out_proj_final_kernel(y_ref, w_ref, h_ref, g_ref, o_ref):
    x = h_ref[...] + jnp.dot(y_ref[...], w_ref[...], preferred_element_type=f32)
    var = jnp.mean(x * x, axis=-1, keepdims=True)
    o_ref[...] = x * lax.rsqrt(var + RMS_EPS) * g_ref[...]


def _out_proj(y, w, layer, h, final_g=None):
    _, k, d = w.shape
    if final_g is None:
        rows, tile, body, extra, extra_specs = h.shape[0], ROW_TILE, _out_proj_kernel, (), []
        out_specs, out_shape = _stream_out(rows, d, tile)
    else:
        rows, tile, body = ROWS_X, OUT_TILE, _out_proj_final_kernel
        extra, extra_specs = (final_g.reshape(1, d),), [pl.BlockSpec((1, d), lambda i: (0, 0))]
        out_specs, out_shape = pl.BlockSpec((tile, d), lambda i: (i, 0)), jax.ShapeDtypeStruct((rows, d), f32)
    return pl.pallas_call(
        body,
        grid=(rows // tile,),
        in_specs=[
            pl.BlockSpec((tile, k), lambda i: (i, 0)),
            pl.BlockSpec((None, k, d), lambda i: (layer, 0, 0)),
            pl.BlockSpec((tile, d), lambda i: (i, 0)),
        ] + extra_specs,
        out_specs=out_specs,
        out_shape=out_shape,
        compiler_params=pltpu.CompilerParams(
            dimension_semantics=("arbitrary",), vmem_limit_bytes=VMEM_LIMIT),
        name="out_proj" if final_g is None else "out_proj_final",
    )(y, w, h, *extra)


def _gla_out_kernel(o_ref, r_ref, w_ref, h_ref, out_ref, outb_ref, rs_ref):
    acc = h_ref[...]
    for hd in range(GLA_HEADS):
        cols = slice(hd * GLA_HEAD_V, (hd + 1) * GLA_HEAD_V)
        o = o_ref[:, cols].astype(f32)
        var = jnp.mean(o * o, axis=-1, keepdims=True)
        y = o * lax.rsqrt(var + RMS_EPS) * _silu(r_ref[:, cols].astype(f32))
        acc = acc + jnp.dot(y.astype(bf16), w_ref[cols, :], preferred_element_type=f32)
    _emit_stream(acc, out_ref, outb_ref, rs_ref)


def _gla_out_proj(o, proj, w, layer, h):
    _, k, d = w.shape
    rows = h.shape[0]
    out_specs, out_shape = _stream_out(rows, d, ROW_TILE)
    return pl.pallas_call(
        _gla_out_kernel,
        grid=(rows // ROW_TILE,),
        in_specs=[
            pl.BlockSpec((ROW_TILE, k), lambda i: (i, 0)),
            pl.BlockSpec((ROW_TILE, k), lambda i: (i, proj.shape[1] // k - 1)),
            pl.BlockSpec((None, k, d), lambda i: (layer, 0, 0)),
            pl.BlockSpec((ROW_TILE, d), lambda i: (i, 0)),
        ],
        out_specs=out_specs,
        out_shape=out_shape,
        compiler_params=pltpu.CompilerParams(
            dimension_semantics=("arbitrary",), vmem_limit_bytes=VMEM_LIMIT),
        name="gla_out_proj",
    )(o, proj, w, h)


def _gla_masks(c):
    i = lax.broadcasted_iota(jnp.int32, (c, c), 0)
    j = lax.broadcasted_iota(jnp.int32, (c, c), 1)
    base = min(c, GLA_BASE)
    uses = [((i & -base) == (j & -base)) & (j <= i)]
    size = 2 * base
    while size <= c:
        half = size // 2
        uses.append(((i & -size) == (j & -size)) & ((i & half) != 0) & ((j & half) == 0))
        size *= 2
    return (j <= i).astype(bf16), uses


def _gla_group(nc, c, masks, q, k, v, log2_a, state_t, emit):
    n, dk = q.shape
    nt = (((1,), (1,)), ((), ()))
    tn = (((0,), (0,)), ((), ()))
    tril, uses = masks
    row = lax.broadcasted_iota(jnp.int32, (n, 1), 0)

    b = jnp.concatenate([_split_dot(tril, log2_a[ci * c:(ci + 1) * c], 2, split_rhs=True)
                         for ci in range(nc)], axis=0)

    def minus_anchor(size):
        b3 = b.reshape(n // size, size, dk)
        return (b3 - b3[:, size // 2 - 1:size // 2, :]).reshape(n, dk)

    base = min(c, GLA_BASE)
    e0 = minus_anchor(base)
    operands = [((q * jnp.exp2(e0)).astype(bf16), (k * jnp.exp2(-e0)).astype(bf16))]
    size = 2 * base
    while size <= c:
        qk = (jnp.where((row & (size // 2)) != 0, q, k) * jnp.exp2(-jnp.abs(minus_anchor(size)))).astype(bf16)
        operands.append((qk, qk))
        size *= 2

    b3 = b.reshape(nc, c, dk)
    b_last = b3[:, c - 1:c, :]
    q_dec = (q * jnp.exp2(b)).astype(bf16)
    k_rem = (k.reshape(nc, c, dk) * jnp.exp2(b_last - b3)).reshape(n, dk).astype(bf16)
    decay = jnp.exp2(b_last)

    o_intra, update = [], []
    for ci in range(nc):
        rows = slice(ci * c, (ci + 1) * c)
        attn = None
        for (lhs, rhs), use in zip(operands, uses):
            p = lax.dot_general(lhs[rows], rhs[rows], nt, preferred_element_type=f32)
            attn = jnp.where(use, p, 0.0 if attn is None else attn)
        o_intra.append(jnp.dot(attn.astype(bf16), v[rows], preferred_element_type=f32))
        update.append(lax.dot_general(v[rows], k_rem[rows], tn, preferred_element_type=f32))

    for ci in range(nc):
        rows = slice(ci * c, (ci + 1) * c)
        emit(ci, o_intra[ci] + lax.dot_general(q_dec[rows], state_t.astype(bf16), nt,
                                               preferred_element_type=f32))
        state_t = state_t * decay[ci] + update[ci]
    return state_t


def _gla_kernel(q_ref, k_ref, v_ref, g_ref, wup_ref, bg_ref, o_ref, st_ref, st0_ref):
    t = pl.program_id(1)
    wup = wup_ref[...]
    bg = bg_ref[...]

    def run_group(row0, nc, c, masks, state_t):
        rows = pl.ds(row0, nc * c)
        logits = jnp.dot(g_ref[rows, :].astype(bf16), wup, preferred_element_type=f32) + bg
        log2_a = _log_sigmoid(logits) * (LOG2E / GLA_GATE_NORMALIZER)

        def emit(ci, o):
            o_ref[pl.ds(row0 + ci * c, c), :] = o.astype(o_ref.dtype)

        return _gla_group(nc, c, masks, q_ref[rows, :].astype(f32), k_ref[rows, :].astype(f32),
                          v_ref[rows, :], log2_a, state_t, emit)

    @pl.when(t == 0)
    def _():
        zero_state = jnp.zeros((GLA_HEAD_V, GLA_HEAD_K), f32)
        st0_ref[...] = run_group(0, 1, N_META, _gla_masks(N_META), zero_state)
        o_ref[N_META:LANES, :] = jnp.zeros((LANES - N_META, GLA_HEAD_V), o_ref.dtype)

    @pl.when(t > 0)
    def _():
        st_ref[...] = st0_ref[...]
        group_rows = GLA_GROUP * GLA_CHUNK
        masks = _gla_masks(GLA_CHUNK)

        def body(gi, carry):
            row0 = pl.multiple_of(gi * group_rows, group_rows)
            st_ref[...] = run_group(row0, GLA_GROUP, GLA_CHUNK, masks, st_ref[...])
            return carry

        lax.fori_loop(0, SEQ // group_rows, body, 0)


def _gla_mix(proj, g_low, w_up, b_gate, layer):
    kq = GLA_DK_TOTAL // GLA_HEAD_K
    kv = 2 * GLA_DK_TOTAL // GLA_HEAD_V
    return pl.pallas_call(
        _gla_kernel,
        grid=(GLA_HEADS, BATCH + 1),
        in_specs=[
            pl.BlockSpec((SEQ, GLA_HEAD_K), lambda h, t: (_seq_block(t), h)),
            pl.BlockSpec((SEQ, GLA_HEAD_K), lambda h, t: (_seq_block(t), kq + h)),
            pl.BlockSpec((SEQ, GLA_HEAD_V), lambda h, t: (_seq_block(t), kv + h)),
            pl.BlockSpec((SEQ, LANES), lambda h, t: (_seq_block(t), 0)),
            pl.BlockSpec((None, LANES, GLA_HEAD_K), lambda h, t: (layer, 0, h)),
            pl.BlockSpec((None, 1, GLA_HEAD_K), lambda h, t: (layer, 0, h)),
        ],
        out_specs=pl.BlockSpec((SEQ, GLA_HEAD_V), lambda h, t: (_seq_block(t), h)),
        out_shape=jax.ShapeDtypeStruct((ROWS, D_INNER), bf16),
        scratch_shapes=[pltpu.VMEM((GLA_HEAD_V, GLA_HEAD_K), f32),
                        pltpu.VMEM((GLA_HEAD_V, GLA_HEAD_K), f32)],
        compiler_params=pltpu.CompilerParams(
            dimension_semantics=("arbitrary", "arbitrary"), vmem_limit_bytes=VMEM_LIMIT),
        name="gla_mix",
    )(proj, proj, proj, g_low, w_up, b_gate)


def _bias_kernel(f_ref, b_ref, o_ref, meta_ref):
    t = pl.program_id(0)
    i = lax.broadcasted_iota(jnp.int32, (LANES, LANES), 0)
    j = lax.broadcasted_iota(jnp.int32, (LANES, LANES), 1)
    upper = (i <= j).astype(bf16)

    def block_cumsum(c, carry):
        log_f = _log_sigmoid(f_ref[c * LANES:(c + 1) * LANES, :] + b_ref[...])
        return _split_dot(log_f.T, upper, 3, split_rhs=False) + carry

    @pl.when(t == 0)
    def _():
        meta_ref[...] = block_cumsum(0, jnp.zeros((LANES, 1), f32))

    @pl.when(t > 0)
    def _():
        meta = meta_ref[...]
        o_ref[:, 0, 0, 0:N_META] = meta[0:FOX_HEADS, 0:N_META] * -LOG2E
        carry = meta[:, N_META - 1:N_META]
        for c in range(SEQ // LANES):
            cs = block_cumsum(c, carry)
            o_ref[:, 0, 0, N_META + c * LANES:N_META + (c + 1) * LANES] = cs[0:FOX_HEADS, :] * -LOG2E
            carry = cs[:, LANES - 1:LANES]
        o_ref[:, 0, 0, N_META + SEQ:KEYS] = jnp.zeros((FOX_HEADS, KEYS - N_META - SEQ), f32)


def _forget_bias(f_logit, b_forget):
    return pl.pallas_call(
        _bias_kernel,
        grid=(BATCH + 1,),
        in_specs=[
            pl.BlockSpec((SEQ, LANES), lambda t: (_seq_block(t), 0)),
            pl.BlockSpec((1, LANES), lambda t: (0, 0)),
        ],
        out_specs=pl.BlockSpec((FOX_HEADS, 1, 1, KEYS), lambda t: (0, jnp.maximum(t - 1, 0), 0, 0)),
        out_shape=jax.ShapeDtypeStruct((FOX_HEADS, BATCH, 1, KEYS), f32),
        scratch_shapes=[pltpu.VMEM((LANES, LANES), f32)],
        compiler_params=pltpu.CompilerParams(dimension_semantics=("arbitrary",)),
        name="forget_bias",
    )(f_logit, b_forget)


def _scores(q, k, bias, mask):
    s = lax.dot_general(q, k, (((1,), (1,)), ((), ())), preferred_element_type=f32) + bias
    return s if mask is None else jnp.where(mask, s, NEG_INF)


def _softmax_pv(scores, values):
    m = functools.reduce(jnp.maximum, [jnp.max(s, axis=-1, keepdims=True) for s in scores])
    acc = None
    for s, v in zip(scores, values):
        pv = jnp.dot(jnp.exp2(s - m).astype(bf16), v, preferred_element_type=f32)
        acc = pv if acc is None else acc + pv
    return acc[:, 0:FOX_HEAD_DIM] * (1.0 / acc[:, FOX_HEAD_DIM:])


def _fox_kernel(q_ref, r_ref, k_ref, v_ref, b_ref, y_ref, kc_ref, vc_ref):
    t = pl.program_id(1)
    dh = FOX_HEAD_DIM

    def lower_tri(nq, nk, shift):
        i = lax.broadcasted_iota(jnp.int32, (nq, nk), 0)
        j = lax.broadcasted_iota(jnp.int32, (nq, nk), 1)
        return j <= i + shift

    def finish(rows, o):
        y_ref[rows, :] = (o * _silu(r_ref[rows, :].astype(f32))).astype(y_ref.dtype)

    @pl.when(t == 0)
    def _():
        for s in range(FOX_STREAMS):
            vc_ref[s, :, dh:] = jnp.ones((KEYS, dh), bf16)
            kc_ref[s, 0:N_META, :] = k_ref[0:N_META, :]
            vc_ref[s, 0:N_META, 0:dh] = v_ref[0:N_META, :]
        sc = _scores(q_ref[0:N_META, :], kc_ref[0, 0:N_META, :], b_ref[0, 0, :, 0:N_META],
                     lower_tri(N_META, N_META, 0))
        finish(pl.ds(0, N_META), _softmax_pv([sc], [vc_ref[0, 0:N_META, :]]))
        y_ref[N_META:LANES, :] = jnp.zeros((LANES - N_META, dh), y_ref.dtype)

    @pl.when(t > 0)
    def _():
        n_keys = N_META + SEQ
        for s in range(FOX_STREAMS):
            kc_ref[s, N_META:n_keys, :] = k_ref[s * SEQ:(s + 1) * SEQ, :]
            vc_ref[s, N_META:n_keys, 0:dh] = v_ref[s * SEQ:(s + 1) * SEQ, :]

        def tile_scores(s, ti):
            p0 = ti * FOX_TQ
            k1 = min(p0 + FOX_TQ, n_keys)
            q0 = max(p0 - N_META, 0)
            q1 = k1 - N_META
            q = q_ref[s * SEQ + q0:s * SEQ + q1, :]
            k0 = p0 // MXU_COLS * MXU_COLS
            scores, values = [], []
            if k0 > 0:
                scores.append(_scores(q, kc_ref[s, 0:k0, :], b_ref[0, s, :, 0:k0], None))
                values.append(vc_ref[s, 0:k0, :])
            shift = q0 + N_META - k0
            scores.append(_scores(q, kc_ref[s, k0:k1, :], b_ref[0, s, :, k0:k1],
                                  lower_tri(q1 - q0, k1 - k0, shift)))
            values.append(vc_ref[s, k0:k1, :])
            return pl.ds(s * SEQ + q0, q1 - q0), scores, values

        n_tiles = -(-n_keys // FOX_TQ)
        order = [(s, ti if s % 2 == 0 else n_tiles - 1 - ti) for ti in range(n_tiles) for s in range(FOX_STREAMS)]
        pending = tile_scores(*order[0])
        for pos in range(len(order)):
            rows, scores, values = pending
            if pos + 1 < len(order):
                pending = tile_scores(*order[pos + 1])
            finish(rows, _softmax_pv(scores, values))


def _fox_mix(qr, kv, bias):
    nh = FOX_HEADS
    rows = FOX_STREAMS * SEQ
    n_steps = BATCH // FOX_STREAMS

    def row_block(t):
        return jnp.where(t == 0, n_steps, t - 1)

    return pl.pallas_call(
        _fox_kernel,
        grid=(FOX_HEADS, n_steps + 1),
        in_specs=[
            pl.BlockSpec((rows, FOX_HEAD_DIM), lambda h, t: (row_block(t), h)),
            pl.BlockSpec((rows, FOX_HEAD_DIM), lambda h, t: (row_block(t), nh + h)),
            pl.BlockSpec((rows, FOX_HEAD_DIM), lambda h, t: (row_block(t), h)),
            pl.BlockSpec((rows, FOX_HEAD_DIM), lambda h, t: (row_block(t), nh + h)),
            pl.BlockSpec((1, FOX_STREAMS, 1, KEYS), lambda h, t: (h, jnp.maximum(t - 1, 0), 0, 0)),
        ],
        out_specs=pl.BlockSpec((rows, FOX_HEAD_DIM), lambda h, t: (row_block(t), h)),
        out_shape=jax.ShapeDtypeStruct((ROWS, D_INNER), bf16),
        scratch_shapes=[pltpu.VMEM((FOX_STREAMS, KEYS, FOX_HEAD_DIM), bf16),
                        pltpu.VMEM((FOX_STREAMS, KEYS, 2 * FOX_HEAD_DIM), bf16)],
        compiler_params=pltpu.CompilerParams(
            dimension_semantics=("arbitrary", "arbitrary"), vmem_limit_bytes=VMEM_LIMIT),
        name="fox_mix",
    )(qr, qr, kv, kv, bias)


def _prep_kernel(outputs, w_ref, g_ref, *out_refs):
    g = g_ref[...]
    for (width, pieces), o_ref in zip(outputs, out_refs):
        if sum(p[1] for p in pieces) < width:
            o_ref[...] = jnp.zeros(o_ref.shape, o_ref.dtype)
        for src, cols, dst, scale in pieces:
            o_ref[:, dst:dst + cols] = (w_ref[:, src:src + cols] * (g * scale)).astype(o_ref.dtype)


def _prep_weight(w, row_gain, outputs):
    layers, k, n = w.shape
    return pl.pallas_call(
        functools.partial(_prep_kernel, outputs),
        grid=(layers, k // PREP_TILE),
        in_specs=[
            pl.BlockSpec((None, PREP_TILE, n), lambda l, i: (l, i, 0)),
            pl.BlockSpec((None, PREP_TILE, 1), lambda l, i: (l, i, 0)),
        ],
        out_specs=[pl.BlockSpec((None, PREP_TILE, width), lambda l, i: (l, i, 0)) for width, _ in outputs],
        out_shape=[jax.ShapeDtypeStruct((layers, k, width), bf16) for width, _ in outputs],
        compiler_params=pltpu.CompilerParams(
            dimension_semantics=("arbitrary", "arbitrary"), vmem_limit_bytes=VMEM_LIMIT),
        name="prep_weight",
    )(w, row_gain.reshape(layers, k, 1))


def _prep_rows_kernel(scaled_tiles, scale, w_ref, g_ref, o_ref):
    s = jnp.where(pl.program_id(1) < scaled_tiles, scale, 1.0)
    o_ref[...] = (w_ref[...] * (g_ref[...] * s)).astype(o_ref.dtype)


def _prep_rows(w_t, col_gain, n_rows, scaled_rows, scale):
    layers, _, k = w_t.shape
    assert n_rows % PREP_TILE == 0 and scaled_rows % PREP_TILE == 0
    return pl.pallas_call(
        functools.partial(_prep_rows_kernel, scaled_rows // PREP_TILE, scale),
        grid=(layers, n_rows // PREP_TILE),
        in_specs=[
            pl.BlockSpec((None, PREP_TILE, k), lambda l, i: (l, i, 0)),
            pl.BlockSpec((None, 1, k), lambda l, i: (l, 0, 0)),
        ],
        out_specs=pl.BlockSpec((None, PREP_TILE, k), lambda l, i: (l, i, 0)),
        out_shape=jax.ShapeDtypeStruct((layers, n_rows, k), bf16),
        compiler_params=pltpu.CompilerParams(
            dimension_semantics=("arbitrary", "arbitrary"), vmem_limit_bytes=VMEM_LIMIT),
        name="prep_rows",
    )(w_t, col_gain.reshape(layers, 1, k))


def _pad_last(w, axis):
    pad = [(0, 0)] * w.ndim
    pad[axis] = (0, LANES - w.shape[axis])
    return jnp.pad(w, pad)


def kernel(x, meta_tokens, norm_g, gla_w_in, gla_w_gate_up, gla_b_gate, gla_head_norm_g, gla_w_out,
           kv_norm_g, fox_w_kv, fox_b_forget, fox_w_in, fox_w_out, final_norm_g):
    n_gla = gla_w_in.shape[0]
    n_fox = fox_w_in.shape[0]

    o_g = 2 * GLA_DK_TOTAL + D_INNER
    o_r = o_g + GLA_GATE_RANK
    gla_g = norm_g[:n_gla]
    gla_w_t = jnp.swapaxes(gla_w_in, 1, 2)
    gla_w_qkv = _prep_rows(gla_w_t, gla_g, o_g, GLA_DK_TOTAL, GLA_HEAD_K ** -0.5)
    gla_w_r = _prep_rows(gla_w_t[:, o_r:], gla_g, D_INNER, 0, 1.0)
    gla_w_low = _pad_last(gla_w_t[:, o_g:o_r] * gla_g[:, None, :], 1).astype(bf16)
    gla_w_up = _pad_last(gla_w_gate_up, 1).astype(bf16)
    gla_w_o, = _prep_weight(gla_w_out, jnp.tile(gla_head_norm_g, (1, GLA_HEADS)), [(D_MODEL, [(0, D_MODEL, 0, 1.0)])])
    kv_w_t = fox_w_kv.T[None]
    kv_w = _prep_rows(kv_w_t, kv_norm_g[None], 2 * D_INNER, 0, 1.0)
    kv_w_f = _pad_last(kv_w_t[:, 2 * D_INNER:] * kv_norm_g[None, None, :], 1).astype(bf16)
    fox_w_qr, = _prep_weight(fox_w_in, norm_g[n_gla:], [
        (2 * D_INNER, [(0, D_INNER, 0, FOX_HEAD_DIM ** -0.5 * LOG2E), (D_INNER, D_INNER, D_INNER, 1.0)])])
    fox_w_o, = _prep_weight(fox_w_out, jnp.ones(fox_w_out.shape[:2], f32), [(D_MODEL, [(0, D_MODEL, 0, 1.0)])])

    h, hb, rs = _embed(x.reshape(ROWS_X, D_MODEL), meta_tokens)
    for layer in range(n_gla):
        proj, g_low = _proj(hb, rs, gla_w_qkv, layer, o_g + D_INNER, w_tail=gla_w_r, w_small=gla_w_low,
                             transposed=True)
        o = _gla_mix(proj, g_low, gla_w_up, gla_b_gate.reshape(n_gla, 1, -1), layer)
        h, hb, rs = _gla_out_proj(o, proj, gla_w_o, layer, h)

    kv, f_logit = _proj(hb, rs, kv_w, 0, 2 * D_INNER, w_small=kv_w_f, transposed=True)
    bias = _forget_bias(f_logit, _pad_last(fox_b_forget.reshape(1, -1), 1))

    for j in range(n_fox):
        qr = _proj(hb, rs, fox_w_qr, j, 2 * D_INNER)
        y = _fox_mix(qr, kv, bias)
        if j + 1 < n_fox:
            h, hb, rs = _out_proj(y, fox_w_o, j, h)
        else:
            out = _out_proj(y, fox_w_o, j, h, final_g=final_norm_g)
    return out.reshape(BATCH, SEQ, D_MODEL)
```

```python
import functools

import jax
import jax.numpy as jnp
from jax import lax
from jax.experimental import pallas as pl
from jax.experimental.pallas import tpu as pltpu

f32 = jnp.float32
bf16 = jnp.bfloat16

D_MODEL = 1024
BATCH = 8
SEQ = 2048
N_META = 16
D_INNER = 2048
GLA_HEADS = 4
GLA_HEAD_K = 256
GLA_HEAD_V = 512
GLA_DK_TOTAL = GLA_HEADS * GLA_HEAD_K
GLA_GATE_RANK = 16
GLA_GATE_NORMALIZER = 16.0
FOX_HEADS = 16
FOX_HEAD_DIM = 128
RMS_EPS = 1e-6
NEG_INF = -1e30
LOG2E = 1.4426950408889634

LANES = 128
ROWS_X = BATCH * SEQ
ROWS = ROWS_X + LANES
META_BLOCK = BATCH
VMEM_LIMIT = 56 * 1024 * 1024

GLA_CHUNK = 128
GLA_BASE = 32
GLA_GROUP = 16
MXU_COLS = 256
FOX_TQ = 256
KEYS = 2304
FOX_STREAMS = 2
ROW_TILE = 688
PROJ_TILE = 1376
PROJ_COLS = 2048
OUT_TILE = 512
EMBED_TILE = 512
PREP_TILE = 512
PREP_ROWS_TILE = 1024


def _seq_block(t):
    return jnp.where(t == 0, META_BLOCK, t - 1)


def _log_sigmoid(x):
    return jnp.minimum(x, 0.0) - jnp.log1p(jnp.exp(-jnp.abs(x)))


def _silu(x):
    return x * (1.0 / (1.0 + jnp.exp(-x)))


def _split_dot(a, b, pieces, split_rhs):
    x = b if split_rhs else a
    acc = None
    for _ in range(pieces):
        p = x.astype(bf16)
        term = (jnp.dot(a, p, preferred_element_type=f32) if split_rhs
                else jnp.dot(p, b, preferred_element_type=f32))
        acc = term if acc is None else acc + term
        x = x - p.astype(f32)
    return acc


def _emit_stream(h, wn_ref, h_ref, hb_ref, rs_ref, narrow_ref=None):
    hb = h.astype(bf16)
    rs = lax.rsqrt(jnp.mean(h * h, axis=-1, keepdims=True) + RMS_EPS)
    h_ref[...] = h
    hb_ref[...] = hb
    rs_ref[...] = rs
    if narrow_ref is not None:
        narrow_ref[...] = _proj_dot(True, hb, wn_ref[...]) * rs


def _stream_specs(rows, d, tile, narrow):
    out_specs = [pl.BlockSpec((tile, d), lambda i: (i, 0)),
                 pl.BlockSpec((tile, d), lambda i: (i, 0)),
                 pl.BlockSpec((tile, 1), lambda i: (i, 0))]
    out_shapes = [jax.ShapeDtypeStruct((rows, d), f32),
                  jax.ShapeDtypeStruct((rows, d), bf16),
                  jax.ShapeDtypeStruct((rows, 1), f32)]
    if narrow is None:
        return [], [], out_specs, out_shapes
    w, layer = narrow
    out_specs.append(pl.BlockSpec((tile, LANES), lambda i: (i, 0)))
    out_shapes.append(jax.ShapeDtypeStruct((rows, LANES), f32))
    return [pl.BlockSpec((None, LANES, d), lambda i: (layer, 0, 0))], [w], out_specs, out_shapes


def _embed_kernel(x_ref, meta_ref, wn_ref, *out_refs):
    i = pl.program_id(0)

    @pl.when(i < ROWS_X // EMBED_TILE)
    def _():
        _emit_stream(x_ref[...], wn_ref, *out_refs)

    @pl.when(i == ROWS_X // EMBED_TILE)
    def _():
        pad = jnp.zeros((EMBED_TILE - N_META, D_MODEL), f32)
        _emit_stream(jnp.concatenate([meta_ref[...], pad], axis=0), wn_ref, *out_refs)


def _embed(x2d, meta_tokens, narrow):
    n_x = ROWS_X // EMBED_TILE
    w_specs, w_args, out_specs, out_shapes = _stream_specs(ROWS, D_MODEL, EMBED_TILE, narrow)
    return pl.pallas_call(
        _embed_kernel,
        grid=(n_x + 1,),
        in_specs=[
            pl.BlockSpec((EMBED_TILE, D_MODEL), lambda i: (jnp.minimum(i, n_x - 1), 0)),
            pl.BlockSpec((N_META, D_MODEL), lambda i: (0, 0)),
        ] + w_specs,
        out_specs=out_specs,
        out_shape=out_shapes,
        compiler_params=pltpu.CompilerParams(dimension_semantics=("arbitrary",), vmem_limit_bytes=VMEM_LIMIT),
        name="embed",
    )(x2d, meta_tokens, *w_args)


def _proj_dot(transposed, x, w):
    dims = (((1,), (1 if transposed else 0,)), ((), ()))
    return lax.dot_general(x, w, dims, preferred_element_type=f32)


def _proj_store(transposed, xb_ref, rs_ref, w_ref, o_ref):
    o_ref[...] = (_proj_dot(transposed, xb_ref[...], w_ref[...]) * rs_ref[...]).astype(o_ref.dtype)


def _proj_kernel(transposed, xb_ref, rs_ref, w_ref, o_ref):
    _proj_store(transposed, xb_ref, rs_ref, w_ref, o_ref)


def _proj_tail_kernel(transposed, xb_ref, rs_ref, w_ref, wt_ref, o_ref):
    last = pl.num_programs(1) - 1

    @pl.when(pl.program_id(1) < last)
    def _():
        _proj_store(transposed, xb_ref, rs_ref, w_ref, o_ref)

    @pl.when(pl.program_id(1) == last)
    def _():
        _proj_store(transposed, xb_ref, rs_ref, wt_ref, o_ref)


def _proj(hb, rs, w, layer, n, w_tail=None, transposed=False):
    rows, d = hb.shape
    tm, tn = PROJ_TILE, PROJ_COLS
    assert rows % tm == 0 and n % tn == 0
    n_w = n // tn - (w_tail is not None)

    def w_spec(col_block):
        if transposed:
            return pl.BlockSpec((None, tn, d), lambda i, j: (layer, col_block(j), 0))
        return pl.BlockSpec((None, d, tn), lambda i, j: (layer, 0, col_block(j)))

    in_specs = [
        pl.BlockSpec((tm, d), lambda i, j: (i, 0)),
        pl.BlockSpec((tm, 1), lambda i, j: (i, 0)),
        w_spec(lambda j: jnp.minimum(j, n_w - 1)),
    ]
    args = [hb, rs, w]
    if w_tail is not None:
        in_specs.append(w_spec(lambda j: 0))
        args.append(w_tail)
    return pl.pallas_call(
        functools.partial(_proj_kernel if w_tail is None else _proj_tail_kernel, transposed),
        grid=(rows // tm, n // tn),
        in_specs=in_specs,
        out_specs=pl.BlockSpec((tm, tn), lambda i, j: (i, j)),
        out_shape=jax.ShapeDtypeStruct((rows, n), bf16),
        compiler_params=pltpu.CompilerParams(
            dimension_semantics=("arbitrary", "arbitrary"), vmem_limit_bytes=VMEM_LIMIT),
        name="proj",
    )(*args)


def _out_proj_kernel(y_ref, w_ref, h_ref, *out_refs):
    _emit_stream(h_ref[...] + jnp.dot(y_ref[...], w_ref[...], preferred_element_type=f32), None, *out_refs)


def _out_proj_final_kernel(y_ref, w_ref, h_ref, g_ref, o_ref):
    x = h_ref[...] + jnp.dot(y_ref[...], w_ref[...], preferred_element_type=f32)
    var = jnp.mean(x * x, axis=-1, keepdims=True)
    o_ref[...] = x * lax.rsqrt(var + RMS_EPS) * g_ref[...]


def _out_proj(y, w, layer, h, final_g=None):
    _, k, d = w.shape
    if final_g is None:
        rows, tile, body, extra, extra_specs = h.shape[0], ROW_TILE, _out_proj_kernel, (), []
        _, _, out_specs, out_shape = _stream_specs(rows, d, tile, None)
    else:
        rows, tile, body = ROWS_X, OUT_TILE, _out_proj_final_kernel
        extra, extra_specs = (final_g.reshape(1, d),), [pl.BlockSpec((1, d), lambda i: (0, 0))]
        out_specs, out_shape = pl.BlockSpec((tile, d), lambda i: (i, 0)), jax.ShapeDtypeStruct((rows, d), f32)
    return pl.pallas_call(
        body,
        grid=(rows // tile,),
        in_specs=[
            pl.BlockSpec((tile, k), lambda i: (i, 0)),
            pl.BlockSpec((None, k, d), lambda i: (layer, 0, 0)),
            pl.BlockSpec((tile, d), lambda i: (i, 0)),
        ] + extra_specs,
        out_specs=out_specs,
        out_shape=out_shape,
        compiler_params=pltpu.CompilerParams(
            dimension_semantics=("arbitrary",), vmem_limit_bytes=VMEM_LIMIT),
        name="out_proj" if final_g is None else "out_proj_final",
    )(y, w, h, *extra)


def _gla_out_kernel(o_ref, r_ref, w_ref, h_ref, wn_ref, *out_refs):
    acc = h_ref[...]
    for hd in range(GLA_HEADS):
        cols = slice(hd * GLA_HEAD_V, (hd + 1) * GLA_HEAD_V)
        o = o_ref[:, cols].astype(f32)
        var = jnp.mean(o * o, axis=-1, keepdims=True)
        y = o * lax.rsqrt(var + RMS_EPS) * _silu(r_ref[:, cols].astype(f32))
        acc = acc + jnp.dot(y.astype(bf16), w_ref[cols, :], preferred_element_type=f32)
    _emit_stream(acc, wn_ref, *out_refs)


def _gla_out_proj(o, proj, w, layer, h, narrow):
    _, k, d = w.shape
    rows = h.shape[0]
    w_specs, w_args, out_specs, out_shape = _stream_specs(rows, d, ROW_TILE, narrow)
    return pl.pallas_call(
        _gla_out_kernel,
        grid=(rows // ROW_TILE,),
        in_specs=[
            pl.BlockSpec((ROW_TILE, k), lambda i: (i, 0)),
            pl.BlockSpec((ROW_TILE, k), lambda i: (i, proj.shape[1] // k - 1)),
            pl.BlockSpec((None, k, d), lambda i: (layer, 0, 0)),
            pl.BlockSpec((ROW_TILE, d), lambda i: (i, 0)),
        ] + w_specs,
        out_specs=out_specs,
        out_shape=out_shape,
        compiler_params=pltpu.CompilerParams(
            dimension_semantics=("arbitrary",), vmem_limit_bytes=VMEM_LIMIT),
        name="gla_out_proj",
    )(o, proj, w, h, *w_args)


def _gla_masks(c):
    i = lax.broadcasted_iota(jnp.int32, (c, c), 0)
    j = lax.broadcasted_iota(jnp.int32, (c, c), 1)
    base = min(c, GLA_BASE)
    uses = [((i & -base) == (j & -base)) & (j <= i)]
    size = 2 * base
    while size <= c:
        half = size // 2
        uses.append(((i & -size) == (j & -size)) & ((i & half) != 0) & ((j & half) == 0))
        size *= 2
    return (j <= i).astype(bf16), uses


def _gla_group(nc, c, masks, q, k, v, log2_a, state_t, emit):
    n, dk = q.shape
    nt = (((1,), (1,)), ((), ()))
    tn = (((0,), (0,)), ((), ()))
    tril, uses = masks
    row = lax.broadcasted_iota(jnp.int32, (n, 1), 0)

    b = jnp.concatenate([_split_dot(tril, log2_a[ci * c:(ci + 1) * c], 2, split_rhs=True)
                         for ci in range(nc)], axis=0)

    def minus_anchor(size):
        b3 = b.reshape(n // size, size, dk)
        return (b3 - b3[:, size // 2 - 1:size // 2, :]).reshape(n, dk)

    base = min(c, GLA_BASE)
    e0 = minus_anchor(base)
    operands = [((q * jnp.exp2(e0)).astype(bf16), (k * jnp.exp2(-e0)).astype(bf16))]
    size = 2 * base
    while size <= c:
        qk = (jnp.where((row & (size // 2)) != 0, q, k) * jnp.exp2(-jnp.abs(minus_anchor(size)))).astype(bf16)
        operands.append((qk, qk))
        size *= 2

    b3 = b.reshape(nc, c, dk)
    b_last = b3[:, c - 1:c, :]
    q_dec = (q * jnp.exp2(b)).astype(bf16)
    k_rem = (k.reshape(nc, c, dk) * jnp.exp2(b_last - b3)).reshape(n, dk).astype(bf16)
    decay = jnp.exp2(b_last)

    o_intra, update = [], []
    for ci in range(nc):
        rows = slice(ci * c, (ci + 1) * c)
        attn = None
        for (lhs, rhs), use in zip(operands, uses):
            p = lax.dot_general(lhs[rows], rhs[rows], nt, preferred_element_type=f32)
            attn = jnp.where(use, p, 0.0 if attn is None else attn)
        o_intra.append(jnp.dot(attn.astype(bf16), v[rows], preferred_element_type=f32))
        update.append(lax.dot_general(v[rows], k_rem[rows], tn, preferred_element_type=f32))

    for ci in range(nc):
        rows = slice(ci * c, (ci + 1) * c)
        emit(ci, o_intra[ci] + lax.dot_general(q_dec[rows], state_t.astype(bf16), nt,
                                               preferred_element_type=f32))
        state_t = state_t * decay[ci] + update[ci]
    return state_t


def _gla_kernel(q_ref, k_ref, v_ref, g_ref, wup_ref, bg_ref, o_ref, st_ref, st0_ref):
    t = pl.program_id(1)
    wup = wup_ref[...]
    bg = bg_ref[...]

    def run_group(row0, nc, c, masks, state_t):
        rows = pl.ds(row0, nc * c)
        logits = jnp.dot(g_ref[rows, :].astype(bf16), wup, preferred_element_type=f32) + bg
        log2_a = _log_sigmoid(logits) * (LOG2E / GLA_GATE_NORMALIZER)

        def emit(ci, o):
            o_ref[pl.ds(row0 + ci * c, c), :] = o.astype(o_ref.dtype)

        return _gla_group(nc, c, masks, q_ref[rows, :].astype(f32), k_ref[rows, :].astype(f32),
                          v_ref[rows, :], log2_a, state_t, emit)

    @pl.when(t == 0)
    def _():
        zero_state = jnp.zeros((GLA_HEAD_V, GLA_HEAD_K), f32)
        st0_ref[...] = run_group(0, 1, N_META, _gla_masks(N_META), zero_state)
        o_ref[N_META:LANES, :] = jnp.zeros((LANES - N_META, GLA_HEAD_V), o_ref.dtype)

    @pl.when(t > 0)
    def _():
        st_ref[...] = st0_ref[...]
        group_rows = GLA_GROUP * GLA_CHUNK
        masks = _gla_masks(GLA_CHUNK)

        def body(gi, carry):
            row0 = pl.multiple_of(gi * group_rows, group_rows)
            st_ref[...] = run_group(row0, GLA_GROUP, GLA_CHUNK, masks, st_ref[...])
            return carry

        lax.fori_loop(0, SEQ // group_rows, body, 0)


def _gla_mix(proj, g_low, w_up, b_gate, layer):
    kq = GLA_DK_TOTAL // GLA_HEAD_K
    kv = 2 * GLA_DK_TOTAL // GLA_HEAD_V
    return pl.pallas_call(
        _gla_kernel,
        grid=(GLA_HEADS, BATCH + 1),
        in_specs=[
            pl.BlockSpec((SEQ, GLA_HEAD_K), lambda h, t: (_seq_block(t), h)),
            pl.BlockSpec((SEQ, GLA_HEAD_K), lambda h, t: (_seq_block(t), kq + h)),
            pl.BlockSpec((SEQ, GLA_HEAD_V), lambda h, t: (_seq_block(t), kv + h)),
            pl.BlockSpec((SEQ, LANES), lambda h, t: (_seq_block(t), 0)),
            pl.BlockSpec((None, LANES, GLA_HEAD_K), lambda h, t: (layer, 0, h)),
            pl.BlockSpec((None, 1, GLA_HEAD_K), lambda h, t: (layer, 0, h)),
        ],
        out_specs=pl.BlockSpec((SEQ, GLA_HEAD_V), lambda h, t: (_seq_block(t), h)),
        out_shape=jax.ShapeDtypeStruct((ROWS, D_INNER), bf16),
        scratch_shapes=[pltpu.VMEM((GLA_HEAD_V, GLA_HEAD_K), f32),
                        pltpu.VMEM((GLA_HEAD_V, GLA_HEAD_K), f32)],
        compiler_params=pltpu.CompilerParams(
            dimension_semantics=("arbitrary", "arbitrary"), vmem_limit_bytes=VMEM_LIMIT),
        name="gla_mix",
    )(proj, proj, proj, g_low, w_up, b_gate)


def _bias_kernel(f_ref, b_ref, o_ref, meta_ref):
    t = pl.program_id(0)
    i = lax.broadcasted_iota(jnp.int32, (LANES, LANES), 0)
    j = lax.broadcasted_iota(jnp.int32, (LANES, LANES), 1)
    upper = (i <= j).astype(bf16)

    def block_cumsum(c, carry):
        log_f = _log_sigmoid(f_ref[c * LANES:(c + 1) * LANES, :] + b_ref[...])
        return _split_dot(log_f.T, upper, 3, split_rhs=False) + carry

    @pl.when(t == 0)
    def _():
        meta_ref[...] = block_cumsum(0, jnp.zeros((LANES, 1), f32))

    @pl.when(t > 0)
    def _():
        meta = meta_ref[...]
        o_ref[:, 0, 0, 0:N_META] = meta[0:FOX_HEADS, 0:N_META] * -LOG2E
        carry = meta[:, N_META - 1:N_META]
        for c in range(SEQ // LANES):
            cs = block_cumsum(c, carry)
            o_ref[:, 0, 0, N_META + c * LANES:N_META + (c + 1) * LANES] = cs[0:FOX_HEADS, :] * -LOG2E
            carry = cs[:, LANES - 1:LANES]
        o_ref[:, 0, 0, N_META + SEQ:KEYS] = jnp.zeros((FOX_HEADS, KEYS - N_META - SEQ), f32)


def _forget_bias(f_logit, b_forget):
    return pl.pallas_call(
        _bias_kernel,
        grid=(BATCH + 1,),
        in_specs=[
            pl.BlockSpec((SEQ, LANES), lambda t: (_seq_block(t), 0)),
            pl.BlockSpec((1, LANES), lambda t: (0, 0)),
        ],
        out_specs=pl.BlockSpec((FOX_HEADS, 1, 1, KEYS), lambda t: (0, jnp.maximum(t - 1, 0), 0, 0)),
        out_shape=jax.ShapeDtypeStruct((FOX_HEADS, BATCH, 1, KEYS), f32),
        scratch_shapes=[pltpu.VMEM((LANES, LANES), f32)],
        compiler_params=pltpu.CompilerParams(dimension_semantics=("arbitrary",)),
        name="forget_bias",
    )(f_logit, b_forget)


def _scores(q, k, bias, mask):
    s = lax.dot_general(q, k, (((1,), (1,)), ((), ())), preferred_element_type=f32) + bias
    return s if mask is None else jnp.where(mask, s, NEG_INF)


def _softmax_pv(scores, values):
    m = functools.reduce(jnp.maximum, [jnp.max(s, axis=-1, keepdims=True) for s in scores])
    acc = None
    for s, v in zip(scores, values):
        pv = jnp.dot(jnp.exp2(s - m).astype(bf16), v, preferred_element_type=f32)
        acc = pv if acc is None else acc + pv
    return acc[:, 0:FOX_HEAD_DIM] * (1.0 / acc[:, FOX_HEAD_DIM:])


def _fox_kernel(q_ref, r_ref, k_ref, v_ref, b_ref, y_ref, kc_ref, vc_ref):
    t = pl.program_id(1)
    dh = FOX_HEAD_DIM

    def lower_tri(nq, nk, shift):
        i = lax.broadcasted_iota(jnp.int32, (nq, nk), 0)
        j = lax.broadcasted_iota(jnp.int32, (nq, nk), 1)
        return j <= i + shift

    def finish(rows, o):
        y_ref[rows, :] = (o * _silu(r_ref[rows, :].astype(f32))).astype(y_ref.dtype)

    @pl.when(t == 0)
    def _():
        for s in range(FOX_STREAMS):
            vc_ref[s, :, dh:] = jnp.ones((KEYS, dh), bf16)
            kc_ref[s, 0:N_META, :] = k_ref[0:N_META, :]
            vc_ref[s, 0:N_META, 0:dh] = v_ref[0:N_META, :]
        sc = _scores(q_ref[0:N_META, :], kc_ref[0, 0:N_META, :], b_ref[0, 0, :, 0:N_META],
                     lower_tri(N_META, N_META, 0))
        finish(pl.ds(0, N_META), _softmax_pv([sc], [vc_ref[0, 0:N_META, :]]))
        y_ref[N_META:LANES, :] = jnp.zeros((LANES - N_META, dh), y_ref.dtype)

    @pl.when(t > 0)
    def _():
        n_keys = N_META + SEQ
        for s in range(FOX_STREAMS):
            kc_ref[s, N_META:n_keys, :] = k_ref[s * SEQ:(s + 1) * SEQ, :]
            vc_ref[s, N_META:n_keys, 0:dh] = v_ref[s * SEQ:(s + 1) * SEQ, :]

        def tile_scores(s, ti):
            p0 = ti * FOX_TQ
            k1 = min(p0 + FOX_TQ, n_keys)
            q0 = max(p0 - N_META, 0)
            q1 = k1 - N_META
            q = q_ref[s * SEQ + q0:s * SEQ + q1, :]
            k0 = p0 // MXU_COLS * MXU_COLS
            scores, values = [], []
            if k0 > 0:
                scores.append(_scores(q, kc_ref[s, 0:k0, :], b_ref[0, s, :, 0:k0], None))
                values.append(vc_ref[s, 0:k0, :])
            shift = q0 + N_META - k0
            scores.append(_scores(q, kc_ref[s, k0:k1, :], b_ref[0, s, :, k0:k1],
                                  lower_tri(q1 - q0, k1 - k0, shift)))
            values.append(vc_ref[s, k0:k1, :])
            return pl.ds(s * SEQ + q0, q1 - q0), scores, values

        n_tiles = -(-n_keys // FOX_TQ)
        order = [(s, ti if s % 2 == 0 else n_tiles - 1 - ti) for ti in range(n_tiles) for s in range(FOX_STREAMS)]
        pending = tile_scores(*order[0])
        for pos in range(len(order)):
            rows, scores, values = pending
            if pos + 1 < len(order):
                pending = tile_scores(*order[pos + 1])
            finish(rows, _softmax_pv(scores, values))


def _fox_mix(qr, kv, bias):
    nh = FOX_HEADS
    rows = FOX_STREAMS * SEQ
    n_steps = BATCH // FOX_STREAMS

    def row_block(t):
        return jnp.where(t == 0, n_steps, t - 1)

    return pl.pallas_call(
        _fox_kernel,
        grid=(FOX_HEADS, n_steps + 1),
        in_specs=[
            pl.BlockSpec((rows, FOX_HEAD_DIM), lambda h, t: (row_block(t), h)),
            pl.BlockSpec((rows, FOX_HEAD_DIM), lambda h, t: (row_block(t), nh + h)),
            pl.BlockSpec((rows, FOX_HEAD_DIM), lambda h, t: (row_block(t), h)),
            pl.BlockSpec((rows, FOX_HEAD_DIM), lambda h, t: (row_block(t), nh + h)),
            pl.BlockSpec((1, FOX_STREAMS, 1, KEYS), lambda h, t: (h, jnp.maximum(t - 1, 0), 0, 0)),
        ],
        out_specs=pl.BlockSpec((rows, FOX_HEAD_DIM), lambda h, t: (row_block(t), h)),
        out_shape=jax.ShapeDtypeStruct((ROWS, D_INNER), bf16),
        scratch_shapes=[pltpu.VMEM((FOX_STREAMS, KEYS, FOX_HEAD_DIM), bf16),
                        pltpu.VMEM((FOX_STREAMS, KEYS, 2 * FOX_HEAD_DIM), bf16)],
        compiler_params=pltpu.CompilerParams(
            dimension_semantics=("arbitrary", "arbitrary"), vmem_limit_bytes=VMEM_LIMIT),
        name="fox_mix",
    )(qr, qr, kv, kv, bias)


def _prep_kernel(outputs, w_ref, g_ref, *out_refs):
    g = g_ref[...]
    for (width, pieces), o_ref in zip(outputs, out_refs):
        if sum(p[1] for p in pieces) < width:
            o_ref[...] = jnp.zeros(o_ref.shape, o_ref.dtype)
        for src, cols, dst, scale in pieces:
            o_ref[:, dst:dst + cols] = (w_ref[:, src:src + cols] * (g * scale)).astype(o_ref.dtype)


def _prep_weight(w, row_gain, outputs):
    layers, k, n = w.shape
    return pl.pallas_call(
        functools.partial(_prep_kernel, outputs),
        grid=(layers, k // PREP_TILE),
        in_specs=[
            pl.BlockSpec((None, PREP_TILE, n), lambda l, i: (l, i, 0)),
            pl.BlockSpec((None, PREP_TILE, 1), lambda l, i: (l, i, 0)),
        ],
        out_specs=[pl.BlockSpec((None, PREP_TILE, width), lambda l, i: (l, i, 0)) for width, _ in outputs],
        out_shape=[jax.ShapeDtypeStruct((layers, k, width), bf16) for width, _ in outputs],
        compiler_params=pltpu.CompilerParams(
            dimension_semantics=("arbitrary", "arbitrary"), vmem_limit_bytes=VMEM_LIMIT),
        name="prep_weight",
    )(w, row_gain.reshape(layers, k, 1))


def _prep_rows_kernel(scaled_tiles, scale, w_ref, g_ref, o_ref):
    s = jnp.where(pl.program_id(1) < scaled_tiles, scale, 1.0)
    o_ref[...] = (w_ref[...] * (g_ref[...] * s)).astype(o_ref.dtype)


def _prep_rows(w_t, col_gain, n_rows, scaled_rows, scale):
    layers, _, k = w_t.shape
    tile = PREP_ROWS_TILE
    assert n_rows % tile == 0 and scaled_rows % tile == 0
    return pl.pallas_call(
        functools.partial(_prep_rows_kernel, scaled_rows // tile, scale),
        grid=(layers, n_rows // tile),
        in_specs=[
            pl.BlockSpec((None, tile, k), lambda l, i: (l, i, 0)),
            pl.BlockSpec((None, 1, k), lambda l, i: (l, 0, 0)),
        ],
        out_specs=pl.BlockSpec((None, tile, k), lambda l, i: (l, i, 0)),
        out_shape=jax.ShapeDtypeStruct((layers, n_rows, k), bf16),
        compiler_params=pltpu.CompilerParams(
            dimension_semantics=("arbitrary", "arbitrary"), vmem_limit_bytes=VMEM_LIMIT),
        name="prep_rows",
    )(w_t, col_gain.reshape(layers, 1, k))


def _pad_last(w, axis):
    pad = [(0, 0)] * w.ndim
    pad[axis] = (0, LANES - w.shape[axis])
    return jnp.pad(w, pad)


def kernel(x, meta_tokens, norm_g, gla_w_in, gla_w_gate_up, gla_b_gate, gla_head_norm_g, gla_w_out,
           kv_norm_g, fox_w_kv, fox_b_forget, fox_w_in, fox_w_out, final_norm_g):
    n_gla = gla_w_in.shape[0]
    n_fox = fox_w_in.shape[0]

    o_g = 2 * GLA_DK_TOTAL + D_INNER
    o_r = o_g + GLA_GATE_RANK
    gla_g = norm_g[:n_gla]
    gla_w_t = jnp.swapaxes(gla_w_in, 1, 2)
    gla_w_qkv = _prep_rows(gla_w_t, gla_g, o_g, GLA_DK_TOTAL, GLA_HEAD_K ** -0.5)
    gla_w_r = _prep_rows(gla_w_t[:, o_r:], gla_g, D_INNER, 0, 1.0)
    gla_w_low = _pad_last(gla_w_t[:, o_g:o_r] * gla_g[:, None, :], 1).astype(bf16)
    gla_w_up = _pad_last(gla_w_gate_up, 1).astype(bf16)
    gla_w_o, = _prep_weight(gla_w_out, jnp.tile(gla_head_norm_g, (1, GLA_HEADS)), [(D_MODEL, [(0, D_MODEL, 0, 1.0)])])
    kv_w_t = fox_w_kv.T[None]
    kv_w = _prep_rows(kv_w_t, kv_norm_g[None], 2 * D_INNER, 0, 1.0)
    kv_w_f = _pad_last(kv_w_t[:, 2 * D_INNER:] * kv_norm_g[None, None, :], 1).astype(bf16)
    fox_w_qr, = _prep_weight(fox_w_in, norm_g[n_gla:], [
        (2 * D_INNER, [(0, D_INNER, 0, FOX_HEAD_DIM ** -0.5 * LOG2E), (D_INNER, D_INNER, D_INNER, 1.0)])])
    fox_w_o, = _prep_weight(fox_w_out, jnp.ones(fox_w_out.shape[:2], f32), [(D_MODEL, [(0, D_MODEL, 0, 1.0)])])

    h, hb, rs, narrow = _embed(x.reshape(ROWS_X, D_MODEL), meta_tokens, (gla_w_low, 0))
    for layer in range(n_gla):
        proj = _proj(hb, rs, gla_w_qkv, layer, o_g + D_INNER, w_tail=gla_w_r, transposed=True)
        o = _gla_mix(proj, narrow, gla_w_up, gla_b_gate.reshape(n_gla, 1, -1), layer)
        next_narrow = (gla_w_low, layer + 1) if layer + 1 < n_gla else (kv_w_f, 0)
        h, hb, rs, narrow = _gla_out_proj(o, proj, gla_w_o, layer, h, next_narrow)

    kv = _proj(hb, rs, kv_w, 0, 2 * D_INNER, transposed=True)
    bias = _forget_bias(narrow, _pad_last(fox_b_forget.reshape(1, -1), 1))

    for j in range(n_fox):
        qr = _proj(hb, rs, fox_w_qr, j, 2 * D_INNER)
        y = _fox_mix(qr, kv, bias)
        if j + 1 < n_fox:
            h, hb, rs = _out_proj(y, fox_w_o, j, h)
        else:
            out = _out_proj(y, fox_w_o, j, h, final_g=final_norm_g)
    return out.reshape(BATCH, SEQ, D_MODEL)
```

```python
import functools

import jax
import jax.numpy as jnp
from jax import lax
from jax.experimental import pallas as pl
from jax.experimental.pallas import tpu as pltpu

f32 = jnp.float32
bf16 = jnp.bfloat16

D_MODEL = 1024
BATCH = 8
SEQ = 2048
N_META = 16
D_INNER = 2048
GLA_HEADS = 4
GLA_HEAD_K = 256
GLA_HEAD_V = 512
GLA_DK_TOTAL = GLA_HEADS * GLA_HEAD_K
GLA_GATE_RANK = 16
GLA_GATE_NORMALIZER = 16.0
FOX_HEADS = 16
FOX_HEAD_DIM = 128
RMS_EPS = 1e-6
NEG_INF = -1e30
LOG2E = 1.4426950408889634

LANES = 128
ROWS_X = BATCH * SEQ
ROWS = ROWS_X + LANES
META_BLOCK = BATCH
VMEM_LIMIT = 56 * 1024 * 1024

GLA_CHUNK = 128
GLA_BASE = 32
GLA_GROUP = 16
MXU_COLS = 256
FOX_TQ = 256
KEYS = 2304
FOX_STREAMS = 1
ROW_TILE = 688
PROJ_TILE = 1376
PROJ_COLS = 2048
OUT_TILE = 512
EMBED_TILE = 512
PREP_TILE = 512
PREP_ROWS_TILE = 1024


def _seq_block(t):
    return jnp.where(t == 0, META_BLOCK, t - 1)


def _log_sigmoid(x):
    return jnp.minimum(x, 0.0) - jnp.log1p(jnp.exp(-jnp.abs(x)))


def _silu(x):
    half = 0.5 * x
    return half + half * jnp.tanh(half)


def _split_dot(a, b, pieces, split_rhs):
    x = b if split_rhs else a
    acc = None
    for _ in range(pieces):
        p = x.astype(bf16)
        term = (jnp.dot(a, p, preferred_element_type=f32) if split_rhs
                else jnp.dot(p, b, preferred_element_type=f32))
        acc = term if acc is None else acc + term
        x = x - p.astype(f32)
    return acc


def _emit_stream(h, wn_ref, h_ref, hb_ref, rs_ref, narrow_ref=None):
    hb = h.astype(bf16)
    rs = lax.rsqrt(jnp.mean(h * h, axis=-1, keepdims=True) + RMS_EPS)
    h_ref[...] = h
    hb_ref[...] = hb
    rs_ref[...] = rs
    if narrow_ref is not None:
        narrow_ref[...] = _proj_dot(True, hb, wn_ref[...]) * rs


def _stream_specs(rows, d, tile, narrow):
    out_specs = [pl.BlockSpec((tile, d), lambda i: (i, 0)),
                 pl.BlockSpec((tile, d), lambda i: (i, 0)),
                 pl.BlockSpec((tile, 1), lambda i: (i, 0))]
    out_shapes = [jax.ShapeDtypeStruct((rows, d), f32),
                  jax.ShapeDtypeStruct((rows, d), bf16),
                  jax.ShapeDtypeStruct((rows, 1), f32)]
    if narrow is None:
        return [], [], out_specs, out_shapes
    w, layer = narrow
    out_specs.append(pl.BlockSpec((tile, LANES), lambda i: (i, 0)))
    out_shapes.append(jax.ShapeDtypeStruct((rows, LANES), f32))
    return [pl.BlockSpec((None, LANES, d), lambda i: (layer, 0, 0))], [w], out_specs, out_shapes


def _embed_kernel(x_ref, meta_ref, wn_ref, *out_refs):
    i = pl.program_id(0)

    @pl.when(i < ROWS_X // EMBED_TILE)
    def _():
        _emit_stream(x_ref[...], wn_ref, *out_refs)

    @pl.when(i == ROWS_X // EMBED_TILE)
    def _():
        pad = jnp.zeros((EMBED_TILE - N_META, D_MODEL), f32)
        _emit_stream(jnp.concatenate([meta_ref[...], pad], axis=0), wn_ref, *out_refs)


def _embed(x2d, meta_tokens, narrow):
    n_x = ROWS_X // EMBED_TILE
    w_specs, w_args, out_specs, out_shapes = _stream_specs(ROWS, D_MODEL, EMBED_TILE, narrow)
    return pl.pallas_call(
        _embed_kernel,
        grid=(n_x + 1,),
        in_specs=[
            pl.BlockSpec((EMBED_TILE, D_MODEL), lambda i: (jnp.minimum(i, n_x - 1), 0)),
            pl.BlockSpec((N_META, D_MODEL), lambda i: (0, 0)),
        ] + w_specs,
        out_specs=out_specs,
        out_shape=out_shapes,
        compiler_params=pltpu.CompilerParams(dimension_semantics=("arbitrary",), vmem_limit_bytes=VMEM_LIMIT),
        name="embed",
    )(x2d, meta_tokens, *w_args)


def _proj_dot(transposed, x, w):
    dims = (((1,), (1 if transposed else 0,)), ((), ()))
    return lax.dot_general(x, w, dims, preferred_element_type=f32)


def _proj_store(transposed, gated, xb_ref, rs_ref, w_ref, o_ref):
    acc = _proj_dot(transposed, xb_ref[...], w_ref[...]) * rs_ref[...]
    o_ref[...] = (_silu(acc) if gated else acc).astype(o_ref.dtype)


def _proj_kernel(transposed, xb_ref, rs_ref, w_ref, o_ref):
    _proj_store(transposed, False, xb_ref, rs_ref, w_ref, o_ref)


def _proj_gated_kernel(transposed, has_tail, xb_ref, rs_ref, w_ref, *rest):
    wt_ref, o_ref = rest if has_tail else (w_ref,) + rest
    last = pl.num_programs(1) - 1

    @pl.when(pl.program_id(1) < last)
    def _():
        _proj_store(transposed, False, xb_ref, rs_ref, w_ref, o_ref)

    @pl.when(pl.program_id(1) == last)
    def _():
        _proj_store(transposed, True, xb_ref, rs_ref, wt_ref, o_ref)


def _proj(hb, rs, w, layer, n, w_tail=None, transposed=False, gated=False):
    rows, d = hb.shape
    tm, tn = PROJ_TILE, PROJ_COLS
    assert rows % tm == 0 and n % tn == 0 and (gated or w_tail is None)
    n_w = n // tn - (w_tail is not None)

    def w_spec(col_block):
        if transposed:
            return pl.BlockSpec((None, tn, d), lambda i, j: (layer, col_block(j), 0))
        return pl.BlockSpec((None, d, tn), lambda i, j: (layer, 0, col_block(j)))

    in_specs = [
        pl.BlockSpec((tm, d), lambda i, j: (i, 0)),
        pl.BlockSpec((tm, 1), lambda i, j: (i, 0)),
        w_spec(lambda j: jnp.minimum(j, n_w - 1)),
    ]
    args = [hb, rs, w]
    if w_tail is not None:
        in_specs.append(w_spec(lambda j: 0))
        args.append(w_tail)
    body = (functools.partial(_proj_gated_kernel, transposed, w_tail is not None) if gated
            else functools.partial(_proj_kernel, transposed))
    return pl.pallas_call(
        body,
        grid=(rows // tm, n // tn),
        in_specs=in_specs,
        out_specs=pl.BlockSpec((tm, tn), lambda i, j: (i, j)),
        out_shape=jax.ShapeDtypeStruct((rows, n), bf16),
        compiler_params=pltpu.CompilerParams(
            dimension_semantics=("arbitrary", "arbitrary"), vmem_limit_bytes=VMEM_LIMIT),
        name="proj",
    )(*args)


def _out_proj_kernel(y_ref, w_ref, h_ref, *out_refs):
    _emit_stream(h_ref[...] + jnp.dot(y_ref[...], w_ref[...], preferred_element_type=f32), None, *out_refs)


def _out_proj_final_kernel(y_ref, w_ref, h_ref, g_ref, o_ref):
    x = h_ref[...] + jnp.dot(y_ref[...], w_ref[...], preferred_element_type=f32)
    var = jnp.mean(x * x, axis=-1, keepdims=True)
    o_ref[...] = x * lax.rsqrt(var + RMS_EPS) * g_ref[...]


def _out_proj(y, w, layer, h, final_g=None):
    _, k, d = w.shape
    if final_g is None:
        rows, tile, body, extra, extra_specs = h.shape[0], ROW_TILE, _out_proj_kernel, (), []
        _, _, out_specs, out_shape = _stream_specs(rows, d, tile, None)
    else:
        rows, tile, body = ROWS_X, OUT_TILE, _out_proj_final_kernel
        extra, extra_specs = (final_g.reshape(1, d),), [pl.BlockSpec((1, d), lambda i: (0, 0))]
        out_specs, out_shape = pl.BlockSpec((tile, d), lambda i: (i, 0)), jax.ShapeDtypeStruct((rows, d), f32)
    return pl.pallas_call(
        body,
        grid=(rows // tile,),
        in_specs=[
            pl.BlockSpec((tile, k), lambda i: (i, 0)),
            pl.BlockSpec((None, k, d), lambda i: (layer, 0, 0)),
            pl.BlockSpec((tile, d), lambda i: (i, 0)),
        ] + extra_specs,
        out_specs=out_specs,
        out_shape=out_shape,
        compiler_params=pltpu.CompilerParams(
            dimension_semantics=("arbitrary",), vmem_limit_bytes=VMEM_LIMIT),
        name="out_proj" if final_g is None else "out_proj_final",
    )(y, w, h, *extra)


def _gla_out_kernel(o_ref, gate_ref, w_ref, h_ref, wn_ref, *out_refs):
    acc = h_ref[...]
    for hd in range(GLA_HEADS):
        cols = slice(hd * GLA_HEAD_V, (hd + 1) * GLA_HEAD_V)
        o = o_ref[:, cols].astype(f32)
        var = jnp.mean(o * o, axis=-1, keepdims=True)
        y = o * lax.rsqrt(var + RMS_EPS) * gate_ref[:, cols].astype(f32)
        acc = acc + jnp.dot(y.astype(bf16), w_ref[cols, :], preferred_element_type=f32)
    _emit_stream(acc, wn_ref, *out_refs)


def _gla_out_proj(o, proj, w, layer, h, narrow):
    _, k, d = w.shape
    rows = h.shape[0]
    w_specs, w_args, out_specs, out_shape = _stream_specs(rows, d, ROW_TILE, narrow)
    return pl.pallas_call(
        _gla_out_kernel,
        grid=(rows // ROW_TILE,),
        in_specs=[
            pl.BlockSpec((ROW_TILE, k), lambda i: (i, 0)),
            pl.BlockSpec((ROW_TILE, k), lambda i: (i, proj.shape[1] // k - 1)),
            pl.BlockSpec((None, k, d), lambda i: (layer, 0, 0)),
            pl.BlockSpec((ROW_TILE, d), lambda i: (i, 0)),
        ] + w_specs,
        out_specs=out_specs,
        out_shape=out_shape,
        compiler_params=pltpu.CompilerParams(
            dimension_semantics=("arbitrary",), vmem_limit_bytes=VMEM_LIMIT),
        name="gla_out_proj",
    )(o, proj, w, h, *w_args)


def _gla_masks(c):
    i = lax.broadcasted_iota(jnp.int32, (c, c), 0)
    j = lax.broadcasted_iota(jnp.int32, (c, c), 1)
    base = min(c, GLA_BASE)
    uses = [((i & -base) == (j & -base)) & (j <= i)]
    size = 2 * base
    while size <= c:
        half = size // 2
        uses.append(((i & -size) == (j & -size)) & ((i & half) != 0) & ((j & half) == 0))
        size *= 2
    return (j <= i).astype(bf16), uses


def _gla_group(nc, c, masks, q, k, v, log2_a, state_t, emit):
    n, dk = q.shape
    nt = (((1,), (1,)), ((), ()))
    tn = (((0,), (0,)), ((), ()))
    tril, uses = masks
    row = lax.broadcasted_iota(jnp.int32, (n, 1), 0)

    b = jnp.concatenate([_split_dot(tril, log2_a[ci * c:(ci + 1) * c], 2, split_rhs=True)
                         for ci in range(nc)], axis=0)

    def minus_anchor(size):
        b3 = b.reshape(n // size, size, dk)
        return (b3 - b3[:, size // 2 - 1:size // 2, :]).reshape(n, dk)

    base = min(c, GLA_BASE)
    e0 = minus_anchor(base)
    operands = [((q * jnp.exp2(e0)).astype(bf16), (k * jnp.exp2(-e0)).astype(bf16))]
    size = 2 * base
    while size <= c:
        qk = (jnp.where((row & (size // 2)) != 0, q, k) * jnp.exp2(-jnp.abs(minus_anchor(size)))).astype(bf16)
        operands.append((qk, qk))
        size *= 2

    b3 = b.reshape(nc, c, dk)
    b_last = b3[:, c - 1:c, :]
    q_dec = (q * jnp.exp2(b)).astype(bf16)
    k_rem = (k.reshape(nc, c, dk) * jnp.exp2(b_last - b3)).reshape(n, dk).astype(bf16)
    decay = jnp.exp2(b_last)

    o_intra, update = [], []
    for ci in range(nc):
        rows = slice(ci * c, (ci + 1) * c)
        attn = None
        for (lhs, rhs), use in zip(operands, uses):
            p = lax.dot_general(lhs[rows], rhs[rows], nt, preferred_element_type=f32)
            attn = jnp.where(use, p, 0.0 if attn is None else attn)
        o_intra.append(jnp.dot(attn.astype(bf16), v[rows], preferred_element_type=f32))
        update.append(lax.dot_general(v[rows], k_rem[rows], tn, preferred_element_type=f32))

    for ci in range(nc):
        rows = slice(ci * c, (ci + 1) * c)
        emit(ci, o_intra[ci] + lax.dot_general(q_dec[rows], state_t.astype(bf16), nt,
                                               preferred_element_type=f32))
        state_t = state_t * decay[ci] + update[ci]
    return state_t


def _gla_kernel(q_ref, k_ref, v_ref, g_ref, wup_ref, bg_ref, o_ref, st_ref, st0_ref):
    t = pl.program_id(1)
    wup = wup_ref[...]
    bg = bg_ref[...]

    def run_group(row0, nc, c, masks, state_t):
        rows = pl.ds(row0, nc * c)
        logits = jnp.dot(g_ref[rows, :].astype(bf16), wup, preferred_element_type=f32) + bg
        log2_a = _log_sigmoid(logits) * (LOG2E / GLA_GATE_NORMALIZER)

        def emit(ci, o):
            o_ref[pl.ds(row0 + ci * c, c), :] = o.astype(o_ref.dtype)

        return _gla_group(nc, c, masks, q_ref[rows, :].astype(f32), k_ref[rows, :].astype(f32),
                          v_ref[rows, :], log2_a, state_t, emit)

    @pl.when(t == 0)
    def _():
        zero_state = jnp.zeros((GLA_HEAD_V, GLA_HEAD_K), f32)
        st0_ref[...] = run_group(0, 1, N_META, _gla_masks(N_META), zero_state)
        o_ref[N_META:LANES, :] = jnp.zeros((LANES - N_META, GLA_HEAD_V), o_ref.dtype)

    @pl.when(t > 0)
    def _():
        st_ref[...] = st0_ref[...]
        group_rows = GLA_GROUP * GLA_CHUNK
        masks = _gla_masks(GLA_CHUNK)

        def body(gi, carry):
            row0 = pl.multiple_of(gi * group_rows, group_rows)
            st_ref[...] = run_group(row0, GLA_GROUP, GLA_CHUNK, masks, st_ref[...])
            return carry

        lax.fori_loop(0, SEQ // group_rows, body, 0)


def _gla_mix(proj, g_low, w_up, b_gate, layer):
    kq = GLA_DK_TOTAL // GLA_HEAD_K
    kv = 2 * GLA_DK_TOTAL // GLA_HEAD_V
    return pl.pallas_call(
        _gla_kernel,
        grid=(GLA_HEADS, BATCH + 1),
        in_specs=[
            pl.BlockSpec((SEQ, GLA_HEAD_K), lambda h, t: (_seq_block(t), h)),
            pl.BlockSpec((SEQ, GLA_HEAD_K), lambda h, t: (_seq_block(t), kq + h)),
            pl.BlockSpec((SEQ, GLA_HEAD_V), lambda h, t: (_seq_block(t), kv + h)),
            pl.BlockSpec((SEQ, LANES), lambda h, t: (_seq_block(t), 0)),
            pl.BlockSpec((None, LANES, GLA_HEAD_K), lambda h, t: (layer, 0, h)),
            pl.BlockSpec((None, 1, GLA_HEAD_K), lambda h, t: (layer, 0, h)),
        ],
        out_specs=pl.BlockSpec((SEQ, GLA_HEAD_V), lambda h, t: (_seq_block(t), h)),
        out_shape=jax.ShapeDtypeStruct((ROWS, D_INNER), bf16),
        scratch_shapes=[pltpu.VMEM((GLA_HEAD_V, GLA_HEAD_K), f32),
                        pltpu.VMEM((GLA_HEAD_V, GLA_HEAD_K), f32)],
        compiler_params=pltpu.CompilerParams(
            dimension_semantics=("arbitrary", "arbitrary"), vmem_limit_bytes=VMEM_LIMIT),
        name="gla_mix",
    )(proj, proj, proj, g_low, w_up, b_gate)


def _bias_kernel(f_ref, b_ref, o_ref, meta_ref):
    t = pl.program_id(0)
    i = lax.broadcasted_iota(jnp.int32, (LANES, LANES), 0)
    j = lax.broadcasted_iota(jnp.int32, (LANES, LANES), 1)
    upper = (i <= j).astype(bf16)

    def block_cumsum(c, carry):
        log_f = _log_sigmoid(f_ref[c * LANES:(c + 1) * LANES, :] + b_ref[...])
        return _split_dot(log_f.T, upper, 3, split_rhs=False) + carry

    @pl.when(t == 0)
    def _():
        meta_ref[...] = block_cumsum(0, jnp.zeros((LANES, 1), f32))

    @pl.when(t > 0)
    def _():
        meta = meta_ref[...]
        o_ref[:, 0, 0, 0:N_META] = meta[0:FOX_HEADS, 0:N_META] * -LOG2E
        carry = meta[:, N_META - 1:N_META]
        for c in range(SEQ // LANES):
            cs = block_cumsum(c, carry)
            o_ref[:, 0, 0, N_META + c * LANES:N_META + (c + 1) * LANES] = cs[0:FOX_HEADS, :] * -LOG2E
            carry = cs[:, LANES - 1:LANES]
        o_ref[:, 0, 0, N_META + SEQ:KEYS] = jnp.zeros((FOX_HEADS, KEYS - N_META - SEQ), f32)


def _forget_bias(f_logit, b_forget):
    return pl.pallas_call(
        _bias_kernel,
        grid=(BATCH + 1,),
        in_specs=[
            pl.BlockSpec((SEQ, LANES), lambda t: (_seq_block(t), 0)),
            pl.BlockSpec((1, LANES), lambda t: (0, 0)),
        ],
        out_specs=pl.BlockSpec((FOX_HEADS, 1, 1, KEYS), lambda t: (0, jnp.maximum(t - 1, 0), 0, 0)),
        out_shape=jax.ShapeDtypeStruct((FOX_HEADS, BATCH, 1, KEYS), f32),
        scratch_shapes=[pltpu.VMEM((LANES, LANES), f32)],
        compiler_params=pltpu.CompilerParams(dimension_semantics=("arbitrary",)),
        name="forget_bias",
    )(f_logit, b_forget)


def _scores(q, k, bias, mask):
    s = lax.dot_general(q, k, (((1,), (1,)), ((), ())), preferred_element_type=f32) + bias
    return s if mask is None else jnp.where(mask, s, NEG_INF)


def _softmax_pv(scores, values):
    m = functools.reduce(jnp.maximum, [jnp.max(s, axis=-1, keepdims=True) for s in scores])
    acc = None
    for s, v in zip(scores, values):
        pv = jnp.dot(jnp.exp2(s - m).astype(bf16), v, preferred_element_type=f32)
        acc = pv if acc is None else acc + pv
    return acc[:, 0:FOX_HEAD_DIM] * (1.0 / acc[:, FOX_HEAD_DIM:])


def _fox_kernel(q_ref, r_ref, k_ref, v_ref, b_ref, y_ref, kc_ref, vc_ref):
    t = pl.program_id(1)
    dh = FOX_HEAD_DIM

    def lower_tri(nq, nk, shift):
        i = lax.broadcasted_iota(jnp.int32, (nq, nk), 0)
        j = lax.broadcasted_iota(jnp.int32, (nq, nk), 1)
        return j <= i + shift

    def finish(rows, o):
        y_ref[rows, :] = (o * _silu(r_ref[rows, :].astype(f32))).astype(y_ref.dtype)

    @pl.when(t == 0)
    def _():
        for s in range(FOX_STREAMS):
            vc_ref[s, :, dh:] = jnp.ones((KEYS, dh), bf16)
            kc_ref[s, 0:N_META, :] = k_ref[0:N_META, :]
            vc_ref[s, 0:N_META, 0:dh] = v_ref[0:N_META, :]
        sc = _scores(q_ref[0:N_META, :], kc_ref[0, 0:N_META, :], b_ref[0, 0, :, 0:N_META],
                     lower_tri(N_META, N_META, 0))
        finish(pl.ds(0, N_META), _softmax_pv([sc], [vc_ref[0, 0:N_META, :]]))
        y_ref[N_META:LANES, :] = jnp.zeros((LANES - N_META, dh), y_ref.dtype)

    @pl.when(t > 0)
    def _():
        n_keys = N_META + SEQ
        for s in range(FOX_STREAMS):
            kc_ref[s, N_META:n_keys, :] = k_ref[s * SEQ:(s + 1) * SEQ, :]
            vc_ref[s, N_META:n_keys, 0:dh] = v_ref[s * SEQ:(s + 1) * SEQ, :]

        def tile_scores(s, ti):
            p0 = ti * FOX_TQ
            k1 = min(p0 + FOX_TQ, n_keys)
            q0 = max(p0 - N_META, 0)
            q1 = k1 - N_META
            q = q_ref[s * SEQ + q0:s * SEQ + q1, :]
            k0 = p0 // MXU_COLS * MXU_COLS
            scores, values = [], []
            if k0 > 0:
                scores.append(_scores(q, kc_ref[s, 0:k0, :], b_ref[0, s, :, 0:k0], None))
                values.append(vc_ref[s, 0:k0, :])
            shift = q0 + N_META - k0
            scores.append(_scores(q, kc_ref[s, k0:k1, :], b_ref[0, s, :, k0:k1],
                                  lower_tri(q1 - q0, k1 - k0, shift)))
            values.append(vc_ref[s, k0:k1, :])
            return pl.ds(s * SEQ + q0, q1 - q0), scores, values

        n_tiles = -(-n_keys // FOX_TQ)
        order = [(s, ti if s % 2 == 0 else n_tiles - 1 - ti) for ti in range(n_tiles) for s in range(FOX_STREAMS)]
        pending = tile_scores(*order[0])
        for pos in range(len(order)):
            rows, scores, values = pending
            if pos + 1 < len(order):
                pending = tile_scores(*order[pos + 1])
            finish(rows, _softmax_pv(scores, values))


def _fox_mix(qr, kv, bias):
    nh = FOX_HEADS
    rows = FOX_STREAMS * SEQ
    n_steps = BATCH // FOX_STREAMS

    def row_block(t):
        return jnp.where(t == 0, n_steps, t - 1)

    return pl.pallas_call(
        _fox_kernel,
        grid=(FOX_HEADS, n_steps + 1),
        in_specs=[
            pl.BlockSpec((rows, FOX_HEAD_DIM), lambda h, t: (row_block(t), h)),
            pl.BlockSpec((rows, FOX_HEAD_DIM), lambda h, t: (row_block(t), nh + h)),
            pl.BlockSpec((rows, FOX_HEAD_DIM), lambda h, t: (row_block(t), h)),
            pl.BlockSpec((rows, FOX_HEAD_DIM), lambda h, t: (row_block(t), nh + h)),
            pl.BlockSpec((1, FOX_STREAMS, 1, KEYS), lambda h, t: (h, jnp.maximum(t - 1, 0), 0, 0)),
        ],
        out_specs=pl.BlockSpec((rows, FOX_HEAD_DIM), lambda h, t: (row_block(t), h)),
        out_shape=jax.ShapeDtypeStruct((ROWS, D_INNER), bf16),
        scratch_shapes=[pltpu.VMEM((FOX_STREAMS, KEYS, FOX_HEAD_DIM), bf16),
                        pltpu.VMEM((FOX_STREAMS, KEYS, 2 * FOX_HEAD_DIM), bf16)],
        compiler_params=pltpu.CompilerParams(
            dimension_semantics=("arbitrary", "arbitrary"), vmem_limit_bytes=VMEM_LIMIT),
        name="fox_mix",
    )(qr, qr, kv, kv, bias)


def _prep_kernel(outputs, w_ref, g_ref, *out_refs):
    g = g_ref[...]
    for (width, pieces), o_ref in zip(outputs, out_refs):
        if sum(p[1] for p in pieces) < width:
            o_ref[...] = jnp.zeros(o_ref.shape, o_ref.dtype)
        for src, cols, dst, scale in pieces:
            o_ref[:, dst:dst + cols] = (w_ref[:, src:src + cols] * (g * scale)).astype(o_ref.dtype)


def _prep_weight(w, row_gain, outputs):
    layers, k, n = w.shape
    return pl.pallas_call(
        functools.partial(_prep_kernel, outputs),
        grid=(layers, k // PREP_TILE),
        in_specs=[
            pl.BlockSpec((None, PREP_TILE, n), lambda l, i: (l, i, 0)),
            pl.BlockSpec((None, PREP_TILE, 1), lambda l, i: (l, i, 0)),
        ],
        out_specs=[pl.BlockSpec((None, PREP_TILE, width), lambda l, i: (l, i, 0)) for width, _ in outputs],
        out_shape=[jax.ShapeDtypeStruct((layers, k, width), bf16) for width, _ in outputs],
        compiler_params=pltpu.CompilerParams(
            dimension_semantics=("arbitrary", "arbitrary"), vmem_limit_bytes=VMEM_LIMIT),
        name="prep_weight",
    )(w, row_gain.reshape(layers, k, 1))


def _prep_rows_kernel(scaled_tiles, scale, w_ref, g_ref, o_ref):
    s = jnp.where(pl.program_id(1) < scaled_tiles, scale, 1.0)
    o_ref[...] = (w_ref[...] * (g_ref[...] * s)).astype(o_ref.dtype)


def _prep_rows(w_t, col_gain, n_rows, scaled_rows, scale):
    layers, _, k = w_t.shape
    tile = PREP_ROWS_TILE
    assert n_rows % tile == 0 and scaled_rows % tile == 0
    return pl.pallas_call(
        functools.partial(_prep_rows_kernel, scaled_rows // tile, scale),
        grid=(layers, n_rows // tile),
        in_specs=[
            pl.BlockSpec((None, tile, k), lambda l, i: (l, i, 0)),
            pl.BlockSpec((None, 1, k), lambda l, i: (l, 0, 0)),
        ],
        out_specs=pl.BlockSpec((None, tile, k), lambda l, i: (l, i, 0)),
        out_shape=jax.ShapeDtypeStruct((layers, n_rows, k), bf16),
        compiler_params=pltpu.CompilerParams(
            dimension_semantics=("arbitrary", "arbitrary"), vmem_limit_bytes=VMEM_LIMIT),
        name="prep_rows",
    )(w_t, col_gain.reshape(layers, 1, k))


def _pad_last(w, axis):
    pad = [(0, 0)] * w.ndim
    pad[axis] = (0, LANES - w.shape[axis])
    return jnp.pad(w, pad)


def kernel(x, meta_tokens, norm_g, gla_w_in, gla_w_gate_up, gla_b_gate, gla_head_norm_g, gla_w_out,
           kv_norm_g, fox_w_kv, fox_b_forget, fox_w_in, fox_w_out, final_norm_g):
    n_gla = gla_w_in.shape[0]
    n_fox = fox_w_in.shape[0]

    o_g = 2 * GLA_DK_TOTAL + D_INNER
    o_r = o_g + GLA_GATE_RANK
    gla_g = norm_g[:n_gla]
    gla_w_t = jnp.swapaxes(gla_w_in, 1, 2)
    gla_w_qkv = _prep_rows(gla_w_t, gla_g, o_g, GLA_DK_TOTAL, GLA_HEAD_K ** -0.5)
    gla_w_r = _prep_rows(gla_w_t[:, o_r:], gla_g, D_INNER, 0, 1.0)
    gla_w_low = _pad_last(gla_w_t[:, o_g:o_r] * gla_g[:, None, :], 1).astype(bf16)
    gla_w_up = _pad_last(gla_w_gate_up, 1).astype(bf16)
    gla_w_o, = _prep_weight(gla_w_out, jnp.tile(gla_head_norm_g, (1, GLA_HEADS)), [(D_MODEL, [(0, D_MODEL, 0, 1.0)])])
    kv_w_t = fox_w_kv.T[None]
    kv_w = _prep_rows(kv_w_t, kv_norm_g[None], 2 * D_INNER, 0, 1.0)
    kv_w_f = _pad_last(kv_w_t[:, 2 * D_INNER:] * kv_norm_g[None, None, :], 1).astype(bf16)
    fox_w_qr, = _prep_weight(fox_w_in, norm_g[n_gla:], [
        (2 * D_INNER, [(0, D_INNER, 0, FOX_HEAD_DIM ** -0.5 * LOG2E), (D_INNER, D_INNER, D_INNER, 1.0)])])
    fox_w_o, = _prep_weight(fox_w_out, jnp.ones(fox_w_out.shape[:2], f32), [(D_MODEL, [(0, D_MODEL, 0, 1.0)])])

    h, hb, rs, narrow = _embed(x.reshape(ROWS_X, D_MODEL), meta_tokens, (gla_w_low, 0))
    for layer in range(n_gla):
        proj = _proj(hb, rs, gla_w_qkv, layer, o_g + D_INNER, w_tail=gla_w_r, transposed=True, gated=True)
        o = _gla_mix(proj, narrow, gla_w_up, gla_b_gate.reshape(n_gla, 1, -1), layer)
        next_narrow = (gla_w_low, layer + 1) if layer + 1 < n_gla else (kv_w_f, 0)
        h, hb, rs, narrow = _gla_out_proj(o, proj, gla_w_o, layer, h, next_narrow)

    kv = _proj(hb, rs, kv_w, 0, 2 * D_INNER, transposed=True)
    bias = _forget_bias(narrow, _pad_last(fox_b_forget.reshape(1, -1), 1))

    for j in range(n_fox):
        qr = _proj(hb, rs, fox_w_qr, j, 2 * D_INNER)
        y = _fox_mix(qr, kv, bias)
        if j + 1 < n_fox:
            h, hb, rs = _out_proj(y, fox_w_o, j, h)
        else:
            out = _out_proj(y, fox_w_o, j, h, final_g=final_norm_g)
    return out.reshape(BATCH, SEQ, D_MODEL)
```

```python
import functools

import jax
import jax.numpy as jnp
from jax import lax
from jax.experimental import pallas as pl
from jax.experimental.pallas import tpu as pltpu

f32 = jnp.float32
bf16 = jnp.bfloat16

D_MODEL = 1024
BATCH = 8
SEQ = 2048
N_META = 16
D_INNER = 2048
GLA_HEADS = 4
GLA_HEAD_K = 256
GLA_HEAD_V = 512
GLA_DK_TOTAL = GLA_HEADS * GLA_HEAD_K
GLA_GATE_RANK = 16
GLA_GATE_NORMALIZER = 16.0
FOX_HEADS = 16
FOX_HEAD_DIM = 128
RMS_EPS = 1e-6
NEG_INF = -1e30
LOG2E = 1.4426950408889634

LANES = 128
ROWS_X = BATCH * SEQ
ROWS = ROWS_X + LANES
META_BLOCK = BATCH
VMEM_LIMIT = 56 * 1024 * 1024

GLA_CHUNK = 128
GLA_BASE = 32
GLA_GROUP = 16
MXU_COLS = 256
FOX_TQ = 256
KEYS = 2304
FOX_STREAMS = 1
ROW_TILE = 688
PROJ_TILE = 1376
PROJ_COLS = 2048
OUT_TILE = 1024
EMBED_TILE = 2048
PREP_TILE = 512
PREP_ROWS_TILE = 1024


def _seq_block(t):
    return jnp.where(t == 0, META_BLOCK, t - 1)


def _log2_sigmoid(x):
    return jnp.minimum(x, 0.0) * LOG2E - jnp.log2(1.0 + jnp.exp2(jnp.abs(x) * -LOG2E))


def _silu(x):
    half = 0.5 * x
    return half + half * jnp.tanh(half)


def _split_dot(a, b, pieces, split_rhs):
    x = b if split_rhs else a
    acc = None
    for _ in range(pieces):
        p = x.astype(bf16)
        term = (jnp.dot(a, p, preferred_element_type=f32) if split_rhs
                else jnp.dot(p, b, preferred_element_type=f32))
        acc = term if acc is None else acc + term
        x = x - p.astype(f32)
    return acc


def _emit_stream(h, wn_ref, h_ref, hb_ref, rs_ref, narrow_ref=None):
    hb = h.astype(bf16)
    rs = lax.rsqrt(jnp.mean(h * h, axis=-1, keepdims=True) + RMS_EPS)
    h_ref[...] = h
    hb_ref[...] = hb
    rs_ref[...] = rs
    if narrow_ref is not None:
        narrow_ref[...] = _proj_dot(True, hb, wn_ref[...]) * rs


def _stream_specs(rows, d, tile, narrow):
    out_specs = [pl.BlockSpec((tile, d), lambda i: (i, 0)),
                 pl.BlockSpec((tile, d), lambda i: (i, 0)),
                 pl.BlockSpec((tile, 1), lambda i: (i, 0))]
    out_shapes = [jax.ShapeDtypeStruct((rows, d), f32),
                  jax.ShapeDtypeStruct((rows, d), bf16),
                  jax.ShapeDtypeStruct((rows, 1), f32)]
    if narrow is None:
        return [], [], out_specs, out_shapes
    w, layer = narrow
    out_specs.append(pl.BlockSpec((tile, LANES), lambda i: (i, 0)))
    out_shapes.append(jax.ShapeDtypeStruct((rows, LANES), f32))
    return [pl.BlockSpec((None, LANES, d), lambda i: (layer, 0, 0))], [w], out_specs, out_shapes


def _embed_kernel(x_ref, meta_ref, wn_ref, *out_refs):
    i = pl.program_id(0)

    @pl.when(i < ROWS_X // EMBED_TILE)
    def _():
        _emit_stream(x_ref[...], wn_ref, *out_refs)

    @pl.when(i == ROWS_X // EMBED_TILE)
    def _():
        pad = jnp.zeros((EMBED_TILE - N_META, D_MODEL), f32)
        _emit_stream(jnp.concatenate([meta_ref[...], pad], axis=0), wn_ref, *out_refs)


def _embed(x2d, meta_tokens, narrow):
    n_x = ROWS_X // EMBED_TILE
    w_specs, w_args, out_specs, out_shapes = _stream_specs(ROWS, D_MODEL, EMBED_TILE, narrow)
    return pl.pallas_call(
        _embed_kernel,
        grid=(n_x + 1,),
        in_specs=[
            pl.BlockSpec((EMBED_TILE, D_MODEL), lambda i: (jnp.minimum(i, n_x - 1), 0)),
            pl.BlockSpec((N_META, D_MODEL), lambda i: (0, 0)),
        ] + w_specs,
        out_specs=out_specs,
        out_shape=out_shapes,
        compiler_params=pltpu.CompilerParams(dimension_semantics=("arbitrary",), vmem_limit_bytes=VMEM_LIMIT),
        name="embed",
    )(x2d, meta_tokens, *w_args)


def _proj_dot(transposed, x, w):
    dims = (((1,), (1 if transposed else 0,)), ((), ()))
    return lax.dot_general(x, w, dims, preferred_element_type=f32)


def _proj_store(transposed, gated, xb_ref, rs_ref, w_ref, o_ref):
    acc = _proj_dot(transposed, xb_ref[...], w_ref[...]) * rs_ref[...]
    o_ref[...] = (_silu(acc) if gated else acc).astype(o_ref.dtype)


def _proj_kernel(transposed, xb_ref, rs_ref, w_ref, o_ref):
    _proj_store(transposed, False, xb_ref, rs_ref, w_ref, o_ref)


def _proj_gated_kernel(transposed, has_tail, xb_ref, rs_ref, w_ref, *rest):
    wt_ref, o_ref = rest if has_tail else (w_ref,) + rest
    last = pl.num_programs(1) - 1

    @pl.when(pl.program_id(1) < last)
    def _():
        _proj_store(transposed, False, xb_ref, rs_ref, w_ref, o_ref)

    @pl.when(pl.program_id(1) == last)
    def _():
        _proj_store(transposed, True, xb_ref, rs_ref, wt_ref, o_ref)


def _proj(hb, rs, w, layer, n, w_tail=None, transposed=False, gated=False):
    rows, d = hb.shape
    tm, tn = PROJ_TILE, PROJ_COLS
    assert rows % tm == 0 and n % tn == 0 and (gated or w_tail is None)
    n_w = n // tn - (w_tail is not None)

    def w_spec(col_block):
        if transposed:
            return pl.BlockSpec((None, tn, d), lambda i, j: (layer, col_block(j), 0))
        return pl.BlockSpec((None, d, tn), lambda i, j: (layer, 0, col_block(j)))

    in_specs = [
        pl.BlockSpec((tm, d), lambda i, j: (i, 0)),
        pl.BlockSpec((tm, 1), lambda i, j: (i, 0)),
        w_spec(lambda j: jnp.minimum(j, n_w - 1)),
    ]
    args = [hb, rs, w]
    if w_tail is not None:
        in_specs.append(w_spec(lambda j: 0))
        args.append(w_tail)
    body = (functools.partial(_proj_gated_kernel, transposed, w_tail is not None) if gated
            else functools.partial(_proj_kernel, transposed))
    return pl.pallas_call(
        body,
        grid=(rows // tm, n // tn),
        in_specs=in_specs,
        out_specs=pl.BlockSpec((tm, tn), lambda i, j: (i, j)),
        out_shape=jax.ShapeDtypeStruct((rows, n), bf16),
        compiler_params=pltpu.CompilerParams(
            dimension_semantics=("arbitrary", "arbitrary"), vmem_limit_bytes=VMEM_LIMIT),
        name="proj",
    )(*args)


def _out_proj_kernel(y_ref, w_ref, h_ref, *out_refs):
    _emit_stream(h_ref[...] + jnp.dot(y_ref[...], w_ref[...], preferred_element_type=f32), None, *out_refs)


def _out_proj_final_kernel(y_ref, w_ref, h_ref, g_ref, o_ref):
    x = h_ref[...] + jnp.dot(y_ref[...], w_ref[...], preferred_element_type=f32)
    var = jnp.mean(x * x, axis=-1, keepdims=True)
    o_ref[...] = x * lax.rsqrt(var + RMS_EPS) * g_ref[...]


def _out_proj(y, w, layer, h, final_g=None):
    _, k, d = w.shape
    if final_g is None:
        rows, tile, body, extra, extra_specs = h.shape[0], ROW_TILE, _out_proj_kernel, (), []
        _, _, out_specs, out_shape = _stream_specs(rows, d, tile, None)
    else:
        rows, tile, body = ROWS_X, OUT_TILE, _out_proj_final_kernel
        extra, extra_specs = (final_g.reshape(1, d),), [pl.BlockSpec((1, d), lambda i: (0, 0))]
        out_specs, out_shape = pl.BlockSpec((tile, d), lambda i: (i, 0)), jax.ShapeDtypeStruct((rows, d), f32)
    return pl.pallas_call(
        body,
        grid=(rows // tile,),
        in_specs=[
            pl.BlockSpec((tile, k), lambda i: (i, 0)),
            pl.BlockSpec((None, k, d), lambda i: (layer, 0, 0)),
            pl.BlockSpec((tile, d), lambda i: (i, 0)),
        ] + extra_specs,
        out_specs=out_specs,
        out_shape=out_shape,
        compiler_params=pltpu.CompilerParams(
            dimension_semantics=("arbitrary",), vmem_limit_bytes=VMEM_LIMIT),
        name="out_proj" if final_g is None else "out_proj_final",
    )(y, w, h, *extra)


def _gla_out_kernel(o_ref, gate_ref, w_ref, h_ref, wn_ref, *out_refs):
    acc = h_ref[...]
    for hd in range(GLA_HEADS):
        cols = slice(hd * GLA_HEAD_V, (hd + 1) * GLA_HEAD_V)
        o = o_ref[:, cols].astype(f32)
        var = jnp.mean(o * o, axis=-1, keepdims=True)
        y = o * lax.rsqrt(var + RMS_EPS) * gate_ref[:, cols].astype(f32)
        acc = acc + jnp.dot(y.astype(bf16), w_ref[cols, :], preferred_element_type=f32)
    _emit_stream(acc, wn_ref, *out_refs)


def _gla_out_proj(o, proj, w, layer, h, narrow):
    _, k, d = w.shape
    rows = h.shape[0]
    w_specs, w_args, out_specs, out_shape = _stream_specs(rows, d, ROW_TILE, narrow)
    return pl.pallas_call(
        _gla_out_kernel,
        grid=(rows // ROW_TILE,),
        in_specs=[
            pl.BlockSpec((ROW_TILE, k), lambda i: (i, 0)),
            pl.BlockSpec((ROW_TILE, k), lambda i: (i, proj.shape[1] // k - 1)),
            pl.BlockSpec((None, k, d), lambda i: (layer, 0, 0)),
            pl.BlockSpec((ROW_TILE, d), lambda i: (i, 0)),
        ] + w_specs,
        out_specs=out_specs,
        out_shape=out_shape,
        compiler_params=pltpu.CompilerParams(
            dimension_semantics=("arbitrary",), vmem_limit_bytes=VMEM_LIMIT),
        name="gla_out_proj",
    )(o, proj, w, h, *w_args)


def _gla_masks(c):
    i = lax.broadcasted_iota(jnp.int32, (c, c), 0)
    j = lax.broadcasted_iota(jnp.int32, (c, c), 1)
    base = min(c, GLA_BASE)
    uses = [((i & -base) == (j & -base)) & (j <= i)]
    size = 2 * base
    while size <= c:
        half = size // 2
        uses.append(((i & -size) == (j & -size)) & ((i & half) != 0) & ((j & half) == 0))
        size *= 2
    return (j <= i).astype(bf16), uses


def _gla_group(nc, c, masks, q, k, v, log2_a, state_t, emit):
    n, dk = q.shape
    nt = (((1,), (1,)), ((), ()))
    tn = (((0,), (0,)), ((), ()))
    tril, uses = masks
    row = lax.broadcasted_iota(jnp.int32, (n, 1), 0)

    b = jnp.concatenate([_split_dot(tril, log2_a[ci * c:(ci + 1) * c], 2, split_rhs=True)
                         for ci in range(nc)], axis=0)

    def minus_anchor(size):
        b3 = b.reshape(n // size, size, dk)
        return (b3 - b3[:, size // 2 - 1:size // 2, :]).reshape(n, dk)

    base = min(c, GLA_BASE)
    e0 = minus_anchor(base)
    operands = [((q * jnp.exp2(e0)).astype(bf16), (k * jnp.exp2(-e0)).astype(bf16))]
    size = 2 * base
    while size <= c:
        qk = (jnp.where((row & (size // 2)) != 0, q, k) * jnp.exp2(-jnp.abs(minus_anchor(size)))).astype(bf16)
        operands.append((qk, qk))
        size *= 2

    b3 = b.reshape(nc, c, dk)
    b_last = b3[:, c - 1:c, :]
    q_dec = (q * jnp.exp2(b)).astype(bf16)
    k_rem = (k.reshape(nc, c, dk) * jnp.exp2(b_last - b3)).reshape(n, dk).astype(bf16)
    decay = jnp.exp2(b_last)

    o_intra, update = [], []
    for ci in range(nc):
        rows = slice(ci * c, (ci + 1) * c)
        attn = None
        for (lhs, rhs), use in zip(operands, uses):
            p = lax.dot_general(lhs[rows], rhs[rows], nt, preferred_element_type=f32)
            attn = jnp.where(use, p, 0.0 if attn is None else attn)
        o_intra.append(jnp.dot(attn.astype(bf16), v[rows], preferred_element_type=f32))
        update.append(lax.dot_general(v[rows], k_rem[rows], tn, preferred_element_type=f32))

    for ci in range(nc):
        rows = slice(ci * c, (ci + 1) * c)
        emit(ci, o_intra[ci] + lax.dot_general(q_dec[rows], state_t.astype(bf16), nt,
                                               preferred_element_type=f32))
        state_t = state_t * decay[ci] + update[ci]
    return state_t


def _gla_kernel(q_ref, k_ref, v_ref, g_ref, wup_ref, bg_ref, o_ref, st_ref, st0_ref):
    t = pl.program_id(1)
    wup = wup_ref[...]
    bg = bg_ref[...]

    def run_group(row0, nc, c, masks, state_t):
        rows = pl.ds(row0, nc * c)
        logits = jnp.dot(g_ref[rows, :].astype(bf16), wup, preferred_element_type=f32) + bg
        log2_a = _log2_sigmoid(logits) * (1.0 / GLA_GATE_NORMALIZER)

        def emit(ci, o):
            o_ref[pl.ds(row0 + ci * c, c), :] = o.astype(o_ref.dtype)

        return _gla_group(nc, c, masks, q_ref[rows, :].astype(f32), k_ref[rows, :].astype(f32),
                          v_ref[rows, :], log2_a, state_t, emit)

    @pl.when(t == 0)
    def _():
        zero_state = jnp.zeros((GLA_HEAD_V, GLA_HEAD_K), f32)
        st0_ref[...] = run_group(0, 1, N_META, _gla_masks(N_META), zero_state)
        o_ref[N_META:LANES, :] = jnp.zeros((LANES - N_META, GLA_HEAD_V), o_ref.dtype)

    @pl.when(t > 0)
    def _():
        st_ref[...] = st0_ref[...]
        group_rows = GLA_GROUP * GLA_CHUNK
        masks = _gla_masks(GLA_CHUNK)

        def body(gi, carry):
            row0 = pl.multiple_of(gi * group_rows, group_rows)
            st_ref[...] = run_group(row0, GLA_GROUP, GLA_CHUNK, masks, st_ref[...])
            return carry

        lax.fori_loop(0, SEQ // group_rows, body, 0)


def _gla_mix(proj, g_low, w_up, b_gate, layer):
    kq = GLA_DK_TOTAL // GLA_HEAD_K
    kv = 2 * GLA_DK_TOTAL // GLA_HEAD_V
    return pl.pallas_call(
        _gla_kernel,
        grid=(GLA_HEADS, BATCH + 1),
        in_specs=[
            pl.BlockSpec((SEQ, GLA_HEAD_K), lambda h, t: (_seq_block(t), h)),
            pl.BlockSpec((SEQ, GLA_HEAD_K), lambda h, t: (_seq_block(t), kq + h)),
            pl.BlockSpec((SEQ, GLA_HEAD_V), lambda h, t: (_seq_block(t), kv + h)),
            pl.BlockSpec((SEQ, LANES), lambda h, t: (_seq_block(t), 0)),
            pl.BlockSpec((None, LANES, GLA_HEAD_K), lambda h, t: (layer, 0, h)),
            pl.BlockSpec((None, 1, GLA_HEAD_K), lambda h, t: (layer, 0, h)),
        ],
        out_specs=pl.BlockSpec((SEQ, GLA_HEAD_V), lambda h, t: (_seq_block(t), h)),
        out_shape=jax.ShapeDtypeStruct((ROWS, D_INNER), bf16),
        scratch_shapes=[pltpu.VMEM((GLA_HEAD_V, GLA_HEAD_K), f32),
                        pltpu.VMEM((GLA_HEAD_V, GLA_HEAD_K), f32)],
        compiler_params=pltpu.CompilerParams(
            dimension_semantics=("arbitrary", "arbitrary"), vmem_limit_bytes=VMEM_LIMIT),
        name="gla_mix",
    )(proj, proj, proj, g_low, w_up, b_gate)


def _bias_kernel(f_ref, b_ref, o_ref, meta_ref):
    t = pl.program_id(0)
    i = lax.broadcasted_iota(jnp.int32, (LANES, LANES), 0)
    j = lax.broadcasted_iota(jnp.int32, (LANES, LANES), 1)
    upper = (i <= j).astype(bf16)

    def block_cumsum(c, carry):
        log2_f = _log2_sigmoid(f_ref[c * LANES:(c + 1) * LANES, :] + b_ref[...])
        return _split_dot(log2_f.T, upper, 3, split_rhs=False) + carry

    @pl.when(t == 0)
    def _():
        meta_ref[...] = block_cumsum(0, jnp.zeros((LANES, 1), f32))

    @pl.when(t > 0)
    def _():
        meta = meta_ref[...]
        o_ref[:, 0, 0, 0:N_META] = -meta[0:FOX_HEADS, 0:N_META]
        carry = meta[:, N_META - 1:N_META]
        for c in range(SEQ // LANES):
            cs = block_cumsum(c, carry)
            o_ref[:, 0, 0, N_META + c * LANES:N_META + (c + 1) * LANES] = -cs[0:FOX_HEADS, :]
            carry = cs[:, LANES - 1:LANES]
        o_ref[:, 0, 0, N_META + SEQ:KEYS] = jnp.zeros((FOX_HEADS, KEYS - N_META - SEQ), f32)


def _forget_bias(f_logit, b_forget):
    return pl.pallas_call(
        _bias_kernel,
        grid=(BATCH + 1,),
        in_specs=[
            pl.BlockSpec((SEQ, LANES), lambda t: (_seq_block(t), 0)),
            pl.BlockSpec((1, LANES), lambda t: (0, 0)),
        ],
        out_specs=pl.BlockSpec((FOX_HEADS, 1, 1, KEYS), lambda t: (0, jnp.maximum(t - 1, 0), 0, 0)),
        out_shape=jax.ShapeDtypeStruct((FOX_HEADS, BATCH, 1, KEYS), f32),
        scratch_shapes=[pltpu.VMEM((LANES, LANES), f32)],
        compiler_params=pltpu.CompilerParams(dimension_semantics=("arbitrary",)),
        name="forget_bias",
    )(f_logit, b_forget)


def _scores(q, k, bias, mask):
    s = lax.dot_general(q, k, (((1,), (1,)), ((), ())), preferred_element_type=f32) + bias
    return s if mask is None else jnp.where(mask, s, NEG_INF)


def _softmax_pv(scores, values):
    m = functools.reduce(jnp.maximum, [jnp.max(s, axis=-1, keepdims=True) for s in scores])
    acc = None
    for s, v in zip(scores, values):
        pv = jnp.dot(jnp.exp2(s - m).astype(bf16), v, preferred_element_type=f32)
        acc = pv if acc is None else acc + pv
    return acc[:, 0:FOX_HEAD_DIM] * (1.0 / acc[:, FOX_HEAD_DIM:])


def _fox_kernel(q_ref, r_ref, k_ref, v_ref, b_ref, y_ref, kc_ref, vc_ref):
    t = pl.program_id(1)
    dh = FOX_HEAD_DIM

    def lower_tri(nq, nk, shift):
        i = lax.broadcasted_iota(jnp.int32, (nq, nk), 0)
        j = lax.broadcasted_iota(jnp.int32, (nq, nk), 1)
        return j <= i + shift

    def finish(rows, o):
        y_ref[rows, :] = (o * _silu(r_ref[rows, :].astype(f32))).astype(y_ref.dtype)

    @pl.when(t == 0)
    def _():
        for s in range(FOX_STREAMS):
            vc_ref[s, :, dh:] = jnp.ones((KEYS, dh), bf16)
            kc_ref[s, 0:N_META, :] = k_ref[0:N_META, :]
            vc_ref[s, 0:N_META, 0:dh] = v_ref[0:N_META, :]
        sc = _scores(q_ref[0:N_META, :], kc_ref[0, 0:N_META, :], b_ref[0, 0, :, 0:N_META],
                     lower_tri(N_META, N_META, 0))
        finish(pl.ds(0, N_META), _softmax_pv([sc], [vc_ref[0, 0:N_META, :]]))
        y_ref[N_META:LANES, :] = jnp.zeros((LANES - N_META, dh), y_ref.dtype)

    @pl.when(t > 0)
    def _():
        n_keys = N_META + SEQ
        for s in range(FOX_STREAMS):
            kc_ref[s, N_META:n_keys, :] = k_ref[s * SEQ:(s + 1) * SEQ, :]
            vc_ref[s, N_META:n_keys, 0:dh] = v_ref[s * SEQ:(s + 1) * SEQ, :]

        def tile_scores(s, ti):
            p0 = ti * FOX_TQ
            k1 = min(p0 + FOX_TQ, n_keys)
            q0 = max(p0 - N_META, 0)
            q1 = k1 - N_META
            q = q_ref[s * SEQ + q0:s * SEQ + q1, :]
            k0 = p0 // MXU_COLS * MXU_COLS
            scores, values = [], []
            if k0 > 0:
                scores.append(_scores(q, kc_ref[s, 0:k0, :], b_ref[0, s, :, 0:k0], None))
                values.append(vc_ref[s, 0:k0, :])
            shift = q0 + N_META - k0
            scores.append(_scores(q, kc_ref[s, k0:k1, :], b_ref[0, s, :, k0:k1],
                                  lower_tri(q1 - q0, k1 - k0, shift)))
            values.append(vc_ref[s, k0:k1, :])
            return pl.ds(s * SEQ + q0, q1 - q0), scores, values

        n_tiles = -(-n_keys // FOX_TQ)
        order = [(s, ti if s % 2 == 0 else n_tiles - 1 - ti) for ti in range(n_tiles) for s in range(FOX_STREAMS)]
        pending = tile_scores(*order[0])
        for pos in range(len(order)):
            rows, scores, values = pending
            if pos + 1 < len(order):
                pending = tile_scores(*order[pos + 1])
            finish(rows, _softmax_pv(scores, values))


def _fox_mix(qr, kv, bias):
    nh = FOX_HEADS
    rows = FOX_STREAMS * SEQ
    n_steps = BATCH // FOX_STREAMS

    def row_block(t):
        return jnp.where(t == 0, n_steps, t - 1)

    return pl.pallas_call(
        _fox_kernel,
        grid=(FOX_HEADS, n_steps + 1),
        in_specs=[
            pl.BlockSpec((rows, FOX_HEAD_DIM), lambda h, t: (row_block(t), h)),
            pl.BlockSpec((rows, FOX_HEAD_DIM), lambda h, t: (row_block(t), nh + h)),
            pl.BlockSpec((rows, FOX_HEAD_DIM), lambda h, t: (row_block(t), h)),
            pl.BlockSpec((rows, FOX_HEAD_DIM), lambda h, t: (row_block(t), nh + h)),
            pl.BlockSpec((1, FOX_STREAMS, 1, KEYS), lambda h, t: (h, jnp.maximum(t - 1, 0), 0, 0)),
        ],
        out_specs=pl.BlockSpec((rows, FOX_HEAD_DIM), lambda h, t: (row_block(t), h)),
        out_shape=jax.ShapeDtypeStruct((ROWS, D_INNER), bf16),
        scratch_shapes=[pltpu.VMEM((FOX_STREAMS, KEYS, FOX_HEAD_DIM), bf16),
                        pltpu.VMEM((FOX_STREAMS, KEYS, 2 * FOX_HEAD_DIM), bf16)],
        compiler_params=pltpu.CompilerParams(
            dimension_semantics=("arbitrary", "arbitrary"), vmem_limit_bytes=VMEM_LIMIT),
        name="fox_mix",
    )(qr, qr, kv, kv, bias)


def _prep_kernel(outputs, w_ref, g_ref, *out_refs):
    g = g_ref[...]
    for (width, pieces), o_ref in zip(outputs, out_refs):
        if sum(p[1] for p in pieces) < width:
            o_ref[...] = jnp.zeros(o_ref.shape, o_ref.dtype)
        for src, cols, dst, scale in pieces:
            o_ref[:, dst:dst + cols] = (w_ref[:, src:src + cols] * (g * scale)).astype(o_ref.dtype)


def _prep_weight(w, row_gain, outputs):
    layers, k, n = w.shape
    return pl.pallas_call(
        functools.partial(_prep_kernel, outputs),
        grid=(layers, k // PREP_TILE),
        in_specs=[
            pl.BlockSpec((None, PREP_TILE, n), lambda l, i: (l, i, 0)),
            pl.BlockSpec((None, PREP_TILE, 1), lambda l, i: (l, i, 0)),
        ],
        out_specs=[pl.BlockSpec((None, PREP_TILE, width), lambda l, i: (l, i, 0)) for width, _ in outputs],
        out_shape=[jax.ShapeDtypeStruct((layers, k, width), bf16) for width, _ in outputs],
        compiler_params=pltpu.CompilerParams(
            dimension_semantics=("arbitrary", "arbitrary"), vmem_limit_bytes=VMEM_LIMIT),
        name="prep_weight",
    )(w, row_gain.reshape(layers, k, 1))


def _prep_rows_kernel(scaled_tiles, scale, w_ref, g_ref, o_ref):
    s = jnp.where(pl.program_id(1) < scaled_tiles, scale, 1.0)
    o_ref[...] = (w_ref[...] * (g_ref[...] * s)).astype(o_ref.dtype)


def _prep_rows(w_t, col_gain, n_rows, scaled_rows, scale):
    layers, _, k = w_t.shape
    tile = PREP_ROWS_TILE
    assert n_rows % tile == 0 and scaled_rows % tile == 0
    return pl.pallas_call(
        functools.partial(_prep_rows_kernel, scaled_rows // tile, scale),
        grid=(layers, n_rows // tile),
        in_specs=[
            pl.BlockSpec((None, tile, k), lambda l, i: (l, i, 0)),
            pl.BlockSpec((None, 1, k), lambda l, i: (l, 0, 0)),
        ],
        out_specs=pl.BlockSpec((None, tile, k), lambda l, i: (l, i, 0)),
        out_shape=jax.ShapeDtypeStruct((layers, n_rows, k), bf16),
        compiler_params=pltpu.CompilerParams(
            dimension_semantics=("arbitrary", "arbitrary"), vmem_limit_bytes=VMEM_LIMIT),
        name="prep_rows",
    )(w_t, col_gain.reshape(layers, 1, k))


def _pad_last(w, axis):
    pad = [(0, 0)] * w.ndim
    pad[axis] = (0, LANES - w.shape[axis])
    return jnp.pad(w, pad)


def kernel(x, meta_tokens, norm_g, gla_w_in, gla_w_gate_up, gla_b_gate, gla_head_norm_g, gla_w_out,
           kv_norm_g, fox_w_kv, fox_b_forget, fox_w_in, fox_w_out, final_norm_g):
    n_gla = gla_w_in.shape[0]
    n_fox = fox_w_in.shape[0]

    o_g = 2 * GLA_DK_TOTAL + D_INNER
    o_r = o_g + GLA_GATE_RANK
    gla_g = norm_g[:n_gla]
    gla_w_t = jnp.swapaxes(gla_w_in, 1, 2)
    gla_w_qkv = _prep_rows(gla_w_t, gla_g, o_g, GLA_DK_TOTAL, GLA_HEAD_K ** -0.5)
    gla_w_r = _prep_rows(gla_w_t[:, o_r:], gla_g, D_INNER, 0, 1.0)
    gla_w_low = _pad_last(gla_w_t[:, o_g:o_r] * gla_g[:, None, :], 1).astype(bf16)
    gla_w_up = _pad_last(gla_w_gate_up, 1).astype(bf16)
    gla_w_o, = _prep_weight(gla_w_out, jnp.tile(gla_head_norm_g, (1, GLA_HEADS)), [(D_MODEL, [(0, D_MODEL, 0, 1.0)])])
    kv_w_t = fox_w_kv.T[None]
    kv_w = _prep_rows(kv_w_t, kv_norm_g[None], 2 * D_INNER, 0, 1.0)
    kv_w_f = _pad_last(kv_w_t[:, 2 * D_INNER:] * kv_norm_g[None, None, :], 1).astype(bf16)
    fox_w_qr, = _prep_weight(fox_w_in, norm_g[n_gla:], [
        (2 * D_INNER, [(0, D_INNER, 0, FOX_HEAD_DIM ** -0.5 * LOG2E), (D_INNER, D_INNER, D_INNER, 1.0)])])
    fox_w_o, = _prep_weight(fox_w_out, jnp.ones(fox_w_out.shape[:2], f32), [(D_MODEL, [(0, D_MODEL, 0, 1.0)])])

    h, hb, rs, narrow = _embed(x.reshape(ROWS_X, D_MODEL), meta_tokens, (gla_w_low, 0))
    for layer in range(n_gla):
        proj = _proj(hb, rs, gla_w_qkv, layer, o_g + D_INNER, w_tail=gla_w_r, transposed=True, gated=True)
        o = _gla_mix(proj, narrow, gla_w_up, gla_b_gate.reshape(n_gla, 1, -1), layer)
        next_narrow = (gla_w_low, layer + 1) if layer + 1 < n_gla else (kv_w_f, 0)
        h, hb, rs, narrow = _gla_out_proj(o, proj, gla_w_o, layer, h, next_narrow)

    kv = _proj(hb, rs, kv_w, 0, 2 * D_INNER, transposed=True)
    bias = _forget_bias(narrow, _pad_last(fox_b_forget.reshape(1, -1), 1))

    for j in range(n_fox):
        qr = _proj(hb, rs, fox_w_qr, j, 2 * D_INNER)
        y = _fox_mix(qr, kv, bias)
        if j + 1 < n_fox:
            h, hb, rs = _out_proj(y, fox_w_o, j, h)
        else:
            out = _out_proj(y, fox_w_o, j, h, final_g=final_norm_g)
    return out.reshape(BATCH, SEQ, D_MODEL)
```

```python
import functools

import jax
import jax.numpy as jnp
from jax import lax
from jax.experimental import pallas as pl
from jax.experimental.pallas import tpu as pltpu

f32 = jnp.float32
bf16 = jnp.bfloat16

D_MODEL = 1024
BATCH = 8
SEQ = 2048
N_META = 16
D_INNER = 2048
GLA_HEADS = 4
GLA_HEAD_K = 256
GLA_HEAD_V = 512
GLA_DK_TOTAL = GLA_HEADS * GLA_HEAD_K
GLA_GATE_RANK = 16
GLA_GATE_NORMALIZER = 16.0
FOX_HEADS = 16
FOX_HEAD_DIM = 128
RMS_EPS = 1e-6
NEG_INF = -1e30
LOG2E = 1.4426950408889634

LANES = 128
ROWS_X = BATCH * SEQ
ROWS = ROWS_X + LANES
META_BLOCK = BATCH
VMEM_LIMIT = 56 * 1024 * 1024

GLA_CHUNK = 128
GLA_BASE = 32
GLA_GROUP = 16
MXU_COLS = 256
FOX_TQ = 256
KEYS = 2304
ROW_TILE = 688
PROJ_TILE = 1376
PROJ_COLS = 2048
OUT_TILE = 1024
EMBED_TILE = 2048
PREP_TILE = 512
PREP_ROWS_TILE = 1024


def _seq_block(t):
    return jnp.where(t == 0, META_BLOCK, t - 1)


def _log2_sigmoid(x):
    return jnp.minimum(x, 0.0) * LOG2E - jnp.log2(1.0 + jnp.exp2(jnp.abs(x) * -LOG2E))


def _silu(x):
    half = 0.5 * x
    return half + half * jnp.tanh(half)


def _split_dot(a, b, pieces, split_rhs):
    x = b if split_rhs else a
    acc = None
    for _ in range(pieces):
        p = x.astype(bf16)
        term = (jnp.dot(a, p, preferred_element_type=f32) if split_rhs
                else jnp.dot(p, b, preferred_element_type=f32))
        acc = term if acc is None else acc + term
        x = x - p.astype(f32)
    return acc


def _emit_stream(h, wn_ref, h_ref, hb_ref, rs_ref, narrow_ref=None):
    hb = h.astype(bf16)
    rs = lax.rsqrt(jnp.mean(h * h, axis=-1, keepdims=True) + RMS_EPS)
    h_ref[...] = h
    hb_ref[...] = hb
    rs_ref[...] = rs
    if narrow_ref is not None:
        narrow_ref[...] = _proj_dot(True, hb, wn_ref[...]) * rs


def _stream_specs(rows, d, tile, narrow):
    out_specs = [pl.BlockSpec((tile, d), lambda i: (i, 0)),
                 pl.BlockSpec((tile, d), lambda i: (i, 0)),
                 pl.BlockSpec((tile, 1), lambda i: (i, 0))]
    out_shapes = [jax.ShapeDtypeStruct((rows, d), f32),
                  jax.ShapeDtypeStruct((rows, d), bf16),
                  jax.ShapeDtypeStruct((rows, 1), f32)]
    if narrow is None:
        return [], [], out_specs, out_shapes
    w, layer = narrow
    out_specs.append(pl.BlockSpec((tile, LANES), lambda i: (i, 0)))
    out_shapes.append(jax.ShapeDtypeStruct((rows, LANES), f32))
    return [pl.BlockSpec((None, LANES, d), lambda i: (layer, 0, 0))], [w], out_specs, out_shapes


def _embed_kernel(x_ref, meta_ref, wn_ref, *out_refs):
    i = pl.program_id(0)

    @pl.when(i < ROWS_X // EMBED_TILE)
    def _():
        _emit_stream(x_ref[...], wn_ref, *out_refs)

    @pl.when(i == ROWS_X // EMBED_TILE)
    def _():
        pad = jnp.zeros((EMBED_TILE - N_META, D_MODEL), f32)
        _emit_stream(jnp.concatenate([meta_ref[...], pad], axis=0), wn_ref, *out_refs)


def _embed(x2d, meta_tokens, narrow):
    n_x = ROWS_X // EMBED_TILE
    w_specs, w_args, out_specs, out_shapes = _stream_specs(ROWS, D_MODEL, EMBED_TILE, narrow)
    return pl.pallas_call(
        _embed_kernel,
        grid=(n_x + 1,),
        in_specs=[
            pl.BlockSpec((EMBED_TILE, D_MODEL), lambda i: (jnp.minimum(i, n_x - 1), 0)),
            pl.BlockSpec((N_META, D_MODEL), lambda i: (0, 0)),
        ] + w_specs,
        out_specs=out_specs,
        out_shape=out_shapes,
        compiler_params=pltpu.CompilerParams(dimension_semantics=("arbitrary",), vmem_limit_bytes=VMEM_LIMIT),
        name="embed",
    )(x2d, meta_tokens, *w_args)


def _proj_dot(transposed, x, w):
    dims = (((1,), (1 if transposed else 0,)), ((), ()))
    return lax.dot_general(x, w, dims, preferred_element_type=f32)


def _proj_store(transposed, gated, xb_ref, rs_ref, w_ref, o_ref):
    acc = _proj_dot(transposed, xb_ref[...], w_ref[...]) * rs_ref[...]
    o_ref[...] = (_silu(acc) if gated else acc).astype(o_ref.dtype)


def _proj_kernel(transposed, xb_ref, rs_ref, w_ref, o_ref):
    _proj_store(transposed, False, xb_ref, rs_ref, w_ref, o_ref)


def _proj_gated_kernel(transposed, has_tail, xb_ref, rs_ref, w_ref, *rest):
    wt_ref, o_ref = rest if has_tail else (w_ref,) + rest
    last = pl.num_programs(1) - 1

    @pl.when(pl.program_id(1) < last)
    def _():
        _proj_store(transposed, False, xb_ref, rs_ref, w_ref, o_ref)

    @pl.when(pl.program_id(1) == last)
    def _():
        _proj_store(transposed, True, xb_ref, rs_ref, wt_ref, o_ref)


def _proj(hb, rs, w, layer, n, w_tail=None, transposed=False, gated=False):
    rows, d = hb.shape
    tm, tn = PROJ_TILE, PROJ_COLS
    assert rows % tm == 0 and n % tn == 0 and (gated or w_tail is None)
    n_w = n // tn - (w_tail is not None)

    def w_spec(col_block):
        if transposed:
            return pl.BlockSpec((None, tn, d), lambda i, j: (layer, col_block(j), 0))
        return pl.BlockSpec((None, d, tn), lambda i, j: (layer, 0, col_block(j)))

    in_specs = [
        pl.BlockSpec((tm, d), lambda i, j: (i, 0)),
        pl.BlockSpec((tm, 1), lambda i, j: (i, 0)),
        w_spec(lambda j: jnp.minimum(j, n_w - 1)),
    ]
    args = [hb, rs, w]
    if w_tail is not None:
        in_specs.append(w_spec(lambda j: 0))
        args.append(w_tail)
    body = (functools.partial(_proj_gated_kernel, transposed, w_tail is not None) if gated
            else functools.partial(_proj_kernel, transposed))
    return pl.pallas_call(
        body,
        grid=(rows // tm, n // tn),
        in_specs=in_specs,
        out_specs=pl.BlockSpec((tm, tn), lambda i, j: (i, j)),
        out_shape=jax.ShapeDtypeStruct((rows, n), bf16),
        compiler_params=pltpu.CompilerParams(
            dimension_semantics=("arbitrary", "arbitrary"), vmem_limit_bytes=VMEM_LIMIT),
        name="proj",
    )(*args)


def _out_proj_kernel(y_ref, w_ref, h_ref, *out_refs):
    _emit_stream(h_ref[...] + jnp.dot(y_ref[...], w_ref[...], preferred_element_type=f32), None, *out_refs)


def _out_proj_final_kernel(y_ref, w_ref, h_ref, g_ref, o_ref):
    x = h_ref[...] + jnp.dot(y_ref[...], w_ref[...], preferred_element_type=f32)
    var = jnp.mean(x * x, axis=-1, keepdims=True)
    o_ref[...] = x * lax.rsqrt(var + RMS_EPS) * g_ref[...]


def _out_proj(y, w, layer, h, final_g=None):
    _, k, d = w.shape
    if final_g is None:
        rows, tile, body, extra, extra_specs = h.shape[0], ROW_TILE, _out_proj_kernel, (), []
        _, _, out_specs, out_shape = _stream_specs(rows, d, tile, None)
    else:
        rows, tile, body = ROWS_X, OUT_TILE, _out_proj_final_kernel
        extra, extra_specs = (final_g.reshape(1, d),), [pl.BlockSpec((1, d), lambda i: (0, 0))]
        out_specs, out_shape = pl.BlockSpec((tile, d), lambda i: (i, 0)), jax.ShapeDtypeStruct((rows, d), f32)
    return pl.pallas_call(
        body,
        grid=(rows // tile,),
        in_specs=[
            pl.BlockSpec((tile, k), lambda i: (i, 0)),
            pl.BlockSpec((None, k, d), lambda i: (layer, 0, 0)),
            pl.BlockSpec((tile, d), lambda i: (i, 0)),
        ] + extra_specs,
        out_specs=out_specs,
        out_shape=out_shape,
        compiler_params=pltpu.CompilerParams(
            dimension_semantics=("arbitrary",), vmem_limit_bytes=VMEM_LIMIT),
        name="out_proj" if final_g is None else "out_proj_final",
    )(y, w, h, *extra)


def _gla_out_kernel(o_ref, gate_ref, w_ref, h_ref, wn_ref, *out_refs):
    acc = h_ref[...]
    for hd in range(GLA_HEADS):
        cols = slice(hd * GLA_HEAD_V, (hd + 1) * GLA_HEAD_V)
        o = o_ref[:, cols].astype(f32)
        var = jnp.mean(o * o, axis=-1, keepdims=True)
        y = o * lax.rsqrt(var + RMS_EPS) * gate_ref[:, cols].astype(f32)
        acc = acc + jnp.dot(y.astype(bf16), w_ref[cols, :], preferred_element_type=f32)
    _emit_stream(acc, wn_ref, *out_refs)


def _gla_out_proj(o, proj, w, layer, h, narrow):
    _, k, d = w.shape
    rows = h.shape[0]
    w_specs, w_args, out_specs, out_shape = _stream_specs(rows, d, ROW_TILE, narrow)
    return pl.pallas_call(
        _gla_out_kernel,
        grid=(rows // ROW_TILE,),
        in_specs=[
            pl.BlockSpec((ROW_TILE, k), lambda i: (i, 0)),
            pl.BlockSpec((ROW_TILE, k), lambda i: (i, proj.shape[1] // k - 1)),
            pl.BlockSpec((None, k, d), lambda i: (layer, 0, 0)),
            pl.BlockSpec((ROW_TILE, d), lambda i: (i, 0)),
        ] + w_specs,
        out_specs=out_specs,
        out_shape=out_shape,
        compiler_params=pltpu.CompilerParams(
            dimension_semantics=("arbitrary",), vmem_limit_bytes=VMEM_LIMIT),
        name="gla_out_proj",
    )(o, proj, w, h, *w_args)


def _gla_masks(c):
    i = lax.broadcasted_iota(jnp.int32, (c, c), 0)
    j = lax.broadcasted_iota(jnp.int32, (c, c), 1)
    base = min(c, GLA_BASE)
    uses = [((i & -base) == (j & -base)) & (j <= i)]
    size = 2 * base
    while size <= c:
        half = size // 2
        uses.append(((i & -size) == (j & -size)) & ((i & half) != 0) & ((j & half) == 0))
        size *= 2
    return (j <= i).astype(bf16), uses


def _gla_group(nc, c, masks, q, k, v, log2_a, state_t, emit):
    n, dk = q.shape
    nt = (((1,), (1,)), ((), ()))
    tn = (((0,), (0,)), ((), ()))
    tril, uses = masks
    row = lax.broadcasted_iota(jnp.int32, (n, 1), 0)

    b = jnp.concatenate([_split_dot(tril, log2_a[ci * c:(ci + 1) * c], 2, split_rhs=True)
                         for ci in range(nc)], axis=0)

    def minus_anchor(size):
        b3 = b.reshape(n // size, size, dk)
        return (b3 - b3[:, size // 2 - 1:size // 2, :]).reshape(n, dk)

    base = min(c, GLA_BASE)
    e0 = minus_anchor(base)
    operands = [((q * jnp.exp2(e0)).astype(bf16), (k * jnp.exp2(-e0)).astype(bf16))]
    size = 2 * base
    while size <= c:
        qk = (jnp.where((row & (size // 2)) != 0, q, k) * jnp.exp2(-jnp.abs(minus_anchor(size)))).astype(bf16)
        operands.append((qk, qk))
        size *= 2

    b3 = b.reshape(nc, c, dk)
    b_last = b3[:, c - 1:c, :]
    q_dec = (q * jnp.exp2(b)).astype(bf16)
    k_rem = (k.reshape(nc, c, dk) * jnp.exp2(b_last - b3)).reshape(n, dk).astype(bf16)
    decay = jnp.exp2(b_last)

    o_intra, update = [], []
    for ci in range(nc):
        rows = slice(ci * c, (ci + 1) * c)
        attn = None
        for (lhs, rhs), use in zip(operands, uses):
            p = lax.dot_general(lhs[rows], rhs[rows], nt, preferred_element_type=f32)
            attn = jnp.where(use, p, 0.0 if attn is None else attn)
        o_intra.append(jnp.dot(attn.astype(bf16), v[rows], preferred_element_type=f32))
        update.append(lax.dot_general(v[rows], k_rem[rows], tn, preferred_element_type=f32))

    for ci in range(nc):
        rows = slice(ci * c, (ci + 1) * c)
        emit(ci, o_intra[ci] + lax.dot_general(q_dec[rows], state_t.astype(bf16), nt,
                                               preferred_element_type=f32))
        state_t = state_t * decay[ci] + update[ci]
    return state_t


def _gla_kernel(q_ref, k_ref, v_ref, g_ref, wup_ref, bg_ref, o_ref, st_ref, st0_ref):
    t = pl.program_id(1)
    wup = wup_ref[...]
    bg = bg_ref[...]

    def run_group(row0, nc, c, masks, state_t):
        rows = pl.ds(row0, nc * c)
        logits = jnp.dot(g_ref[rows, :].astype(bf16), wup, preferred_element_type=f32) + bg
        log2_a = _log2_sigmoid(logits) * (1.0 / GLA_GATE_NORMALIZER)

        def emit(ci, o):
            o_ref[pl.ds(row0 + ci * c, c), :] = o.astype(o_ref.dtype)

        return _gla_group(nc, c, masks, q_ref[rows, :].astype(f32), k_ref[rows, :].astype(f32),
                          v_ref[rows, :], log2_a, state_t, emit)

    @pl.when(t == 0)
    def _():
        zero_state = jnp.zeros((GLA_HEAD_V, GLA_HEAD_K), f32)
        st0_ref[...] = run_group(0, 1, N_META, _gla_masks(N_META), zero_state)
        o_ref[N_META:LANES, :] = jnp.zeros((LANES - N_META, GLA_HEAD_V), o_ref.dtype)

    @pl.when(t > 0)
    def _():
        st_ref[...] = st0_ref[...]
        group_rows = GLA_GROUP * GLA_CHUNK
        masks = _gla_masks(GLA_CHUNK)

        def body(gi, carry):
            row0 = pl.multiple_of(gi * group_rows, group_rows)
            st_ref[...] = run_group(row0, GLA_GROUP, GLA_CHUNK, masks, st_ref[...])
            return carry

        lax.fori_loop(0, SEQ // group_rows, body, 0)


def _gla_mix(proj, g_low, w_up, b_gate, layer):
    kq = GLA_DK_TOTAL // GLA_HEAD_K
    kv = 2 * GLA_DK_TOTAL // GLA_HEAD_V
    return pl.pallas_call(
        _gla_kernel,
        grid=(GLA_HEADS, BATCH + 1),
        in_specs=[
            pl.BlockSpec((SEQ, GLA_HEAD_K), lambda h, t: (_seq_block(t), h)),
            pl.BlockSpec((SEQ, GLA_HEAD_K), lambda h, t: (_seq_block(t), kq + h)),
            pl.BlockSpec((SEQ, GLA_HEAD_V), lambda h, t: (_seq_block(t), kv + h)),
            pl.BlockSpec((SEQ, LANES), lambda h, t: (_seq_block(t), 0)),
            pl.BlockSpec((None, LANES, GLA_HEAD_K), lambda h, t: (layer, 0, h)),
            pl.BlockSpec((None, 1, GLA_HEAD_K), lambda h, t: (layer, 0, h)),
        ],
        out_specs=pl.BlockSpec((SEQ, GLA_HEAD_V), lambda h, t: (_seq_block(t), h)),
        out_shape=jax.ShapeDtypeStruct((ROWS, D_INNER), bf16),
        scratch_shapes=[pltpu.VMEM((GLA_HEAD_V, GLA_HEAD_K), f32),
                        pltpu.VMEM((GLA_HEAD_V, GLA_HEAD_K), f32)],
        compiler_params=pltpu.CompilerParams(
            dimension_semantics=("arbitrary", "arbitrary"), vmem_limit_bytes=VMEM_LIMIT),
        name="gla_mix",
    )(proj, proj, proj, g_low, w_up, b_gate)


def _bias_kernel(f_ref, b_ref, o_ref, meta_ref):
    t = pl.program_id(0)
    i = lax.broadcasted_iota(jnp.int32, (LANES, LANES), 0)
    j = lax.broadcasted_iota(jnp.int32, (LANES, LANES), 1)
    upper = (i <= j).astype(bf16)

    def block_cumsum(c, carry):
        log2_f = _log2_sigmoid(f_ref[c * LANES:(c + 1) * LANES, :] + b_ref[...])
        return _split_dot(log2_f.T, upper, 3, split_rhs=False) + carry

    @pl.when(t == 0)
    def _():
        meta_ref[...] = block_cumsum(0, jnp.zeros((LANES, 1), f32))

    @pl.when(t > 0)
    def _():
        meta = meta_ref[...]
        o_ref[:, 0, 0, 0:N_META] = -meta[0:FOX_HEADS, 0:N_META]
        carry = meta[:, N_META - 1:N_META]
        for c in range(SEQ // LANES):
            cs = block_cumsum(c, carry)
            o_ref[:, 0, 0, N_META + c * LANES:N_META + (c + 1) * LANES] = -cs[0:FOX_HEADS, :]
            carry = cs[:, LANES - 1:LANES]
        o_ref[:, 0, 0, N_META + SEQ:KEYS] = jnp.zeros((FOX_HEADS, KEYS - N_META - SEQ), f32)


def _forget_bias(f_logit, b_forget):
    return pl.pallas_call(
        _bias_kernel,
        grid=(BATCH + 1,),
        in_specs=[
            pl.BlockSpec((SEQ, LANES), lambda t: (_seq_block(t), 0)),
            pl.BlockSpec((1, LANES), lambda t: (0, 0)),
        ],
        out_specs=pl.BlockSpec((FOX_HEADS, 1, 1, KEYS), lambda t: (0, jnp.maximum(t - 1, 0), 0, 0)),
        out_shape=jax.ShapeDtypeStruct((FOX_HEADS, BATCH, 1, KEYS), f32),
        scratch_shapes=[pltpu.VMEM((LANES, LANES), f32)],
        compiler_params=pltpu.CompilerParams(dimension_semantics=("arbitrary",)),
        name="forget_bias",
    )(f_logit, b_forget)


def _scores(q, k, bias, mask):
    s = lax.dot_general(q, k, (((1,), (1,)), ((), ())), preferred_element_type=f32) + bias
    return s if mask is None else jnp.where(mask, s, NEG_INF)


def _softmax_pv(scores, values):
    m = functools.reduce(jnp.maximum, [jnp.max(s, axis=-1, keepdims=True) for s in scores])
    acc = None
    for s, v in zip(scores, values):
        pv = jnp.dot(jnp.exp2(s - m).astype(bf16), v, preferred_element_type=f32)
        acc = pv if acc is None else acc + pv
    return acc[:, 0:FOX_HEAD_DIM] * (1.0 / acc[:, FOX_HEAD_DIM:])


def _fox_kernel(q_ref, r_ref, k_ref, v_ref, b_ref, y_ref, kc_ref, vc_ref):
    t = pl.program_id(1)
    dh = FOX_HEAD_DIM

    def lower_tri(nq, nk, shift):
        i = lax.broadcasted_iota(jnp.int32, (nq, nk), 0)
        j = lax.broadcasted_iota(jnp.int32, (nq, nk), 1)
        return j <= i + shift

    def finish(rows, o):
        y_ref[rows, :] = (o * _silu(r_ref[rows, :].astype(f32))).astype(y_ref.dtype)

    @pl.when(t == 0)
    def _():
        vc_ref[:, dh:] = jnp.ones((KEYS, dh), bf16)
        kc_ref[0:N_META, :] = k_ref[0:N_META, :]
        vc_ref[0:N_META, 0:dh] = v_ref[0:N_META, :]
        sc = _scores(q_ref[0:N_META, :], kc_ref[0:N_META, :], b_ref[0, 0, :, 0:N_META],
                     lower_tri(N_META, N_META, 0))
        finish(pl.ds(0, N_META), _softmax_pv([sc], [vc_ref[0:N_META, :]]))
        y_ref[N_META:LANES, :] = jnp.zeros((LANES - N_META, dh), y_ref.dtype)

    @pl.when(t > 0)
    def _():
        n_keys = N_META + SEQ
        kc_ref[N_META:n_keys, :] = k_ref[...]
        vc_ref[N_META:n_keys, 0:dh] = v_ref[...]

        def tile_scores(ti):
            p0 = ti * FOX_TQ
            k1 = min(p0 + FOX_TQ, n_keys)
            q0 = max(p0 - N_META, 0)
            q1 = k1 - N_META
            q = q_ref[q0:q1, :]
            k0 = p0 // MXU_COLS * MXU_COLS
            scores, values = [], []
            if k0 > 0:
                scores.append(_scores(q, kc_ref[0:k0, :], b_ref[0, 0, :, 0:k0], None))
                values.append(vc_ref[0:k0, :])
            shift = q0 + N_META - k0
            scores.append(_scores(q, kc_ref[k0:k1, :], b_ref[0, 0, :, k0:k1],
                                  lower_tri(q1 - q0, k1 - k0, shift)))
            values.append(vc_ref[k0:k1, :])
            return pl.ds(q0, q1 - q0), scores, values

        n_tiles = -(-n_keys // FOX_TQ)
        pending = tile_scores(0)
        for ti in range(n_tiles):
            rows, scores, values = pending
            if ti + 1 < n_tiles:
                pending = tile_scores(ti + 1)
            finish(rows, _softmax_pv(scores, values))


def _fox_mix(qr, kv, bias):
    nh = FOX_HEADS
    return pl.pallas_call(
        _fox_kernel,
        grid=(FOX_HEADS, BATCH + 1),
        in_specs=[
            pl.BlockSpec((SEQ, FOX_HEAD_DIM), lambda h, t: (_seq_block(t), h)),
            pl.BlockSpec((SEQ, FOX_HEAD_DIM), lambda h, t: (_seq_block(t), nh + h)),
            pl.BlockSpec((SEQ, FOX_HEAD_DIM), lambda h, t: (_seq_block(t), h)),
            pl.BlockSpec((SEQ, FOX_HEAD_DIM), lambda h, t: (_seq_block(t), nh + h)),
            pl.BlockSpec((1, 1, 1, KEYS), lambda h, t: (h, jnp.maximum(t - 1, 0), 0, 0)),
        ],
        out_specs=pl.BlockSpec((SEQ, FOX_HEAD_DIM), lambda h, t: (_seq_block(t), h)),
        out_shape=jax.ShapeDtypeStruct((ROWS, D_INNER), bf16),
        scratch_shapes=[pltpu.VMEM((KEYS, FOX_HEAD_DIM), bf16),
                        pltpu.VMEM((KEYS, 2 * FOX_HEAD_DIM), bf16)],
        compiler_params=pltpu.CompilerParams(
            dimension_semantics=("arbitrary", "arbitrary"), vmem_limit_bytes=VMEM_LIMIT),
        name="fox_mix",
    )(qr, qr, kv, kv, bias)


def _prep_kernel(outputs, w_ref, g_ref, *out_refs):
    g = g_ref[...]
    for (width, pieces), o_ref in zip(outputs, out_refs):
        if sum(p[1] for p in pieces) < width:
            o_ref[...] = jnp.zeros(o_ref.shape, o_ref.dtype)
        for src, cols, dst, scale in pieces:
            o_ref[:, dst:dst + cols] = (w_ref[:, src:src + cols] * (g * scale)).astype(o_ref.dtype)


def _prep_weight(w, row_gain, outputs):
    layers, k, n = w.shape
    return pl.pallas_call(
        functools.partial(_prep_kernel, outputs),
        grid=(layers, k // PREP_TILE),
        in_specs=[
            pl.BlockSpec((None, PREP_TILE, n), lambda l, i: (l, i, 0)),
            pl.BlockSpec((None, PREP_TILE, 1), lambda l, i: (l, i, 0)),
        ],
        out_specs=[pl.BlockSpec((None, PREP_TILE, width), lambda l, i: (l, i, 0)) for width, _ in outputs],
        out_shape=[jax.ShapeDtypeStruct((layers, k, width), bf16) for width, _ in outputs],
        compiler_params=pltpu.CompilerParams(
            dimension_semantics=("arbitrary", "arbitrary"), vmem_limit_bytes=VMEM_LIMIT),
        name="prep_weight",
    )(w, row_gain.reshape(layers, k, 1))


def _prep_rows_kernel(scaled_tiles, scale, w_ref, g_ref, o_ref):
    s = jnp.where(pl.program_id(1) < scaled_tiles, scale, 1.0)
    o_ref[...] = (w_ref[0] * (g_ref[...] * s)).astype(o_ref.dtype)


def _prep_rows(w_t, col_gain, first_row, n_rows, scaled_rows, scale):
    layers, _, k = w_t.shape
    tile = PREP_ROWS_TILE
    assert n_rows % tile == 0 and scaled_rows % tile == 0
    return pl.pallas_call(
        functools.partial(_prep_rows_kernel, scaled_rows // tile, scale),
        grid=(layers, n_rows // tile),
        in_specs=[
            pl.BlockSpec((pl.Element(1), pl.Element(tile), pl.Element(k)),
                         lambda l, i: (l, pl.multiple_of(first_row + i * tile, 16), 0)),
            pl.BlockSpec((None, 1, k), lambda l, i: (l, 0, 0)),
        ],
        out_specs=pl.BlockSpec((None, tile, k), lambda l, i: (l, i, 0)),
        out_shape=jax.ShapeDtypeStruct((layers, n_rows, k), bf16),
        compiler_params=pltpu.CompilerParams(
            dimension_semantics=("arbitrary", "arbitrary"), vmem_limit_bytes=VMEM_LIMIT),
        name="prep_rows",
    )(w_t, col_gain.reshape(layers, 1, k))


def _pad_last(w, axis):
    pad = [(0, 0)] * w.ndim
    pad[axis] = (0, LANES - w.shape[axis])
    return jnp.pad(w, pad)


def kernel(x, meta_tokens, norm_g, gla_w_in, gla_w_gate_up, gla_b_gate, gla_head_norm_g, gla_w_out,
           kv_norm_g, fox_w_kv, fox_b_forget, fox_w_in, fox_w_out, final_norm_g):
    n_gla = gla_w_in.shape[0]
    n_fox = fox_w_in.shape[0]

    o_g = 2 * GLA_DK_TOTAL + D_INNER
    o_r = o_g + GLA_GATE_RANK
    gla_g = norm_g[:n_gla]
    gla_w_t = jnp.swapaxes(gla_w_in, 1, 2)
    gla_w_qkv = _prep_rows(gla_w_t, gla_g, 0, o_g, GLA_DK_TOTAL, GLA_HEAD_K ** -0.5)
    gla_w_r = _prep_rows(gla_w_t, gla_g, o_r, D_INNER, 0, 1.0)
    gla_w_low = _pad_last(gla_w_t[:, o_g:o_r] * gla_g[:, None, :], 1).astype(bf16)
    gla_w_up = _pad_last(gla_w_gate_up, 1).astype(bf16)
    gla_w_o, = _prep_weight(gla_w_out, jnp.tile(gla_head_norm_g, (1, GLA_HEADS)), [(D_MODEL, [(0, D_MODEL, 0, 1.0)])])
    kv_w_t = fox_w_kv.T[None]
    kv_w = _prep_rows(kv_w_t, kv_norm_g[None], 0, 2 * D_INNER, 0, 1.0)
    kv_w_f = _pad_last(kv_w_t[:, 2 * D_INNER:] * kv_norm_g[None, None, :], 1).astype(bf16)
    fox_w_qr, = _prep_weight(fox_w_in, norm_g[n_gla:], [
        (2 * D_INNER, [(0, D_INNER, 0, FOX_HEAD_DIM ** -0.5 * LOG2E), (D_INNER, D_INNER, D_INNER, 1.0)])])
    fox_w_o, = _prep_weight(fox_w_out, jnp.ones(fox_w_out.shape[:2], f32), [(D_MODEL, [(0, D_MODEL, 0, 1.0)])])

    h, hb, rs, narrow = _embed(x.reshape(ROWS_X, D_MODEL), meta_tokens, (gla_w_low, 0))
    for layer in range(n_gla):
        proj = _proj(hb, rs, gla_w_qkv, layer, o_g + D_INNER, w_tail=gla_w_r, transposed=True, gated=True)
        o = _gla_mix(proj, narrow, gla_w_up, gla_b_gate.reshape(n_gla, 1, -1), layer)
        next_narrow = (gla_w_low, layer + 1) if layer + 1 < n_gla else (kv_w_f, 0)
        h, hb, rs, narrow = _gla_out_proj(o, proj, gla_w_o, layer, h, next_narrow)

    kv = _proj(hb, rs, kv_w, 0, 2 * D_INNER, transposed=True)
    bias = _forget_bias(narrow, _pad_last(fox_b_forget.reshape(1, -1), 1))

    for j in range(n_fox):
        qr = _proj(hb, rs, fox_w_qr, j, 2 * D_INNER)
        y = _fox_mix(qr, kv, bias)
        if j + 1 < n_fox:
            h, hb, rs = _out_proj(y, fox_w_o, j, h)
        else:
            out = _out_proj(y, fox_w_o, j, h, final_g=final_norm_g)
    return out.reshape(BATCH, SEQ, D_MODEL)
```

```python
import functools

import jax
import jax.numpy as jnp
from jax import lax
from jax.experimental import pallas as pl
from jax.experimental.pallas import tpu as pltpu

f32 = jnp.float32
bf16 = jnp.bfloat16

D_MODEL = 1024
BATCH = 8
SEQ = 2048
N_META = 16
D_INNER = 2048
GLA_HEADS = 4
GLA_HEAD_K = 256
GLA_HEAD_V = 512
GLA_DK_TOTAL = GLA_HEADS * GLA_HEAD_K
GLA_GATE_RANK = 16
GLA_GATE_NORMALIZER = 16.0
FOX_HEADS = 16
FOX_HEAD_DIM = 128
RMS_EPS = 1e-6
NEG_INF = -1e30
LOG2E = 1.4426950408889634

LANES = 128
ROWS_X = BATCH * SEQ
ROWS = ROWS_X + LANES
META_BLOCK = BATCH
VMEM_LIMIT = 56 * 1024 * 1024

GLA_CHUNK = 128
GLA_BASE = 32
GLA_GROUP = 16
MXU_COLS = 256
FOX_TQ = 256
KEYS = 2304
ROW_TILE = 688
PROJ_TILE = 1376
PROJ_COLS = 2048
OUT_TILE = 1024
EMBED_TILE = 2048
PREP_TILE = 512
PREP_ROWS_TILE = 1024


def _seq_block(t):
    return jnp.where(t == 0, META_BLOCK, t - 1)


def _log2_sigmoid(x):
    return jnp.minimum(x, 0.0) * LOG2E - jnp.log2(1.0 + jnp.exp2(jnp.abs(x) * -LOG2E))


def _silu(x):
    half = 0.5 * x
    return half + half * jnp.tanh(half)


def _split_dot(a, b, pieces, split_rhs):
    x = b if split_rhs else a
    acc = None
    for _ in range(pieces):
        p = x.astype(bf16)
        term = (jnp.dot(a, p, preferred_element_type=f32) if split_rhs
                else jnp.dot(p, b, preferred_element_type=f32))
        acc = term if acc is None else acc + term
        x = x - p.astype(f32)
    return acc


def _emit_stream(h, wn_ref, h_ref, hb_ref, rs_ref, narrow_ref=None):
    hb = h.astype(bf16)
    rs = lax.rsqrt(jnp.mean(h * h, axis=-1, keepdims=True) + RMS_EPS)
    h_ref[...] = h
    hb_ref[...] = hb
    rs_ref[...] = rs
    if narrow_ref is not None:
        narrow_ref[...] = _proj_dot(True, hb, wn_ref[...]) * rs


def _stream_specs(rows, d, tile, narrow):
    out_specs = [pl.BlockSpec((tile, d), lambda i: (i, 0)),
                 pl.BlockSpec((tile, d), lambda i: (i, 0)),
                 pl.BlockSpec((tile, 1), lambda i: (i, 0))]
    out_shapes = [jax.ShapeDtypeStruct((rows, d), f32),
                  jax.ShapeDtypeStruct((rows, d), bf16),
                  jax.ShapeDtypeStruct((rows, 1), f32)]
    if narrow is None:
        return [], [], out_specs, out_shapes
    w, layer = narrow
    out_specs.append(pl.BlockSpec((tile, LANES), lambda i: (i, 0)))
    out_shapes.append(jax.ShapeDtypeStruct((rows, LANES), f32))
    return [pl.BlockSpec((None, LANES, d), lambda i: (layer, 0, 0))], [w], out_specs, out_shapes


def _embed_kernel(x_ref, meta_ref, wn_ref, *out_refs):
    i = pl.program_id(0)

    @pl.when(i < ROWS_X // EMBED_TILE)
    def _():
        _emit_stream(x_ref[...], wn_ref, *out_refs)

    @pl.when(i == ROWS_X // EMBED_TILE)
    def _():
        pad = jnp.zeros((EMBED_TILE - N_META, D_MODEL), f32)
        _emit_stream(jnp.concatenate([meta_ref[...], pad], axis=0), wn_ref, *out_refs)


def _embed(x2d, meta_tokens, narrow):
    n_x = ROWS_X // EMBED_TILE
    w_specs, w_args, out_specs, out_shapes = _stream_specs(ROWS, D_MODEL, EMBED_TILE, narrow)
    return pl.pallas_call(
        _embed_kernel,
        grid=(n_x + 1,),
        in_specs=[
            pl.BlockSpec((EMBED_TILE, D_MODEL), lambda i: (jnp.minimum(i, n_x - 1), 0)),
            pl.BlockSpec((N_META, D_MODEL), lambda i: (0, 0)),
        ] + w_specs,
        out_specs=out_specs,
        out_shape=out_shapes,
        compiler_params=pltpu.CompilerParams(dimension_semantics=("arbitrary",), vmem_limit_bytes=VMEM_LIMIT),
        name="embed",
    )(x2d, meta_tokens, *w_args)


def _proj_dot(transposed, x, w):
    dims = (((1,), (1 if transposed else 0,)), ((), ()))
    return lax.dot_general(x, w, dims, preferred_element_type=f32)


def _proj_store(transposed, gated, xb_ref, rs_ref, w_ref, o_ref):
    acc = _proj_dot(transposed, xb_ref[...], w_ref[...]) * rs_ref[...]
    o_ref[...] = (_silu(acc) if gated else acc).astype(o_ref.dtype)


def _proj_kernel(transposed, xb_ref, rs_ref, w_ref, o_ref):
    _proj_store(transposed, False, xb_ref, rs_ref, w_ref, o_ref)


def _proj_gated_kernel(transposed, has_tail, xb_ref, rs_ref, w_ref, *rest):
    wt_ref, o_ref = rest if has_tail else (w_ref,) + rest
    last = pl.num_programs(1) - 1

    @pl.when(pl.program_id(1) < last)
    def _():
        _proj_store(transposed, False, xb_ref, rs_ref, w_ref, o_ref)

    @pl.when(pl.program_id(1) == last)
    def _():
        _proj_store(transposed, True, xb_ref, rs_ref, wt_ref, o_ref)


def _proj(hb, rs, w, layer, n, w_tail=None, transposed=False, gated=False):
    rows, d = hb.shape
    tm, tn = PROJ_TILE, PROJ_COLS
    assert rows % tm == 0 and n % tn == 0 and (gated or w_tail is None)
    n_w = n // tn - (w_tail is not None)

    def w_spec(col_block):
        if transposed:
            return pl.BlockSpec((None, tn, d), lambda i, j: (layer, col_block(j), 0))
        return pl.BlockSpec((None, d, tn), lambda i, j: (layer, 0, col_block(j)))

    in_specs = [
        pl.BlockSpec((tm, d), lambda i, j: (i, 0)),
        pl.BlockSpec((tm, 1), lambda i, j: (i, 0)),
        w_spec(lambda j: jnp.minimum(j, n_w - 1)),
    ]
    args = [hb, rs, w]
    if w_tail is not None:
        in_specs.append(w_spec(lambda j: 0))
        args.append(w_tail)
    body = (functools.partial(_proj_gated_kernel, transposed, w_tail is not None) if gated
            else functools.partial(_proj_kernel, transposed))
    return pl.pallas_call(
        body,
        grid=(rows // tm, n // tn),
        in_specs=in_specs,
        out_specs=pl.BlockSpec((tm, tn), lambda i, j: (i, j)),
        out_shape=jax.ShapeDtypeStruct((rows, n), bf16),
        compiler_params=pltpu.CompilerParams(
            dimension_semantics=("arbitrary", "arbitrary"), vmem_limit_bytes=VMEM_LIMIT),
        name="proj",
    )(*args)


def _out_proj_kernel(y_ref, w_ref, h_ref, *out_refs):
    _emit_stream(h_ref[...] + jnp.dot(y_ref[...], w_ref[...], preferred_element_type=f32), None, *out_refs)


def _out_proj_final_kernel(y_ref, w_ref, h_ref, g_ref, o_ref):
    x = h_ref[...] + jnp.dot(y_ref[...], w_ref[...], preferred_element_type=f32)
    var = jnp.mean(x * x, axis=-1, keepdims=True)
    o_ref[...] = x * lax.rsqrt(var + RMS_EPS) * g_ref[...]


def _out_proj(y, w, layer, h, final_g=None):
    _, k, d = w.shape
    if final_g is None:
        rows, tile, body, extra, extra_specs = h.shape[0], ROW_TILE, _out_proj_kernel, (), []
        _, _, out_specs, out_shape = _stream_specs(rows, d, tile, None)
    else:
        rows, tile, body = ROWS_X, OUT_TILE, _out_proj_final_kernel
        extra, extra_specs = (final_g.reshape(1, d),), [pl.BlockSpec((1, d), lambda i: (0, 0))]
        out_specs, out_shape = pl.BlockSpec((tile, d), lambda i: (i, 0)), jax.ShapeDtypeStruct((rows, d), f32)
    return pl.pallas_call(
        body,
        grid=(rows // tile,),
        in_specs=[
            pl.BlockSpec((tile, k), lambda i: (i, 0)),
            pl.BlockSpec((None, k, d), lambda i: (layer, 0, 0)),
            pl.BlockSpec((tile, d), lambda i: (i, 0)),
        ] + extra_specs,
        out_specs=out_specs,
        out_shape=out_shape,
        compiler_params=pltpu.CompilerParams(
            dimension_semantics=("arbitrary",), vmem_limit_bytes=VMEM_LIMIT),
        name="out_proj" if final_g is None else "out_proj_final",
    )(y, w, h, *extra)


def _gla_out_kernel(o_ref, gate_ref, w_ref, h_ref, wn_ref, *out_refs):
    acc = h_ref[...]
    for hd in range(GLA_HEADS):
        cols = slice(hd * GLA_HEAD_V, (hd + 1) * GLA_HEAD_V)
        o = o_ref[:, cols].astype(f32)
        var = jnp.mean(o * o, axis=-1, keepdims=True)
        y = o * lax.rsqrt(var + RMS_EPS) * gate_ref[:, cols].astype(f32)
        acc = acc + jnp.dot(y.astype(bf16), w_ref[cols, :], preferred_element_type=f32)
    _emit_stream(acc, wn_ref, *out_refs)


def _gla_out_proj(o, proj, w, layer, h, narrow):
    _, k, d = w.shape
    rows = h.shape[0]
    w_specs, w_args, out_specs, out_shape = _stream_specs(rows, d, ROW_TILE, narrow)
    return pl.pallas_call(
        _gla_out_kernel,
        grid=(rows // ROW_TILE,),
        in_specs=[
            pl.BlockSpec((ROW_TILE, k), lambda i: (i, 0)),
            pl.BlockSpec((ROW_TILE, k), lambda i: (i, proj.shape[1] // k - 1)),
            pl.BlockSpec((None, k, d), lambda i: (layer, 0, 0)),
            pl.BlockSpec((ROW_TILE, d), lambda i: (i, 0)),
        ] + w_specs,
        out_specs=out_specs,
        out_shape=out_shape,
        compiler_params=pltpu.CompilerParams(
            dimension_semantics=("arbitrary",), vmem_limit_bytes=VMEM_LIMIT),
        name="gla_out_proj",
    )(o, proj, w, h, *w_args)


def _gla_masks(c):
    i = lax.broadcasted_iota(jnp.int32, (c, c), 0)
    j = lax.broadcasted_iota(jnp.int32, (c, c), 1)
    base = min(c, GLA_BASE)
    uses = [((i & -base) == (j & -base)) & (j <= i)]
    size = 2 * base
    while size <= c:
        half = size // 2
        uses.append(((i & -size) == (j & -size)) & ((i & half) != 0) & ((j & half) == 0))
        size *= 2
    return (j <= i).astype(bf16), uses


def _gla_group(nc, c, masks, q, k, v, log2_a, state_t, emit):
    n, dk = q.shape
    nt = (((1,), (1,)), ((), ()))
    tn = (((0,), (0,)), ((), ()))
    tril, uses = masks

    b = jnp.concatenate([_split_dot(tril, log2_a[ci * c:(ci + 1) * c], 2, split_rhs=True)
                         for ci in range(nc)], axis=0)

    base = min(c, GLA_BASE)
    nb = n // base
    b_blk = b.reshape(nb, base, dk)
    anchor = b_blk[:, base // 2 - 1:base // 2, :]
    e0 = b_blk - anchor
    qf = q.reshape(nb, base, dk) * jnp.exp2(e0)
    kf = k.reshape(nb, base, dk) * jnp.exp2(-e0)

    def per_block(x, rows_each):
        m = x.shape[0]
        return jnp.broadcast_to(x[:, None], (m, rows_each // base, 1, dk)).reshape(nb, 1, dk)

    def flat(x):
        return x.reshape(n, dk).astype(bf16)

    operands = [(flat(qf), flat(kf))]
    blk_row = lax.broadcasted_iota(jnp.int32, (nb, 1, 1), 0) * base
    size = 2 * base
    while size <= c:
        mid = per_block(b.reshape(n // size, size, dk)[:, size // 2 - 1:size // 2, :], size)
        qk = flat(jnp.where((blk_row & (size // 2)) != 0, qf, kf) * jnp.exp2(-jnp.abs(anchor - mid)))
        operands.append((qk, qk))
        size *= 2

    b_last = b.reshape(nc, c, dk)[:, c - 1:c, :]
    q_dec = flat(qf * jnp.exp2(anchor))
    k_rem = flat(kf * jnp.exp2(per_block(b_last, c) - anchor))
    decay = jnp.exp2(b_last)

    o_intra, update = [], []
    for ci in range(nc):
        rows = slice(ci * c, (ci + 1) * c)
        attn = None
        for (lhs, rhs), use in zip(operands, uses):
            p = lax.dot_general(lhs[rows], rhs[rows], nt, preferred_element_type=f32)
            attn = jnp.where(use, p, 0.0 if attn is None else attn)
        o_intra.append(jnp.dot(attn.astype(bf16), v[rows], preferred_element_type=f32))
        update.append(lax.dot_general(v[rows], k_rem[rows], tn, preferred_element_type=f32))

    for ci in range(nc):
        rows = slice(ci * c, (ci + 1) * c)
        emit(ci, o_intra[ci] + lax.dot_general(q_dec[rows], state_t.astype(bf16), nt,
                                               preferred_element_type=f32))
        state_t = state_t * decay[ci] + update[ci]
    return state_t


def _gla_kernel(q_ref, k_ref, v_ref, g_ref, wup_ref, bg_ref, o_ref, st_ref, st0_ref):
    t = pl.program_id(1)
    wup = wup_ref[...]
    bg = bg_ref[...]

    def run_group(row0, nc, c, masks, state_t):
        rows = pl.ds(row0, nc * c)
        logits = jnp.dot(g_ref[rows, :].astype(bf16), wup, preferred_element_type=f32) + bg
        log2_a = _log2_sigmoid(logits) * (1.0 / GLA_GATE_NORMALIZER)

        def emit(ci, o):
            o_ref[pl.ds(row0 + ci * c, c), :] = o.astype(o_ref.dtype)

        return _gla_group(nc, c, masks, q_ref[rows, :].astype(f32), k_ref[rows, :].astype(f32),
                          v_ref[rows, :], log2_a, state_t, emit)

    @pl.when(t == 0)
    def _():
        zero_state = jnp.zeros((GLA_HEAD_V, GLA_HEAD_K), f32)
        st0_ref[...] = run_group(0, 1, N_META, _gla_masks(N_META), zero_state)
        o_ref[N_META:LANES, :] = jnp.zeros((LANES - N_META, GLA_HEAD_V), o_ref.dtype)

    @pl.when(t > 0)
    def _():
        st_ref[...] = st0_ref[...]
        group_rows = GLA_GROUP * GLA_CHUNK
        masks = _gla_masks(GLA_CHUNK)

        def body(gi, carry):
            row0 = pl.multiple_of(gi * group_rows, group_rows)
            st_ref[...] = run_group(row0, GLA_GROUP, GLA_CHUNK, masks, st_ref[...])
            return carry

        lax.fori_loop(0, SEQ // group_rows, body, 0)


def _gla_mix(proj, g_low, w_up, b_gate, layer):
    kq = GLA_DK_TOTAL // GLA_HEAD_K
    kv = 2 * GLA_DK_TOTAL // GLA_HEAD_V
    return pl.pallas_call(
        _gla_kernel,
        grid=(GLA_HEADS, BATCH + 1),
        in_specs=[
            pl.BlockSpec((SEQ, GLA_HEAD_K), lambda h, t: (_seq_block(t), h)),
            pl.BlockSpec((SEQ, GLA_HEAD_K), lambda h, t: (_seq_block(t), kq + h)),
            pl.BlockSpec((SEQ, GLA_HEAD_V), lambda h, t: (_seq_block(t), kv + h)),
            pl.BlockSpec((SEQ, LANES), lambda h, t: (_seq_block(t), 0)),
            pl.BlockSpec((None, LANES, GLA_HEAD_K), lambda h, t: (layer, 0, h)),
            pl.BlockSpec((None, 1, GLA_HEAD_K), lambda h, t: (layer, 0, h)),
        ],
        out_specs=pl.BlockSpec((SEQ, GLA_HEAD_V), lambda h, t: (_seq_block(t), h)),
        out_shape=jax.ShapeDtypeStruct((ROWS, D_INNER), bf16),
        scratch_shapes=[pltpu.VMEM((GLA_HEAD_V, GLA_HEAD_K), f32),
                        pltpu.VMEM((GLA_HEAD_V, GLA_HEAD_K), f32)],
        compiler_params=pltpu.CompilerParams(
            dimension_semantics=("arbitrary", "arbitrary"), vmem_limit_bytes=VMEM_LIMIT),
        name="gla_mix",
    )(proj, proj, proj, g_low, w_up, b_gate)


def _bias_kernel(f_ref, b_ref, o_ref, meta_ref):
    t = pl.program_id(0)
    i = lax.broadcasted_iota(jnp.int32, (LANES, LANES), 0)
    j = lax.broadcasted_iota(jnp.int32, (LANES, LANES), 1)
    upper = (i <= j).astype(bf16)

    def block_cumsum(c, carry):
        log2_f = _log2_sigmoid(f_ref[c * LANES:(c + 1) * LANES, :] + b_ref[...])
        return _split_dot(log2_f.T, upper, 3, split_rhs=False) + carry

    @pl.when(t == 0)
    def _():
        meta_ref[...] = block_cumsum(0, jnp.zeros((LANES, 1), f32))

    @pl.when(t > 0)
    def _():
        meta = meta_ref[...]
        o_ref[:, 0, 0, 0:N_META] = -meta[0:FOX_HEADS, 0:N_META]
        carry = meta[:, N_META - 1:N_META]
        for c in range(SEQ // LANES):
            cs = block_cumsum(c, carry)
            o_ref[:, 0, 0, N_META + c * LANES:N_META + (c + 1) * LANES] = -cs[0:FOX_HEADS, :]
            carry = cs[:, LANES - 1:LANES]
        o_ref[:, 0, 0, N_META + SEQ:KEYS] = jnp.zeros((FOX_HEADS, KEYS - N_META - SEQ), f32)


def _forget_bias(f_logit, b_forget):
    return pl.pallas_call(
        _bias_kernel,
        grid=(BATCH + 1,),
        in_specs=[
            pl.BlockSpec((SEQ, LANES), lambda t: (_seq_block(t), 0)),
            pl.BlockSpec((1, LANES), lambda t: (0, 0)),
        ],
        out_specs=pl.BlockSpec((FOX_HEADS, 1, 1, KEYS), lambda t: (0, jnp.maximum(t - 1, 0), 0, 0)),
        out_shape=jax.ShapeDtypeStruct((FOX_HEADS, BATCH, 1, KEYS), f32),
        scratch_shapes=[pltpu.VMEM((LANES, LANES), f32)],
        compiler_params=pltpu.CompilerParams(dimension_semantics=("arbitrary",)),
        name="forget_bias",
    )(f_logit, b_forget)


def _scores(q, k, bias, mask):
    s = lax.dot_general(q, k, (((1,), (1,)), ((), ())), preferred_element_type=f32) + bias
    return s if mask is None else jnp.where(mask, s, NEG_INF)


def _softmax_pv(scores, values):
    m = functools.reduce(jnp.maximum, [jnp.max(s, axis=-1, keepdims=True) for s in scores])
    acc = None
    for s, v in zip(scores, values):
        pv = jnp.dot(jnp.exp2(s - m).astype(bf16), v, preferred_element_type=f32)
        acc = pv if acc is None else acc + pv
    return acc[:, 0:FOX_HEAD_DIM] * (1.0 / acc[:, FOX_HEAD_DIM:])


def _fox_kernel(q_ref, r_ref, k_ref, v_ref, b_ref, y_ref, kc_ref, vc_ref):
    t = pl.program_id(1)
    dh = FOX_HEAD_DIM

    def lower_tri(nq, nk, shift):
        i = lax.broadcasted_iota(jnp.int32, (nq, nk), 0)
        j = lax.broadcasted_iota(jnp.int32, (nq, nk), 1)
        return j <= i + shift

    def finish(rows, o):
        y_ref[rows, :] = (o * _silu(r_ref[rows, :].astype(f32))).astype(y_ref.dtype)

    @pl.when(t == 0)
    def _():
        vc_ref[:, dh:] = jnp.ones((KEYS, dh), bf16)
        kc_ref[0:N_META, :] = k_ref[0:N_META, :]
        vc_ref[0:N_META, 0:dh] = v_ref[0:N_META, :]
        sc = _scores(q_ref[0:N_META, :], kc_ref[0:N_META, :], b_ref[0, 0, :, 0:N_META],
                     lower_tri(N_META, N_META, 0))
        finish(pl.ds(0, N_META), _softmax_pv([sc], [vc_ref[0:N_META, :]]))
        y_ref[N_META:LANES, :] = jnp.zeros((LANES - N_META, dh), y_ref.dtype)

    @pl.when(t > 0)
    def _():
        n_keys = N_META + SEQ
        kc_ref[N_META:n_keys, :] = k_ref[...]
        vc_ref[N_META:n_keys, 0:dh] = v_ref[...]

        def tile_scores(ti):
            p0 = ti * FOX_TQ
            k1 = min(p0 + FOX_TQ, n_keys)
            q0 = max(p0 - N_META, 0)
            q1 = k1 - N_META
            q = q_ref[q0:q1, :]
            k0 = p0 // MXU_COLS * MXU_COLS
            scores, values = [], []
            if k0 > 0:
                scores.append(_scores(q, kc_ref[0:k0, :], b_ref[0, 0, :, 0:k0], None))
                values.append(vc_ref[0:k0, :])
            shift = q0 + N_META - k0
            scores.append(_scores(q, kc_ref[k0:k1, :], b_ref[0, 0, :, k0:k1],
                                  lower_tri(q1 - q0, k1 - k0, shift)))
            values.append(vc_ref[k0:k1, :])
            return pl.ds(q0, q1 - q0), scores, values

        n_tiles = -(-n_keys // FOX_TQ)
        pending = tile_scores(0)
        for ti in range(n_tiles):
            rows, scores, values = pending
            if ti + 1 < n_tiles:
                pending = tile_scores(ti + 1)
            finish(rows, _softmax_pv(scores, values))


def _fox_mix(qr, kv, bias):
    nh = FOX_HEADS
    return pl.pallas_call(
        _fox_kernel,
        grid=(FOX_HEADS, BATCH + 1),
        in_specs=[
            pl.BlockSpec((SEQ, FOX_HEAD_DIM), lambda h, t: (_seq_block(t), h)),
            pl.BlockSpec((SEQ, FOX_HEAD_DIM), lambda h, t: (_seq_block(t), nh + h)),
            pl.BlockSpec((SEQ, FOX_HEAD_DIM), lambda h, t: (_seq_block(t), h)),
            pl.BlockSpec((SEQ, FOX_HEAD_DIM), lambda h, t: (_seq_block(t), nh + h)),
            pl.BlockSpec((1, 1, 1, KEYS), lambda h, t: (h, jnp.maximum(t - 1, 0), 0, 0)),
        ],
        out_specs=pl.BlockSpec((SEQ, FOX_HEAD_DIM), lambda h, t: (_seq_block(t), h)),
        out_shape=jax.ShapeDtypeStruct((ROWS, D_INNER), bf16),
        scratch_shapes=[pltpu.VMEM((KEYS, FOX_HEAD_DIM), bf16),
                        pltpu.VMEM((KEYS, 2 * FOX_HEAD_DIM), bf16)],
        compiler_params=pltpu.CompilerParams(
            dimension_semantics=("arbitrary", "arbitrary"), vmem_limit_bytes=VMEM_LIMIT),
        name="fox_mix",
    )(qr, qr, kv, kv, bias)


def _prep_kernel(outputs, w_ref, g_ref, *out_refs):
    g = g_ref[...]
    for (width, pieces), o_ref in zip(outputs, out_refs):
        if sum(p[1] for p in pieces) < width:
            o_ref[...] = jnp.zeros(o_ref.shape, o_ref.dtype)
        for src, cols, dst, scale in pieces:
            o_ref[:, dst:dst + cols] = (w_ref[:, src:src + cols] * (g * scale)).astype(o_ref.dtype)


def _prep_weight(w, row_gain, outputs):
    layers, k, n = w.shape
    return pl.pallas_call(
        functools.partial(_prep_kernel, outputs),
        grid=(layers, k // PREP_TILE),
        in_specs=[
            pl.BlockSpec((None, PREP_TILE, n), lambda l, i: (l, i, 0)),
            pl.BlockSpec((None, PREP_TILE, 1), lambda l, i: (l, i, 0)),
        ],
        out_specs=[pl.BlockSpec((None, PREP_TILE, width), lambda l, i: (l, i, 0)) for width, _ in outputs],
        out_shape=[jax.ShapeDtypeStruct((layers, k, width), bf16) for width, _ in outputs],
        compiler_params=pltpu.CompilerParams(
            dimension_semantics=("arbitrary", "arbitrary"), vmem_limit_bytes=VMEM_LIMIT),
        name="prep_weight",
    )(w, row_gain.reshape(layers, k, 1))


def _prep_rows_kernel(scaled_tiles, scale, w_ref, g_ref, o_ref):
    s = jnp.where(pl.program_id(1) < scaled_tiles, scale, 1.0)
    o_ref[...] = (w_ref[0] * (g_ref[...] * s)).astype(o_ref.dtype)


def _prep_rows(w_t, col_gain, first_row, n_rows, scaled_rows, scale):
    layers, _, k = w_t.shape
    tile = PREP_ROWS_TILE
    assert n_rows % tile == 0 and scaled_rows % tile == 0
    return pl.pallas_call(
        functools.partial(_prep_rows_kernel, scaled_rows // tile, scale),
        grid=(layers, n_rows // tile),
        in_specs=[
            pl.BlockSpec((pl.Element(1), pl.Element(tile), pl.Element(k)),
                         lambda l, i: (l, pl.multiple_of(first_row + i * tile, 16), 0)),
            pl.BlockSpec((None, 1, k), lambda l, i: (l, 0, 0)),
        ],
        out_specs=pl.BlockSpec((None, tile, k), lambda l, i: (l, i, 0)),
        out_shape=jax.ShapeDtypeStruct((layers, n_rows, k), bf16),
        compiler_params=pltpu.CompilerParams(
            dimension_semantics=("arbitrary", "arbitrary"), vmem_limit_bytes=VMEM_LIMIT),
        name="prep_rows",
    )(w_t, col_gain.reshape(layers, 1, k))


def _pad_last(w, axis):
    pad = [(0, 0)] * w.ndim
    pad[axis] = (0, LANES - w.shape[axis])
    return jnp.pad(w, pad)


def kernel(x, meta_tokens, norm_g, gla_w_in, gla_w_gate_up, gla_b_gate, gla_head_norm_g, gla_w_out,
           kv_norm_g, fox_w_kv, fox_b_forget, fox_w_in, fox_w_out, final_norm_g):
    n_gla = gla_w_in.shape[0]
    n_fox = fox_w_in.shape[0]

    o_g = 2 * GLA_DK_TOTAL + D_INNER
    o_r = o_g + GLA_GATE_RANK
    gla_g = norm_g[:n_gla]
    gla_w_t = jnp.swapaxes(gla_w_in, 1, 2)
    gla_w_qkv = _prep_rows(gla_w_t, gla_g, 0, o_g, GLA_DK_TOTAL, GLA_HEAD_K ** -0.5)
    gla_w_r = _prep_rows(gla_w_t, gla_g, o_r, D_INNER, 0, 1.0)
    gla_w_low = _pad_last(gla_w_t[:, o_g:o_r] * gla_g[:, None, :], 1).astype(bf16)
    gla_w_up = _pad_last(gla_w_gate_up, 1).astype(bf16)
    gla_w_o, = _prep_weight(gla_w_out, jnp.tile(gla_head_norm_g, (1, GLA_HEADS)), [(D_MODEL, [(0, D_MODEL, 0, 1.0)])])
    kv_w_t = fox_w_kv.T[None]
    kv_w = _prep_rows(kv_w_t, kv_norm_g[None], 0, 2 * D_INNER, 0, 1.0)
    kv_w_f = _pad_last(kv_w_t[:, 2 * D_INNER:] * kv_norm_g[None, None, :], 1).astype(bf16)
    fox_w_qr, = _prep_weight(fox_w_in, norm_g[n_gla:], [
        (2 * D_INNER, [(0, D_INNER, 0, FOX_HEAD_DIM ** -0.5 * LOG2E), (D_INNER, D_INNER, D_INNER, 1.0)])])
    fox_w_o, = _prep_weight(fox_w_out, jnp.ones(fox_w_out.shape[:2], f32), [(D_MODEL, [(0, D_MODEL, 0, 1.0)])])

    h, hb, rs, narrow = _embed(x.reshape(ROWS_X, D_MODEL), meta_tokens, (gla_w_low, 0))
    for layer in range(n_gla):
        proj = _proj(hb, rs, gla_w_qkv, layer, o_g + D_INNER, w_tail=gla_w_r, transposed=True, gated=True)
        o = _gla_mix(proj, narrow, gla_w_up, gla_b_gate.reshape(n_gla, 1, -1), layer)
        next_narrow = (gla_w_low, layer + 1) if layer + 1 < n_gla else (kv_w_f, 0)
        h, hb, rs, narrow = _gla_out_proj(o, proj, gla_w_o, layer, h, next_narrow)

    kv = _proj(hb, rs, kv_w, 0, 2 * D_INNER, transposed=True)
    bias = _forget_bias(narrow, _pad_last(fox_b_forget.reshape(1, -1), 1))

    for j in range(n_fox):
        qr = _proj(hb, rs, fox_w_qr, j, 2 * D_INNER)
        y = _fox_mix(qr, kv, bias)
        if j + 1 < n_fox:
            h, hb, rs = _out_proj(y, fox_w_o, j, h)
        else:
            out = _out_proj(y, fox_w_o, j, h, final_g=final_norm_g)
    return out.reshape(BATCH, SEQ, D_MODEL)
```

```python
import functools

import jax
import jax.numpy as jnp
from jax import lax
from jax.experimental import pallas as pl
from jax.experimental.pallas import tpu as pltpu

f32 = jnp.float32
bf16 = jnp.bfloat16

D_MODEL = 1024
BATCH = 8
SEQ = 2048
N_META = 16
D_INNER = 2048
GLA_HEADS = 4
GLA_HEAD_K = 256
GLA_HEAD_V = 512
GLA_DK_TOTAL = GLA_HEADS * GLA_HEAD_K
GLA_GATE_RANK = 16
GLA_GATE_NORMALIZER = 16.0
FOX_HEADS = 16
FOX_HEAD_DIM = 128
RMS_EPS = 1e-6
NEG_INF = -1e30
LOG2E = 1.4426950408889634

LANES = 128
ROWS_X = BATCH * SEQ
ROWS = ROWS_X + LANES
META_BLOCK = BATCH
VMEM_LIMIT = 56 * 1024 * 1024

GLA_CHUNK = 256
GLA_BASE = 32
GLA_GROUP = 8
MXU_COLS = 256
FOX_TQ = 256
KEYS = 2304
ROW_TILE = 688
PROJ_TILE = 1376
PROJ_COLS = 2048
OUT_TILE = 1024
EMBED_TILE = 2048
PREP_TILE = 512
PREP_ROWS_TILE = 1024


def _seq_block(t):
    return jnp.where(t == 0, META_BLOCK, t - 1)


def _log2_sigmoid(x):
    return jnp.minimum(x, 0.0) * LOG2E - jnp.log2(1.0 + jnp.exp2(jnp.abs(x) * -LOG2E))


def _silu(x):
    half = 0.5 * x
    return half + half * jnp.tanh(half)


def _split_dot(a, b, pieces, split_rhs):
    x = b if split_rhs else a
    acc = None
    for _ in range(pieces):
        p = x.astype(bf16)
        term = (jnp.dot(a, p, preferred_element_type=f32) if split_rhs
                else jnp.dot(p, b, preferred_element_type=f32))
        acc = term if acc is None else acc + term
        x = x - p.astype(f32)
    return acc


def _emit_stream(h, wn_ref, h_ref, hb_ref, rs_ref, narrow_ref=None):
    hb = h.astype(bf16)
    rs = lax.rsqrt(jnp.mean(h * h, axis=-1, keepdims=True) + RMS_EPS)
    h_ref[...] = h
    hb_ref[...] = hb
    rs_ref[...] = rs
    if narrow_ref is not None:
        narrow_ref[...] = _proj_dot(True, hb, wn_ref[...]) * rs


def _stream_specs(rows, d, tile, narrow):
    out_specs = [pl.BlockSpec((tile, d), lambda i: (i, 0)),
                 pl.BlockSpec((tile, d), lambda i: (i, 0)),
                 pl.BlockSpec((tile, 1), lambda i: (i, 0))]
    out_shapes = [jax.ShapeDtypeStruct((rows, d), f32),
                  jax.ShapeDtypeStruct((rows, d), bf16),
                  jax.ShapeDtypeStruct((rows, 1), f32)]
    if narrow is None:
        return [], [], out_specs, out_shapes
    w, layer = narrow
    out_specs.append(pl.BlockSpec((tile, LANES), lambda i: (i, 0)))
    out_shapes.append(jax.ShapeDtypeStruct((rows, LANES), f32))
    return [pl.BlockSpec((None, LANES, d), lambda i: (layer, 0, 0))], [w], out_specs, out_shapes


def _embed_kernel(x_ref, meta_ref, wn_ref, *out_refs):
    i = pl.program_id(0)

    @pl.when(i < ROWS_X // EMBED_TILE)
    def _():
        _emit_stream(x_ref[...], wn_ref, *out_refs)

    @pl.when(i == ROWS_X // EMBED_TILE)
    def _():
        pad = jnp.zeros((EMBED_TILE - N_META, D_MODEL), f32)
        _emit_stream(jnp.concatenate([meta_ref[...], pad], axis=0), wn_ref, *out_refs)


def _embed(x2d, meta_tokens, narrow):
    n_x = ROWS_X // EMBED_TILE
    w_specs, w_args, out_specs, out_shapes = _stream_specs(ROWS, D_MODEL, EMBED_TILE, narrow)
    return pl.pallas_call(
        _embed_kernel,
        grid=(n_x + 1,),
        in_specs=[
            pl.BlockSpec((EMBED_TILE, D_MODEL), lambda i: (jnp.minimum(i, n_x - 1), 0)),
            pl.BlockSpec((N_META, D_MODEL), lambda i: (0, 0)),
        ] + w_specs,
        out_specs=out_specs,
        out_shape=out_shapes,
        compiler_params=pltpu.CompilerParams(dimension_semantics=("arbitrary",), vmem_limit_bytes=VMEM_LIMIT),
        name="embed",
    )(x2d, meta_tokens, *w_args)


def _proj_dot(transposed, x, w):
    dims = (((1,), (1 if transposed else 0,)), ((), ()))
    return lax.dot_general(x, w, dims, preferred_element_type=f32)


def _proj_store(transposed, gated, xb_ref, rs_ref, w_ref, o_ref):
    acc = _proj_dot(transposed, xb_ref[...], w_ref[...]) * rs_ref[...]
    o_ref[...] = (_silu(acc) if gated else acc).astype(o_ref.dtype)


def _proj_kernel(transposed, xb_ref, rs_ref, w_ref, o_ref):
    _proj_store(transposed, False, xb_ref, rs_ref, w_ref, o_ref)


def _proj_gated_kernel(transposed, has_tail, xb_ref, rs_ref, w_ref, *rest):
    wt_ref, o_ref = rest if has_tail else (w_ref,) + rest
    last = pl.num_programs(1) - 1

    @pl.when(pl.program_id(1) < last)
    def _():
        _proj_store(transposed, False, xb_ref, rs_ref, w_ref, o_ref)

    @pl.when(pl.program_id(1) == last)
    def _():
        _proj_store(transposed, True, xb_ref, rs_ref, wt_ref, o_ref)


def _proj(hb, rs, w, layer, n, w_tail=None, transposed=False, gated=False):
    rows, d = hb.shape
    tm, tn = PROJ_TILE, PROJ_COLS
    assert rows % tm == 0 and n % tn == 0 and (gated or w_tail is None)
    n_w = n // tn - (w_tail is not None)

    def w_spec(col_block):
        if transposed:
            return pl.BlockSpec((None, tn, d), lambda i, j: (layer, col_block(j), 0))
        return pl.BlockSpec((None, d, tn), lambda i, j: (layer, 0, col_block(j)))

    in_specs = [
        pl.BlockSpec((tm, d), lambda i, j: (i, 0)),
        pl.BlockSpec((tm, 1), lambda i, j: (i, 0)),
        w_spec(lambda j: jnp.minimum(j, n_w - 1)),
    ]
    args = [hb, rs, w]
    if w_tail is not None:
        in_specs.append(w_spec(lambda j: 0))
        args.append(w_tail)
    body = (functools.partial(_proj_gated_kernel, transposed, w_tail is not None) if gated
            else functools.partial(_proj_kernel, transposed))
    return pl.pallas_call(
        body,
        grid=(rows // tm, n // tn),
        in_specs=in_specs,
        out_specs=pl.BlockSpec((tm, tn), lambda i, j: (i, j)),
        out_shape=jax.ShapeDtypeStruct((rows, n), bf16),
        compiler_params=pltpu.CompilerParams(
            dimension_semantics=("arbitrary", "arbitrary"), vmem_limit_bytes=VMEM_LIMIT),
        name="proj",
    )(*args)


def _out_proj_kernel(y_ref, w_ref, h_ref, *out_refs):
    _emit_stream(h_ref[...] + jnp.dot(y_ref[...], w_ref[...], preferred_element_type=f32), None, *out_refs)


def _out_proj_final_kernel(y_ref, w_ref, h_ref, g_ref, o_ref):
    x = h_ref[...] + jnp.dot(y_ref[...], w_ref[...], preferred_element_type=f32)
    var = jnp.mean(x * x, axis=-1, keepdims=True)
    o_ref[...] = x * lax.rsqrt(var + RMS_EPS) * g_ref[...]


def _out_proj(y, w, layer, h, final_g=None):
    _, k, d = w.shape
    if final_g is None:
        rows, tile, body, extra, extra_specs = h.shape[0], ROW_TILE, _out_proj_kernel, (), []
        _, _, out_specs, out_shape = _stream_specs(rows, d, tile, None)
    else:
        rows, tile, body = ROWS_X, OUT_TILE, _out_proj_final_kernel
        extra, extra_specs = (final_g.reshape(1, d),), [pl.BlockSpec((1, d), lambda i: (0, 0))]
        out_specs, out_shape = pl.BlockSpec((tile, d), lambda i: (i, 0)), jax.ShapeDtypeStruct((rows, d), f32)
    return pl.pallas_call(
        body,
        grid=(rows // tile,),
        in_specs=[
            pl.BlockSpec((tile, k), lambda i: (i, 0)),
            pl.BlockSpec((None, k, d), lambda i: (layer, 0, 0)),
            pl.BlockSpec((tile, d), lambda i: (i, 0)),
        ] + extra_specs,
        out_specs=out_specs,
        out_shape=out_shape,
        compiler_params=pltpu.CompilerParams(
            dimension_semantics=("arbitrary",), vmem_limit_bytes=VMEM_LIMIT),
        name="out_proj" if final_g is None else "out_proj_final",
    )(y, w, h, *extra)


def _gla_out_kernel(o_ref, gate_ref, w_ref, h_ref, wn_ref, *out_refs):
    acc = h_ref[...]
    for hd in range(GLA_HEADS):
        cols = slice(hd * GLA_HEAD_V, (hd + 1) * GLA_HEAD_V)
        o = o_ref[:, cols].astype(f32)
        var = jnp.mean(o * o, axis=-1, keepdims=True)
        y = o * lax.rsqrt(var + RMS_EPS) * gate_ref[:, cols].astype(f32)
        acc = acc + jnp.dot(y.astype(bf16), w_ref[cols, :], preferred_element_type=f32)
    _emit_stream(acc, wn_ref, *out_refs)


def _gla_out_proj(o, proj, w, layer, h, narrow):
    _, k, d = w.shape
    rows = h.shape[0]
    w_specs, w_args, out_specs, out_shape = _stream_specs(rows, d, ROW_TILE, narrow)
    return pl.pallas_call(
        _gla_out_kernel,
        grid=(rows // ROW_TILE,),
        in_specs=[
            pl.BlockSpec((ROW_TILE, k), lambda i: (i, 0)),
            pl.BlockSpec((ROW_TILE, k), lambda i: (i, proj.shape[1] // k - 1)),
            pl.BlockSpec((None, k, d), lambda i: (layer, 0, 0)),
            pl.BlockSpec((ROW_TILE, d), lambda i: (i, 0)),
        ] + w_specs,
        out_specs=out_specs,
        out_shape=out_shape,
        compiler_params=pltpu.CompilerParams(
            dimension_semantics=("arbitrary",), vmem_limit_bytes=VMEM_LIMIT),
        name="gla_out_proj",
    )(o, proj, w, h, *w_args)


def _gla_masks(c):
    i = lax.broadcasted_iota(jnp.int32, (c, c), 0)
    j = lax.broadcasted_iota(jnp.int32, (c, c), 1)
    base = min(c, GLA_BASE)
    uses = [((i & -base) == (j & -base)) & (j <= i)]
    size = 2 * base
    while size <= c:
        half = size // 2
        uses.append(((i & -size) == (j & -size)) & ((i & half) != 0) & ((j & half) == 0))
        size *= 2
    return (j <= i).astype(bf16), uses


def _gla_group(nc, c, masks, q, k, v, log2_a, state_t, emit):
    n, dk = q.shape
    nt = (((1,), (1,)), ((), ()))
    tn = (((0,), (0,)), ((), ()))
    tril, uses = masks

    b = jnp.concatenate([_split_dot(tril, log2_a[ci * c:(ci + 1) * c], 2, split_rhs=True)
                         for ci in range(nc)], axis=0)

    base = min(c, GLA_BASE)
    nb = n // base
    b_blk = b.reshape(nb, base, dk)
    anchor = b_blk[:, base // 2 - 1:base // 2, :]
    e0 = b_blk - anchor
    qf = q.reshape(nb, base, dk) * jnp.exp2(e0)
    kf = k.reshape(nb, base, dk) * jnp.exp2(-e0)

    def per_block(x, rows_each):
        m = x.shape[0]
        return jnp.broadcast_to(x[:, None], (m, rows_each // base, 1, dk)).reshape(nb, 1, dk)

    def flat(x):
        return x.reshape(n, dk).astype(bf16)

    operands = [(flat(qf), flat(kf))]
    blk_row = lax.broadcasted_iota(jnp.int32, (nb, 1, 1), 0) * base
    size = 2 * base
    while size <= c:
        mid = per_block(b.reshape(n // size, size, dk)[:, size // 2 - 1:size // 2, :], size)
        qk = flat(jnp.where((blk_row & (size // 2)) != 0, qf, kf) * jnp.exp2(-jnp.abs(anchor - mid)))
        operands.append((qk, qk))
        size *= 2

    b_last = b.reshape(nc, c, dk)[:, c - 1:c, :]
    q_dec = flat(qf * jnp.exp2(anchor))
    k_rem = flat(kf * jnp.exp2(per_block(b_last, c) - anchor))
    decay = jnp.exp2(b_last)

    o_intra, update = [], []
    for ci in range(nc):
        rows = slice(ci * c, (ci + 1) * c)
        attn = None
        for (lhs, rhs), use in zip(operands, uses):
            p = lax.dot_general(lhs[rows], rhs[rows], nt, preferred_element_type=f32)
            attn = jnp.where(use, p, 0.0 if attn is None else attn)
        o_intra.append(jnp.dot(attn.astype(bf16), v[rows], preferred_element_type=f32))
        update.append(lax.dot_general(v[rows], k_rem[rows], tn, preferred_element_type=f32))

    for ci in range(nc):
        rows = slice(ci * c, (ci + 1) * c)
        emit(ci, o_intra[ci] + lax.dot_general(q_dec[rows], state_t.astype(bf16), nt,
                                               preferred_element_type=f32))
        state_t = state_t * decay[ci] + update[ci]
    return state_t


def _gla_kernel(q_ref, k_ref, v_ref, g_ref, wup_ref, bg_ref, o_ref, st_ref, st0_ref):
    t = pl.program_id(1)
    wup = wup_ref[...]
    bg = bg_ref[...]

    def run_group(row0, nc, c, masks, state_t):
        rows = pl.ds(row0, nc * c)
        logits = jnp.dot(g_ref[rows, :].astype(bf16), wup, preferred_element_type=f32) + bg
        log2_a = _log2_sigmoid(logits) * (1.0 / GLA_GATE_NORMALIZER)

        def emit(ci, o):
            o_ref[pl.ds(row0 + ci * c, c), :] = o.astype(o_ref.dtype)

        return _gla_group(nc, c, masks, q_ref[rows, :].astype(f32), k_ref[rows, :].astype(f32),
                          v_ref[rows, :], log2_a, state_t, emit)

    @pl.when(t == 0)
    def _():
        zero_state = jnp.zeros((GLA_HEAD_V, GLA_HEAD_K), f32)
        st0_ref[...] = run_group(0, 1, N_META, _gla_masks(N_META), zero_state)
        o_ref[N_META:LANES, :] = jnp.zeros((LANES - N_META, GLA_HEAD_V), o_ref.dtype)

    @pl.when(t > 0)
    def _():
        st_ref[...] = st0_ref[...]
        group_rows = GLA_GROUP * GLA_CHUNK
        masks = _gla_masks(GLA_CHUNK)

        def body(gi, carry):
            row0 = pl.multiple_of(gi * group_rows, group_rows)
            st_ref[...] = run_group(row0, GLA_GROUP, GLA_CHUNK, masks, st_ref[...])
            return carry

        lax.fori_loop(0, SEQ // group_rows, body, 0)


def _gla_mix(proj, g_low, w_up, b_gate, layer):
    kq = GLA_DK_TOTAL // GLA_HEAD_K
    kv = 2 * GLA_DK_TOTAL // GLA_HEAD_V
    return pl.pallas_call(
        _gla_kernel,
        grid=(GLA_HEADS, BATCH + 1),
        in_specs=[
            pl.BlockSpec((SEQ, GLA_HEAD_K), lambda h, t: (_seq_block(t), h)),
            pl.BlockSpec((SEQ, GLA_HEAD_K), lambda h, t: (_seq_block(t), kq + h)),
            pl.BlockSpec((SEQ, GLA_HEAD_V), lambda h, t: (_seq_block(t), kv + h)),
            pl.BlockSpec((SEQ, LANES), lambda h, t: (_seq_block(t), 0)),
            pl.BlockSpec((None, LANES, GLA_HEAD_K), lambda h, t: (layer, 0, h)),
            pl.BlockSpec((None, 1, GLA_HEAD_K), lambda h, t: (layer, 0, h)),
        ],
        out_specs=pl.BlockSpec((SEQ, GLA_HEAD_V), lambda h, t: (_seq_block(t), h)),
        out_shape=jax.ShapeDtypeStruct((ROWS, D_INNER), bf16),
        scratch_shapes=[pltpu.VMEM((GLA_HEAD_V, GLA_HEAD_K), f32),
                        pltpu.VMEM((GLA_HEAD_V, GLA_HEAD_K), f32)],
        compiler_params=pltpu.CompilerParams(
            dimension_semantics=("arbitrary", "arbitrary"), vmem_limit_bytes=VMEM_LIMIT),
        name="gla_mix",
    )(proj, proj, proj, g_low, w_up, b_gate)


def _bias_kernel(f_ref, b_ref, o_ref, meta_ref):
    t = pl.program_id(0)
    i = lax.broadcasted_iota(jnp.int32, (LANES, LANES), 0)
    j = lax.broadcasted_iota(jnp.int32, (LANES, LANES), 1)
    upper = (i <= j).astype(bf16)

    def block_cumsum(c, carry):
        log2_f = _log2_sigmoid(f_ref[c * LANES:(c + 1) * LANES, :] + b_ref[...])
        return _split_dot(log2_f.T, upper, 3, split_rhs=False) + carry

    @pl.when(t == 0)
    def _():
        meta_ref[...] = block_cumsum(0, jnp.zeros((LANES, 1), f32))

    @pl.when(t > 0)
    def _():
        meta = meta_ref[...]
        o_ref[:, 0, 0, 0:N_META] = -meta[0:FOX_HEADS, 0:N_META]
        carry = meta[:, N_META - 1:N_META]
        for c in range(SEQ // LANES):
            cs = block_cumsum(c, carry)
            o_ref[:, 0, 0, N_META + c * LANES:N_META + (c + 1) * LANES] = -cs[0:FOX_HEADS, :]
            carry = cs[:, LANES - 1:LANES]
        o_ref[:, 0, 0, N_META + SEQ:KEYS] = jnp.zeros((FOX_HEADS, KEYS - N_META - SEQ), f32)


def _forget_bias(f_logit, b_forget):
    return pl.pallas_call(
        _bias_kernel,
        grid=(BATCH + 1,),
        in_specs=[
            pl.BlockSpec((SEQ, LANES), lambda t: (_seq_block(t), 0)),
            pl.BlockSpec((1, LANES), lambda t: (0, 0)),
        ],
        out_specs=pl.BlockSpec((FOX_HEADS, 1, 1, KEYS), lambda t: (0, jnp.maximum(t - 1, 0), 0, 0)),
        out_shape=jax.ShapeDtypeStruct((FOX_HEADS, BATCH, 1, KEYS), f32),
        scratch_shapes=[pltpu.VMEM((LANES, LANES), f32)],
        compiler_params=pltpu.CompilerParams(dimension_semantics=("arbitrary",)),
        name="forget_bias",
    )(f_logit, b_forget)


def _scores(q, k, bias, mask):
    s = lax.dot_general(q, k, (((1,), (1,)), ((), ())), preferred_element_type=f32) + bias
    return s if mask is None else jnp.where(mask, s, NEG_INF)


def _softmax_pv(scores, values):
    m = functools.reduce(jnp.maximum, [jnp.max(s, axis=-1, keepdims=True) for s in scores])
    acc = None
    for s, v in zip(scores, values):
        pv = jnp.dot(jnp.exp2(s - m).astype(bf16), v, preferred_element_type=f32)
        acc = pv if acc is None else acc + pv
    return acc[:, 0:FOX_HEAD_DIM] * (1.0 / acc[:, FOX_HEAD_DIM:])


def _fox_kernel(q_ref, r_ref, k_ref, v_ref, b_ref, y_ref, kc_ref, vc_ref):
    t = pl.program_id(1)
    dh = FOX_HEAD_DIM

    def lower_tri(nq, nk, shift):
        i = lax.broadcasted_iota(jnp.int32, (nq, nk), 0)
        j = lax.broadcasted_iota(jnp.int32, (nq, nk), 1)
        return j <= i + shift

    def finish(rows, o):
        y_ref[rows, :] = (o * _silu(r_ref[rows, :].astype(f32))).astype(y_ref.dtype)

    @pl.when(t == 0)
    def _():
        vc_ref[:, dh:] = jnp.ones((KEYS, dh), bf16)
        kc_ref[0:N_META, :] = k_ref[0:N_META, :]
        vc_ref[0:N_META, 0:dh] = v_ref[0:N_META, :]
        sc = _scores(q_ref[0:N_META, :], kc_ref[0:N_META, :], b_ref[0, 0, :, 0:N_META],
                     lower_tri(N_META, N_META, 0))
        finish(pl.ds(0, N_META), _softmax_pv([sc], [vc_ref[0:N_META, :]]))
        y_ref[N_META:LANES, :] = jnp.zeros((LANES - N_META, dh), y_ref.dtype)

    @pl.when(t > 0)
    def _():
        n_keys = N_META + SEQ
        kc_ref[N_META:n_keys, :] = k_ref[...]
        vc_ref[N_META:n_keys, 0:dh] = v_ref[...]

        def tile_scores(ti):
            p0 = ti * FOX_TQ
            k1 = min(p0 + FOX_TQ, n_keys)
            q0 = max(p0 - N_META, 0)
            q1 = k1 - N_META
            q = q_ref[q0:q1, :]
            k0 = p0 // MXU_COLS * MXU_COLS
            scores, values = [], []
            if k0 > 0:
                scores.append(_scores(q, kc_ref[0:k0, :], b_ref[0, 0, :, 0:k0], None))
                values.append(vc_ref[0:k0, :])
            shift = q0 + N_META - k0
            scores.append(_scores(q, kc_ref[k0:k1, :], b_ref[0, 0, :, k0:k1],
                                  lower_tri(q1 - q0, k1 - k0, shift)))
            values.append(vc_ref[k0:k1, :])
            return pl.ds(q0, q1 - q0), scores, values

        n_tiles = -(-n_keys // FOX_TQ)
        pending = tile_scores(0)
        for ti in range(n_tiles):
            rows, scores, values = pending
            if ti + 1 < n_tiles:
                pending = tile_scores(ti + 1)
            finish(rows, _softmax_pv(scores, values))


def _fox_mix(qr, kv, bias):
    nh = FOX_HEADS
    return pl.pallas_call(
        _fox_kernel,
        grid=(FOX_HEADS, BATCH + 1),
        in_specs=[
            pl.BlockSpec((SEQ, FOX_HEAD_DIM), lambda h, t: (_seq_block(t), h)),
            pl.BlockSpec((SEQ, FOX_HEAD_DIM), lambda h, t: (_seq_block(t), nh + h)),
            pl.BlockSpec((SEQ, FOX_HEAD_DIM), lambda h, t: (_seq_block(t), h)),
            pl.BlockSpec((SEQ, FOX_HEAD_DIM), lambda h, t: (_seq_block(t), nh + h)),
            pl.BlockSpec((1, 1, 1, KEYS), lambda h, t: (h, jnp.maximum(t - 1, 0), 0, 0)),
        ],
        out_specs=pl.BlockSpec((SEQ, FOX_HEAD_DIM), lambda h, t: (_seq_block(t), h)),
        out_shape=jax.ShapeDtypeStruct((ROWS, D_INNER), bf16),
        scratch_shapes=[pltpu.VMEM((KEYS, FOX_HEAD_DIM), bf16),
                        pltpu.VMEM((KEYS, 2 * FOX_HEAD_DIM), bf16)],
        compiler_params=pltpu.CompilerParams(
            dimension_semantics=("arbitrary", "arbitrary"), vmem_limit_bytes=VMEM_LIMIT),
        name="fox_mix",
    )(qr, qr, kv, kv, bias)


def _prep_kernel(outputs, w_ref, g_ref, *out_refs):
    g = g_ref[...]
    for (width, pieces), o_ref in zip(outputs, out_refs):
        if sum(p[1] for p in pieces) < width:
            o_ref[...] = jnp.zeros(o_ref.shape, o_ref.dtype)
        for src, cols, dst, scale in pieces:
            o_ref[:, dst:dst + cols] = (w_ref[:, src:src + cols] * (g * scale)).astype(o_ref.dtype)


def _prep_weight(w, row_gain, outputs):
    layers, k, n = w.shape
    return pl.pallas_call(
        functools.partial(_prep_kernel, outputs),
        grid=(layers, k // PREP_TILE),
        in_specs=[
            pl.BlockSpec((None, PREP_TILE, n), lambda l, i: (l, i, 0)),
            pl.BlockSpec((None, PREP_TILE, 1), lambda l, i: (l, i, 0)),
        ],
        out_specs=[pl.BlockSpec((None, PREP_TILE, width), lambda l, i: (l, i, 0)) for width, _ in outputs],
        out_shape=[jax.ShapeDtypeStruct((layers, k, width), bf16) for width, _ in outputs],
        compiler_params=pltpu.CompilerParams(
            dimension_semantics=("arbitrary", "arbitrary"), vmem_limit_bytes=VMEM_LIMIT),
        name="prep_weight",
    )(w, row_gain.reshape(layers, k, 1))


def _prep_rows_kernel(scaled_tiles, scale, w_ref, g_ref, o_ref):
    s = jnp.where(pl.program_id(1) < scaled_tiles, scale, 1.0)
    o_ref[...] = (w_ref[0] * (g_ref[...] * s)).astype(o_ref.dtype)


def _prep_rows(w_t, col_gain, first_row, n_rows, scaled_rows, scale):
    layers, _, k = w_t.shape
    tile = PREP_ROWS_TILE
    assert n_rows % tile == 0 and scaled_rows % tile == 0
    return pl.pallas_call(
        functools.partial(_prep_rows_kernel, scaled_rows // tile, scale),
        grid=(layers, n_rows // tile),
        in_specs=[
            pl.BlockSpec((pl.Element(1), pl.Element(tile), pl.Element(k)),
                         lambda l, i: (l, pl.multiple_of(first_row + i * tile, 16), 0)),
            pl.BlockSpec((None, 1, k), lambda l, i: (l, 0, 0)),
        ],
        out_specs=pl.BlockSpec((None, tile, k), lambda l, i: (l, i, 0)),
        out_shape=jax.ShapeDtypeStruct((layers, n_rows, k), bf16),
        compiler_params=pltpu.CompilerParams(
            dimension_semantics=("arbitrary", "arbitrary"), vmem_limit_bytes=VMEM_LIMIT),
        name="prep_rows",
    )(w_t, col_gain.reshape(layers, 1, k))


def _pad_last(w, axis):
    pad = [(0, 0)] * w.ndim
    pad[axis] = (0, LANES - w.shape[axis])
    return jnp.pad(w, pad)


def kernel(x, meta_tokens, norm_g, gla_w_in, gla_w_gate_up, gla_b_gate, gla_head_norm_g, gla_w_out,
           kv_norm_g, fox_w_kv, fox_b_forget, fox_w_in, fox_w_out, final_norm_g):
    n_gla = gla_w_in.shape[0]
    n_fox = fox_w_in.shape[0]

    o_g = 2 * GLA_DK_TOTAL + D_INNER
    o_r = o_g + GLA_GATE_RANK
    gla_g = norm_g[:n_gla]
    gla_w_t = jnp.swapaxes(gla_w_in, 1, 2)
    gla_w_qkv = _prep_rows(gla_w_t, gla_g, 0, o_g, GLA_DK_TOTAL, GLA_HEAD_K ** -0.5)
    gla_w_r = _prep_rows(gla_w_t, gla_g, o_r, D_INNER, 0, 1.0)
    gla_w_low = _pad_last(gla_w_t[:, o_g:o_r] * gla_g[:, None, :], 1).astype(bf16)
    gla_w_up = _pad_last(gla_w_gate_up, 1).astype(bf16)
    gla_w_o, = _prep_weight(gla_w_out, jnp.tile(gla_head_norm_g, (1, GLA_HEADS)), [(D_MODEL, [(0, D_MODEL, 0, 1.0)])])
    kv_w_t = fox_w_kv.T[None]
    kv_w = _prep_rows(kv_w_t, kv_norm_g[None], 0, 2 * D_INNER, 0, 1.0)
    kv_w_f = _pad_last(kv_w_t[:, 2 * D_INNER:] * kv_norm_g[None, None, :], 1).astype(bf16)
    fox_w_qr, = _prep_weight(fox_w_in, norm_g[n_gla:], [
        (2 * D_INNER, [(0, D_INNER, 0, FOX_HEAD_DIM ** -0.5 * LOG2E), (D_INNER, D_INNER, D_INNER, 1.0)])])
    fox_w_o, = _prep_weight(fox_w_out, jnp.ones(fox_w_out.shape[:2], f32), [(D_MODEL, [(0, D_MODEL, 0, 1.0)])])

    h, hb, rs, narrow = _embed(x.reshape(ROWS_X, D_MODEL), meta_tokens, (gla_w_low, 0))
    for layer in range(n_gla):
        proj = _proj(hb, rs, gla_w_qkv, layer, o_g + D_INNER, w_tail=gla_w_r, transposed=True, gated=True)
        o = _gla_mix(proj, narrow, gla_w_up, gla_b_gate.reshape(n_gla, 1, -1), layer)
        next_narrow = (gla_w_low, layer + 1) if layer + 1 < n_gla else (kv_w_f, 0)
        h, hb, rs, narrow = _gla_out_proj(o, proj, gla_w_o, layer, h, next_narrow)

    kv = _proj(hb, rs, kv_w, 0, 2 * D_INNER, transposed=True)
    bias = _forget_bias(narrow, _pad_last(fox_b_forget.reshape(1, -1), 1))

    for j in range(n_fox):
        qr = _proj(hb, rs, fox_w_qr, j, 2 * D_INNER)
        y = _fox_mix(qr, kv, bias)
        if j + 1 < n_fox:
            h, hb, rs = _out_proj(y, fox_w_o, j, h)
        else:
            out = _out_proj(y, fox_w_o, j, h, final_g=final_norm_g)
    return out.reshape(BATCH, SEQ, D_MODEL)
```

```python
import functools

import jax
import jax.numpy as jnp
from jax import lax
from jax.experimental import pallas as pl
from jax.experimental.pallas import tpu as pltpu

f32 = jnp.float32
bf16 = jnp.bfloat16

D_MODEL = 1024
BATCH = 8
SEQ = 2048
N_META = 16
D_INNER = 2048
GLA_HEADS = 4
GLA_HEAD_K = 256
GLA_HEAD_V = 512
GLA_DK_TOTAL = GLA_HEADS * GLA_HEAD_K
GLA_GATE_RANK = 16
GLA_GATE_NORMALIZER = 16.0
FOX_HEADS = 16
FOX_HEAD_DIM = 128
RMS_EPS = 1e-6
NEG_INF = -1e30
LOG2E = 1.4426950408889634

LANES = 128
ROWS_X = BATCH * SEQ
ROWS = ROWS_X + LANES
META_BLOCK = BATCH
VMEM_LIMIT = 56 * 1024 * 1024

GLA_CHUNK = 256
GLA_BASE = 32
MXU_COLS = 256
BF16_ROWS = 16
FOX_TQ = 256
KEYS = 2304
ROW_TILE = 688
PROJ_TILE = 1376
PROJ_COLS = 2048
OUT_TILE = 1024
EMBED_TILE = 2048
PREP_TILE = 512
PREP_ROWS_TILE = 1024


def _seq_block(t):
    return jnp.where(t == 0, META_BLOCK, t - 1)


def _log2_sigmoid(x):
    return jnp.minimum(x, 0.0) * LOG2E - jnp.log2(1.0 + jnp.exp2(jnp.abs(x) * -LOG2E))


def _silu(x):
    half = 0.5 * x
    return half + half * jnp.tanh(half)


def _split_dot(a, b, pieces, split_rhs):
    x = b if split_rhs else a
    acc = None
    for _ in range(pieces):
        p = x.astype(bf16)
        term = (jnp.dot(a, p, preferred_element_type=f32) if split_rhs
                else jnp.dot(p, b, preferred_element_type=f32))
        acc = term if acc is None else acc + term
        x = x - p.astype(f32)
    return acc


def _emit_stream(h, wn_ref, h_ref, hb_ref, rs_ref, narrow_ref=None):
    hb = h.astype(bf16)
    rs = lax.rsqrt(jnp.mean(h * h, axis=-1, keepdims=True) + RMS_EPS)
    h_ref[...] = h
    hb_ref[...] = hb
    rs_ref[...] = rs
    if narrow_ref is not None:
        narrow_ref[...] = _proj_dot(True, hb, wn_ref[...]) * rs


def _stream_specs(rows, d, tile, narrow):
    out_specs = [pl.BlockSpec((tile, d), lambda i: (i, 0)),
                 pl.BlockSpec((tile, d), lambda i: (i, 0)),
                 pl.BlockSpec((tile, 1), lambda i: (i, 0))]
    out_shapes = [jax.ShapeDtypeStruct((rows, d), f32),
                  jax.ShapeDtypeStruct((rows, d), bf16),
                  jax.ShapeDtypeStruct((rows, 1), f32)]
    if narrow is None:
        return [], [], out_specs, out_shapes
    w, layer = narrow
    out_specs.append(pl.BlockSpec((tile, LANES), lambda i: (i, 0)))
    out_shapes.append(jax.ShapeDtypeStruct((rows, LANES), f32))
    return [pl.BlockSpec((None, LANES, d), lambda i: (layer, 0, 0))], [w], out_specs, out_shapes


def _embed_kernel(x_ref, meta_ref, wn_ref, *out_refs):
    i = pl.program_id(0)

    @pl.when(i < ROWS_X // EMBED_TILE)
    def _():
        _emit_stream(x_ref[...], wn_ref, *out_refs)

    @pl.when(i == ROWS_X // EMBED_TILE)
    def _():
        pad = jnp.zeros((EMBED_TILE - N_META, D_MODEL), f32)
        _emit_stream(jnp.concatenate([meta_ref[...], pad], axis=0), wn_ref, *out_refs)


def _embed(x2d, meta_tokens, narrow):
    n_x = ROWS_X // EMBED_TILE
    w_specs, w_args, out_specs, out_shapes = _stream_specs(ROWS, D_MODEL, EMBED_TILE, narrow)
    return pl.pallas_call(
        _embed_kernel,
        grid=(n_x + 1,),
        in_specs=[
            pl.BlockSpec((EMBED_TILE, D_MODEL), lambda i: (jnp.minimum(i, n_x - 1), 0)),
            pl.BlockSpec((N_META, D_MODEL), lambda i: (0, 0)),
        ] + w_specs,
        out_specs=out_specs,
        out_shape=out_shapes,
        compiler_params=pltpu.CompilerParams(dimension_semantics=("arbitrary",), vmem_limit_bytes=VMEM_LIMIT),
        name="embed",
    )(x2d, meta_tokens, *w_args)


def _proj_dot(transposed, x, w):
    dims = (((1,), (1 if transposed else 0,)), ((), ()))
    return lax.dot_general(x, w, dims, preferred_element_type=f32)


def _proj_store(transposed, gated, xb_ref, rs_ref, w_ref, o_ref):
    acc = _proj_dot(transposed, xb_ref[...], w_ref[...]) * rs_ref[...]
    o_ref[...] = (_silu(acc) if gated else acc).astype(o_ref.dtype)


def _proj_kernel(transposed, xb_ref, rs_ref, w_ref, o_ref):
    _proj_store(transposed, False, xb_ref, rs_ref, w_ref, o_ref)


def _proj_gated_kernel(transposed, has_tail, xb_ref, rs_ref, w_ref, *rest):
    wt_ref, o_ref = rest if has_tail else (w_ref,) + rest
    last = pl.num_programs(1) - 1

    @pl.when(pl.program_id(1) < last)
    def _():
        _proj_store(transposed, False, xb_ref, rs_ref, w_ref, o_ref)

    @pl.when(pl.program_id(1) == last)
    def _():
        _proj_store(transposed, True, xb_ref, rs_ref, wt_ref, o_ref)


def _proj(hb, rs, w, layer, n, w_tail=None, transposed=False, gated=False):
    rows, d = hb.shape
    tm, tn = PROJ_TILE, PROJ_COLS
    assert rows % tm == 0 and n % tn == 0 and (gated or w_tail is None)
    n_w = n // tn - (w_tail is not None)

    def w_spec(col_block):
        if transposed:
            return pl.BlockSpec((None, tn, d), lambda i, j: (layer, col_block(j), 0))
        return pl.BlockSpec((None, d, tn), lambda i, j: (layer, 0, col_block(j)))

    in_specs = [
        pl.BlockSpec((tm, d), lambda i, j: (i, 0)),
        pl.BlockSpec((tm, 1), lambda i, j: (i, 0)),
        w_spec(lambda j: jnp.minimum(j, n_w - 1)),
    ]
    args = [hb, rs, w]
    if w_tail is not None:
        in_specs.append(w_spec(lambda j: 0))
        args.append(w_tail)
    body = (functools.partial(_proj_gated_kernel, transposed, w_tail is not None) if gated
            else functools.partial(_proj_kernel, transposed))
    return pl.pallas_call(
        body,
        grid=(rows // tm, n // tn),
        in_specs=in_specs,
        out_specs=pl.BlockSpec((tm, tn), lambda i, j: (i, j)),
        out_shape=jax.ShapeDtypeStruct((rows, n), bf16),
        compiler_params=pltpu.CompilerParams(
            dimension_semantics=("arbitrary", "arbitrary"), vmem_limit_bytes=VMEM_LIMIT),
        name="proj",
    )(*args)


def _out_proj_kernel(y_ref, w_ref, h_ref, *out_refs):
    _emit_stream(h_ref[...] + jnp.dot(y_ref[...], w_ref[...], preferred_element_type=f32), None, *out_refs)


def _out_proj_final_kernel(y_ref, w_ref, h_ref, g_ref, o_ref):
    x = h_ref[...] + jnp.dot(y_ref[...], w_ref[...], preferred_element_type=f32)
    var = jnp.mean(x * x, axis=-1, keepdims=True)
    o_ref[...] = x * lax.rsqrt(var + RMS_EPS) * g_ref[...]


def _out_proj(y, w, layer, h, final_g=None):
    _, k, d = w.shape
    if final_g is None:
        rows, tile, body, extra, extra_specs = h.shape[0], ROW_TILE, _out_proj_kernel, (), []
        _, _, out_specs, out_shape = _stream_specs(rows, d, tile, None)
    else:
        rows, tile, body = ROWS_X, OUT_TILE, _out_proj_final_kernel
        extra, extra_specs = (final_g.reshape(1, d),), [pl.BlockSpec((1, d), lambda i: (0, 0))]
        out_specs, out_shape = pl.BlockSpec((tile, d), lambda i: (i, 0)), jax.ShapeDtypeStruct((rows, d), f32)
    return pl.pallas_call(
        body,
        grid=(rows // tile,),
        in_specs=[
            pl.BlockSpec((tile, k), lambda i: (i, 0)),
            pl.BlockSpec((None, k, d), lambda i: (layer, 0, 0)),
            pl.BlockSpec((tile, d), lambda i: (i, 0)),
        ] + extra_specs,
        out_specs=out_specs,
        out_shape=out_shape,
        compiler_params=pltpu.CompilerParams(
            dimension_semantics=("arbitrary",), vmem_limit_bytes=VMEM_LIMIT),
        name="out_proj" if final_g is None else "out_proj_final",
    )(y, w, h, *extra)


def _gla_out_kernel(o_ref, gate_ref, w_ref, h_ref, wn_ref, *out_refs):
    acc = h_ref[...]
    for hd in range(GLA_HEADS):
        cols = slice(hd * GLA_HEAD_V, (hd + 1) * GLA_HEAD_V)
        o = o_ref[:, cols].astype(f32)
        var = jnp.mean(o * o, axis=-1, keepdims=True)
        y = o * lax.rsqrt(var + RMS_EPS) * gate_ref[:, cols].astype(f32)
        acc = acc + jnp.dot(y.astype(bf16), w_ref[cols, :], preferred_element_type=f32)
    _emit_stream(acc, wn_ref, *out_refs)


def _gla_out_proj(o, proj, w, layer, h, narrow):
    _, k, d = w.shape
    rows = h.shape[0]
    w_specs, w_args, out_specs, out_shape = _stream_specs(rows, d, ROW_TILE, narrow)
    return pl.pallas_call(
        _gla_out_kernel,
        grid=(rows // ROW_TILE,),
        in_specs=[
            pl.BlockSpec((ROW_TILE, k), lambda i: (i, 0)),
            pl.BlockSpec((ROW_TILE, k), lambda i: (i, proj.shape[1] // k - 1)),
            pl.BlockSpec((None, k, d), lambda i: (layer, 0, 0)),
            pl.BlockSpec((ROW_TILE, d), lambda i: (i, 0)),
        ] + w_specs,
        out_specs=out_specs,
        out_shape=out_shape,
        compiler_params=pltpu.CompilerParams(
            dimension_semantics=("arbitrary",), vmem_limit_bytes=VMEM_LIMIT),
        name="gla_out_proj",
    )(o, proj, w, h, *w_args)


def _gla_masks(c):
    i = lax.broadcasted_iota(jnp.int32, (c, c), 0)
    j = lax.broadcasted_iota(jnp.int32, (c, c), 1)
    base = min(c, GLA_BASE)
    uses = [((i & -base) == (j & -base)) & (j <= i)]
    size = 2 * base
    while size <= c:
        half = size // 2
        uses.append(((i & -size) == (j & -size)) & ((i & half) != 0) & ((j & half) == 0))
        size *= 2
    return (j <= i).astype(bf16), uses


def _gla_group(nc, c, masks, q, k, v, log2_a, state_t, emit):
    n, dk = q.shape
    nt = (((1,), (1,)), ((), ()))
    tn = (((0,), (0,)), ((), ()))
    tril, uses = masks

    b = jnp.concatenate([_split_dot(tril, log2_a[ci * c:(ci + 1) * c], 2, split_rhs=True)
                         for ci in range(nc)], axis=0)

    base = min(c, GLA_BASE)
    nb = n // base
    b_blk = b.reshape(nb, base, dk)
    anchor = b_blk[:, base // 2 - 1:base // 2, :]
    e0 = b_blk - anchor
    qf = q.reshape(nb, base, dk) * jnp.exp2(e0)
    kf = k.reshape(nb, base, dk) * jnp.exp2(-e0)

    def per_block(x, rows_each):
        m = x.shape[0]
        return jnp.broadcast_to(x[:, None], (m, rows_each // base, 1, dk)).reshape(nb, 1, dk)

    def flat(x):
        return x.reshape(n, dk).astype(bf16)

    operands = [(flat(qf), flat(kf))]
    blk_row = lax.broadcasted_iota(jnp.int32, (nb, 1, 1), 0) * base
    size = 2 * base
    while size <= c:
        mid = per_block(b.reshape(n // size, size, dk)[:, size // 2 - 1:size // 2, :], size)
        qk = flat(jnp.where((blk_row & (size // 2)) != 0, qf, kf) * jnp.exp2(-jnp.abs(anchor - mid)))
        operands.append((qk, qk))
        size *= 2

    b_last = b.reshape(nc, c, dk)[:, c - 1:c, :]
    q_dec = flat(qf * jnp.exp2(anchor))
    k_rem = flat(kf * jnp.exp2(per_block(b_last, c) - anchor))
    decay = jnp.exp2(b_last)

    o_intra, update = [], []
    for ci in range(nc):
        rows = slice(ci * c, (ci + 1) * c)
        attn = None
        for (lhs, rhs), use in zip(operands, uses):
            p = lax.dot_general(lhs[rows], rhs[rows], nt, preferred_element_type=f32)
            attn = jnp.where(use, p, 0.0 if attn is None else attn)
        o_intra.append(jnp.dot(attn.astype(bf16), v[rows], preferred_element_type=f32))
        update.append(lax.dot_general(v[rows], k_rem[rows], tn, preferred_element_type=f32))

    for ci in range(nc):
        rows = slice(ci * c, (ci + 1) * c)
        emit(ci, o_intra[ci] + lax.dot_general(q_dec[rows], state_t.astype(bf16), nt,
                                               preferred_element_type=f32))
        state_t = state_t * decay[ci] + update[ci]
    return state_t


def _gla_kernel(q_ref, k_ref, v_ref, g_ref, wup_ref, bg_ref, o_ref, st0_ref):
    t = pl.program_id(1)
    wup = wup_ref[...]
    bg = bg_ref[...]

    def run_group(nc, c, state_t):
        rows = slice(0, nc * c)
        logits = jnp.dot(g_ref[rows, :].astype(bf16), wup, preferred_element_type=f32) + bg
        log2_a = _log2_sigmoid(logits) * (1.0 / GLA_GATE_NORMALIZER)

        def emit(ci, o):
            o_ref[ci * c:(ci + 1) * c, :] = o.astype(o_ref.dtype)

        return _gla_group(nc, c, _gla_masks(c), q_ref[rows, :].astype(f32), k_ref[rows, :].astype(f32),
                          v_ref[rows, :], log2_a, state_t, emit)

    @pl.when(t == 0)
    def _():
        st0_ref[...] = run_group(1, N_META, jnp.zeros((GLA_HEAD_V, GLA_HEAD_K), f32))
        o_ref[N_META:LANES, :] = jnp.zeros((LANES - N_META, GLA_HEAD_V), o_ref.dtype)

    @pl.when(t > 0)
    def _():
        run_group(SEQ // GLA_CHUNK, GLA_CHUNK, st0_ref[...])


def _gla_mix(proj, g_low, w_up, b_gate, layer):
    kq = GLA_DK_TOTAL // GLA_HEAD_K
    kv = 2 * GLA_DK_TOTAL // GLA_HEAD_V
    return pl.pallas_call(
        _gla_kernel,
        grid=(GLA_HEADS, BATCH + 1),
        in_specs=[
            pl.BlockSpec((SEQ, GLA_HEAD_K), lambda h, t: (_seq_block(t), h)),
            pl.BlockSpec((SEQ, GLA_HEAD_K), lambda h, t: (_seq_block(t), kq + h)),
            pl.BlockSpec((SEQ, GLA_HEAD_V), lambda h, t: (_seq_block(t), kv + h)),
            pl.BlockSpec((SEQ, LANES), lambda h, t: (_seq_block(t), 0)),
            pl.BlockSpec((None, LANES, GLA_HEAD_K), lambda h, t: (layer, 0, h)),
            pl.BlockSpec((None, 1, GLA_HEAD_K), lambda h, t: (layer, 0, h)),
        ],
        out_specs=pl.BlockSpec((SEQ, GLA_HEAD_V), lambda h, t: (_seq_block(t), h)),
        out_shape=jax.ShapeDtypeStruct((ROWS, D_INNER), bf16),
        scratch_shapes=[pltpu.VMEM((GLA_HEAD_V, GLA_HEAD_K), f32)],
        compiler_params=pltpu.CompilerParams(
            dimension_semantics=("arbitrary", "arbitrary"), vmem_limit_bytes=VMEM_LIMIT),
        name="gla_mix",
    )(proj, proj, proj, g_low, w_up, b_gate)


def _bias_kernel(f_ref, b_ref, o_ref, meta_ref):
    t = pl.program_id(0)
    i = lax.broadcasted_iota(jnp.int32, (LANES, LANES), 0)
    j = lax.broadcasted_iota(jnp.int32, (LANES, LANES), 1)
    upper = (i <= j).astype(bf16)

    def block_cumsum(c, carry):
        log2_f = _log2_sigmoid(f_ref[c * LANES:(c + 1) * LANES, :] + b_ref[...])
        return _split_dot(log2_f.T, upper, 3, split_rhs=False) + carry

    @pl.when(t == 0)
    def _():
        meta_ref[...] = block_cumsum(0, jnp.zeros((LANES, 1), f32))

    @pl.when(t > 0)
    def _():
        meta = meta_ref[...]
        o_ref[:, 0, 0, 0:N_META] = -meta[0:FOX_HEADS, 0:N_META]
        carry = meta[:, N_META - 1:N_META]
        for c in range(SEQ // LANES):
            cs = block_cumsum(c, carry)
            o_ref[:, 0, 0, N_META + c * LANES:N_META + (c + 1) * LANES] = -cs[0:FOX_HEADS, :]
            carry = cs[:, LANES - 1:LANES]
        o_ref[:, 0, 0, N_META + SEQ:KEYS] = jnp.zeros((FOX_HEADS, KEYS - N_META - SEQ), f32)


def _forget_bias(f_logit, b_forget):
    return pl.pallas_call(
        _bias_kernel,
        grid=(BATCH + 1,),
        in_specs=[
            pl.BlockSpec((SEQ, LANES), lambda t: (_seq_block(t), 0)),
            pl.BlockSpec((1, LANES), lambda t: (0, 0)),
        ],
        out_specs=pl.BlockSpec((FOX_HEADS, 1, 1, KEYS), lambda t: (0, jnp.maximum(t - 1, 0), 0, 0)),
        out_shape=jax.ShapeDtypeStruct((FOX_HEADS, BATCH, 1, KEYS), f32),
        scratch_shapes=[pltpu.VMEM((LANES, LANES), f32)],
        compiler_params=pltpu.CompilerParams(dimension_semantics=("arbitrary",)),
        name="forget_bias",
    )(f_logit, b_forget)


def _scores(q, k, bias, mask):
    s = lax.dot_general(q, k, (((1,), (1,)), ((), ())), preferred_element_type=f32) + bias
    return s if mask is None else jnp.where(mask, s, NEG_INF)


def _softmax_pv(scores, values):
    m = functools.reduce(jnp.maximum, [jnp.max(s, axis=-1, keepdims=True) for s in scores])
    acc = None
    for s, v in zip(scores, values):
        pv = jnp.dot(jnp.exp2(s - m).astype(bf16), v, preferred_element_type=f32)
        acc = pv if acc is None else acc + pv
    return acc[:, 0:FOX_HEAD_DIM] * (1.0 / acc[:, FOX_HEAD_DIM:])


def _fox_kernel(q_ref, r_ref, k_ref, v_ref, b_ref, y_ref, kc_ref, vc_ref):
    t = pl.program_id(1)
    dh = FOX_HEAD_DIM

    def lower_tri(nq, nk, shift):
        i = lax.broadcasted_iota(jnp.int32, (nq, nk), 0)
        j = lax.broadcasted_iota(jnp.int32, (nq, nk), 1)
        return j <= i + shift

    def finish(rows, o):
        y_ref[rows, :] = (o * _silu(r_ref[rows, :].astype(f32))).astype(y_ref.dtype)

    @pl.when(t == 0)
    def _():
        vc_ref[:, dh:] = jnp.ones((KEYS, dh), bf16)
        kc_ref[0:N_META, :] = k_ref[0:N_META, :]
        vc_ref[0:N_META, 0:dh] = v_ref[0:N_META, :]
        sc = _scores(q_ref[0:N_META, :], kc_ref[0:N_META, :], b_ref[0, 0, :, 0:N_META],
                     lower_tri(N_META, N_META, 0))
        finish(pl.ds(0, N_META), _softmax_pv([sc], [vc_ref[0:N_META, :]]))
        y_ref[N_META:LANES, :] = jnp.zeros((LANES - N_META, dh), y_ref.dtype)

    @pl.when(t > 0)
    def _():
        n_keys = N_META + SEQ
        kc_ref[N_META:n_keys, :] = k_ref[...]
        vc_ref[N_META:n_keys, 0:dh] = v_ref[...]

        def tile_scores(ti):
            p0 = ti * FOX_TQ
            k1 = min(p0 + FOX_TQ, n_keys)
            q0 = max(p0 - N_META, 0)
            q1 = k1 - N_META
            q = q_ref[q0:q1, :]
            k0 = p0 // MXU_COLS * MXU_COLS
            scores, values = [], []
            if k0 > 0:
                scores.append(_scores(q, kc_ref[0:k0, :], b_ref[0, 0, :, 0:k0], None))
                values.append(vc_ref[0:k0, :])
            shift = q0 + N_META - k0
            scores.append(_scores(q, kc_ref[k0:k1, :], b_ref[0, 0, :, k0:k1],
                                  lower_tri(q1 - q0, k1 - k0, shift)))
            values.append(vc_ref[k0:k1, :])
            return pl.ds(q0, q1 - q0), scores, values

        n_tiles = -(-n_keys // FOX_TQ)
        pending = tile_scores(0)
        for ti in range(n_tiles):
            rows, scores, values = pending
            if ti + 1 < n_tiles:
                pending = tile_scores(ti + 1)
            finish(rows, _softmax_pv(scores, values))


def _fox_mix(qr, kv, bias):
    nh = FOX_HEADS
    return pl.pallas_call(
        _fox_kernel,
        grid=(FOX_HEADS, BATCH + 1),
        in_specs=[
            pl.BlockSpec((SEQ, FOX_HEAD_DIM), lambda h, t: (_seq_block(t), h)),
            pl.BlockSpec((SEQ, FOX_HEAD_DIM), lambda h, t: (_seq_block(t), nh + h)),
            pl.BlockSpec((SEQ, FOX_HEAD_DIM), lambda h, t: (_seq_block(t), h)),
            pl.BlockSpec((SEQ, FOX_HEAD_DIM), lambda h, t: (_seq_block(t), nh + h)),
            pl.BlockSpec((1, 1, 1, KEYS), lambda h, t: (h, jnp.maximum(t - 1, 0), 0, 0)),
        ],
        out_specs=pl.BlockSpec((SEQ, FOX_HEAD_DIM), lambda h, t: (_seq_block(t), h)),
        out_shape=jax.ShapeDtypeStruct((ROWS, D_INNER), bf16),
        scratch_shapes=[pltpu.VMEM((KEYS, FOX_HEAD_DIM), bf16),
                        pltpu.VMEM((KEYS, 2 * FOX_HEAD_DIM), bf16)],
        compiler_params=pltpu.CompilerParams(
            dimension_semantics=("arbitrary", "arbitrary"), vmem_limit_bytes=VMEM_LIMIT),
        name="fox_mix",
    )(qr, qr, kv, kv, bias)


def _prep_kernel(outputs, w_ref, g_ref, *out_refs):
    g = g_ref[...]
    for (width, pieces), o_ref in zip(outputs, out_refs):
        if sum(p[1] for p in pieces) < width:
            o_ref[...] = jnp.zeros(o_ref.shape, o_ref.dtype)
        for src, cols, dst, scale in pieces:
            o_ref[:, dst:dst + cols] = (w_ref[:, src:src + cols] * (g * scale)).astype(o_ref.dtype)


def _prep_weight(w, row_gain, outputs):
    layers, k, n = w.shape
    return pl.pallas_call(
        functools.partial(_prep_kernel, outputs),
        grid=(layers, k // PREP_TILE),
        in_specs=[
            pl.BlockSpec((None, PREP_TILE, n), lambda l, i: (l, i, 0)),
            pl.BlockSpec((None, PREP_TILE, 1), lambda l, i: (l, i, 0)),
        ],
        out_specs=[pl.BlockSpec((None, PREP_TILE, width), lambda l, i: (l, i, 0)) for width, _ in outputs],
        out_shape=[jax.ShapeDtypeStruct((layers, k, width), bf16) for width, _ in outputs],
        compiler_params=pltpu.CompilerParams(
            dimension_semantics=("arbitrary", "arbitrary"), vmem_limit_bytes=VMEM_LIMIT),
        name="prep_weight",
    )(w, row_gain.reshape(layers, k, 1))


def _prep_rows_kernel(scaled_tiles, scale, w_ref, g_ref, o_ref):
    s = jnp.where(pl.program_id(1) < scaled_tiles, scale, 1.0)
    o_ref[...] = (w_ref[0] * (g_ref[...] * s)).astype(o_ref.dtype)


def _prep_rows(w_t, col_gain, first_row, n_rows, scaled_rows, scale):
    layers, _, k = w_t.shape
    tile = PREP_ROWS_TILE
    assert n_rows % tile == 0 and scaled_rows % tile == 0 and first_row % BF16_ROWS == 0
    return pl.pallas_call(
        functools.partial(_prep_rows_kernel, scaled_rows // tile, scale),
        grid=(layers, n_rows // tile),
        in_specs=[
            pl.BlockSpec((pl.Element(1), pl.Element(tile), pl.Element(k)),
                         lambda l, i: (l, pl.multiple_of(first_row + i * tile, BF16_ROWS), 0)),
            pl.BlockSpec((None, 1, k), lambda l, i: (l, 0, 0)),
        ],
        out_specs=pl.BlockSpec((None, tile, k), lambda l, i: (l, i, 0)),
        out_shape=jax.ShapeDtypeStruct((layers, n_rows, k), bf16),
        compiler_params=pltpu.CompilerParams(
            dimension_semantics=("arbitrary", "arbitrary"), vmem_limit_bytes=VMEM_LIMIT),
        name="prep_rows",
    )(w_t, col_gain.reshape(layers, 1, k))


def _pad_last(w, axis):
    pad = [(0, 0)] * w.ndim
    pad[axis] = (0, LANES - w.shape[axis])
    return jnp.pad(w, pad)


def kernel(x, meta_tokens, norm_g, gla_w_in, gla_w_gate_up, gla_b_gate, gla_head_norm_g, gla_w_out,
           kv_norm_g, fox_w_kv, fox_b_forget, fox_w_in, fox_w_out, final_norm_g):
    n_gla = gla_w_in.shape[0]
    n_fox = fox_w_in.shape[0]

    o_g = 2 * GLA_DK_TOTAL + D_INNER
    o_r = o_g + GLA_GATE_RANK
    gla_g = norm_g[:n_gla]
    gla_w_t = jnp.swapaxes(gla_w_in, 1, 2)
    gla_w_qkv = _prep_rows(gla_w_t, gla_g, 0, o_g, GLA_DK_TOTAL, GLA_HEAD_K ** -0.5)
    gla_w_r = _prep_rows(gla_w_t, gla_g, o_r, D_INNER, 0, 1.0)
    gla_w_low = _pad_last(gla_w_t[:, o_g:o_r] * gla_g[:, None, :], 1).astype(bf16)
    gla_w_up = _pad_last(gla_w_gate_up, 1).astype(bf16)
    gla_w_o, = _prep_weight(gla_w_out, jnp.tile(gla_head_norm_g, (1, GLA_HEADS)), [(D_MODEL, [(0, D_MODEL, 0, 1.0)])])
    kv_w_t = fox_w_kv.T[None]
    kv_w = _prep_rows(kv_w_t, kv_norm_g[None], 0, 2 * D_INNER, 0, 1.0)
    kv_w_f = _pad_last(kv_w_t[:, 2 * D_INNER:] * kv_norm_g[None, None, :], 1).astype(bf16)
    fox_w_qr, = _prep_weight(fox_w_in, norm_g[n_gla:], [
        (2 * D_INNER, [(0, D_INNER, 0, FOX_HEAD_DIM ** -0.5 * LOG2E), (D_INNER, D_INNER, D_INNER, 1.0)])])
    fox_w_o, = _prep_weight(fox_w_out, jnp.ones(fox_w_out.shape[:2], f32), [(D_MODEL, [(0, D_MODEL, 0, 1.0)])])

    h, hb, rs, narrow = _embed(x.reshape(ROWS_X, D_MODEL), meta_tokens, (gla_w_low, 0))
    for layer in range(n_gla):
        proj = _proj(hb, rs, gla_w_qkv, layer, o_g + D_INNER, w_tail=gla_w_r, transposed=True, gated=True)
        o = _gla_mix(proj, narrow, gla_w_up, gla_b_gate.reshape(n_gla, 1, -1), layer)
        next_narrow = (gla_w_low, layer + 1) if layer + 1 < n_gla else (kv_w_f, 0)
        h, hb, rs, narrow = _gla_out_proj(o, proj, gla_w_o, layer, h, next_narrow)

    kv = _proj(hb, rs, kv_w, 0, 2 * D_INNER, transposed=True)
    bias = _forget_bias(narrow, _pad_last(fox_b_forget.reshape(1, -1), 1))

    for j in range(n_fox):
        qr = _proj(hb, rs, fox_w_qr, j, 2 * D_INNER)
        y = _fox_mix(qr, kv, bias)
        if j + 1 < n_fox:
            h, hb, rs = _out_proj(y, fox_w_o, j, h)
        else:
            out = _out_proj(y, fox_w_o, j, h, final_g=final_norm_g)
    return out.reshape(BATCH, SEQ, D_MODEL)
```

```python
import functools

import jax
import jax.numpy as jnp
from jax import lax
from jax.experimental import pallas as pl
from jax.experimental.pallas import tpu as pltpu

f32 = jnp.float32
bf16 = jnp.bfloat16

D_MODEL = 1024
BATCH = 8
SEQ = 2048
N_META = 16
D_INNER = 2048
GLA_HEADS = 4
GLA_HEAD_K = 256
GLA_HEAD_V = 512
GLA_DK_TOTAL = GLA_HEADS * GLA_HEAD_K
GLA_GATE_RANK = 16
GLA_GATE_NORMALIZER = 16.0
FOX_HEADS = 16
FOX_HEAD_DIM = 128
RMS_EPS = 1e-6
NEG_INF = -1e30
LOG2E = 1.4426950408889634

LANES = 128
ROWS_X = BATCH * SEQ
ROWS = ROWS_X + LANES
META_BLOCK = BATCH
VMEM_LIMIT = 56 * 1024 * 1024

GLA_CHUNK = 256
GLA_BASE = 32
MXU_COLS = 256
BF16_ROWS = 16
FOX_TQ = 256
KEYS = 2304
ROW_TILE = 688
PROJ_TILE = 1376
PROJ_COLS = 2048
OUT_TILE = 1024
EMBED_TILE = 2048
PREP_TILE = 512
PREP_ROWS_TILE = 1024


def _seq_block(t):
    return jnp.where(t == 0, META_BLOCK, t - 1)


def _log2_sigmoid(x):
    return jnp.minimum(x, 0.0) * LOG2E - jnp.log2(1.0 + jnp.exp2(jnp.abs(x) * -LOG2E))


def _silu(x):
    half = 0.5 * x
    return half + half * jnp.tanh(half)


def _split_dot(a, b, pieces, split_rhs):
    x = b if split_rhs else a
    acc = None
    for _ in range(pieces):
        p = x.astype(bf16)
        term = (jnp.dot(a, p, preferred_element_type=f32) if split_rhs
                else jnp.dot(p, b, preferred_element_type=f32))
        acc = term if acc is None else acc + term
        x = x - p.astype(f32)
    return acc


def _emit_stream(h, wn_ref, h_ref, hb_ref, rs_ref, narrow_ref=None):
    hb = h.astype(bf16)
    rs = lax.rsqrt(jnp.mean(h * h, axis=-1, keepdims=True) + RMS_EPS)
    h_ref[...] = h
    hb_ref[...] = hb
    rs_ref[...] = rs
    if narrow_ref is not None:
        narrow_ref[...] = _proj_dot(True, hb, wn_ref[...]) * rs


def _stream_specs(rows, d, tile, narrow):
    out_specs = [pl.BlockSpec((tile, d), lambda i: (i, 0)),
                 pl.BlockSpec((tile, d), lambda i: (i, 0)),
                 pl.BlockSpec((tile, 1), lambda i: (i, 0))]
    out_shapes = [jax.ShapeDtypeStruct((rows, d), f32),
                  jax.ShapeDtypeStruct((rows, d), bf16),
                  jax.ShapeDtypeStruct((rows, 1), f32)]
    if narrow is None:
        return [], [], out_specs, out_shapes
    w, layer = narrow
    out_specs.append(pl.BlockSpec((tile, LANES), lambda i: (i, 0)))
    out_shapes.append(jax.ShapeDtypeStruct((rows, LANES), f32))
    return [pl.BlockSpec((None, LANES, d), lambda i: (layer, 0, 0))], [w], out_specs, out_shapes


def _embed_kernel(x_ref, meta_ref, wn_ref, *out_refs):
    i = pl.program_id(0)

    @pl.when(i < ROWS_X // EMBED_TILE)
    def _():
        _emit_stream(x_ref[...], wn_ref, *out_refs)

    @pl.when(i == ROWS_X // EMBED_TILE)
    def _():
        pad = jnp.zeros((EMBED_TILE - N_META, D_MODEL), f32)
        _emit_stream(jnp.concatenate([meta_ref[...], pad], axis=0), wn_ref, *out_refs)


def _embed(x2d, meta_tokens, narrow):
    n_x = ROWS_X // EMBED_TILE
    w_specs, w_args, out_specs, out_shapes = _stream_specs(ROWS, D_MODEL, EMBED_TILE, narrow)
    return pl.pallas_call(
        _embed_kernel,
        grid=(n_x + 1,),
        in_specs=[
            pl.BlockSpec((EMBED_TILE, D_MODEL), lambda i: (jnp.minimum(i, n_x - 1), 0)),
            pl.BlockSpec((N_META, D_MODEL), lambda i: (0, 0)),
        ] + w_specs,
        out_specs=out_specs,
        out_shape=out_shapes,
        compiler_params=pltpu.CompilerParams(dimension_semantics=("arbitrary",), vmem_limit_bytes=VMEM_LIMIT),
        name="embed",
    )(x2d, meta_tokens, *w_args)


def _proj_dot(transposed, x, w):
    dims = (((1,), (1 if transposed else 0,)), ((), ()))
    return lax.dot_general(x, w, dims, preferred_element_type=f32)


def _proj_store(transposed, gated, xb_ref, rs_ref, w_ref, o_ref):
    acc = _proj_dot(transposed, xb_ref[...], w_ref[...]) * rs_ref[...]
    o_ref[...] = (_silu(acc) if gated else acc).astype(o_ref.dtype)


def _proj_kernel(transposed, xb_ref, rs_ref, w_ref, o_ref):
    _proj_store(transposed, False, xb_ref, rs_ref, w_ref, o_ref)


def _proj_gated_kernel(transposed, has_tail, xb_ref, rs_ref, w_ref, *rest):
    wt_ref, o_ref = rest if has_tail else (w_ref,) + rest
    last = pl.num_programs(1) - 1

    @pl.when(pl.program_id(1) < last)
    def _():
        _proj_store(transposed, False, xb_ref, rs_ref, w_ref, o_ref)

    @pl.when(pl.program_id(1) == last)
    def _():
        _proj_store(transposed, True, xb_ref, rs_ref, wt_ref, o_ref)


def _proj(hb, rs, w, layer, n, w_tail=None, transposed=False, gated=False):
    rows, d = hb.shape
    tm, tn = PROJ_TILE, PROJ_COLS
    assert rows % tm == 0 and n % tn == 0 and (gated or w_tail is None)
    n_w = n // tn - (w_tail is not None)

    def w_spec(col_block):
        if transposed:
            return pl.BlockSpec((None, tn, d), lambda i, j: (layer, col_block(j), 0))
        return pl.BlockSpec((None, d, tn), lambda i, j: (layer, 0, col_block(j)))

    in_specs = [
        pl.BlockSpec((tm, d), lambda i, j: (i, 0)),
        pl.BlockSpec((tm, 1), lambda i, j: (i, 0)),
        w_spec(lambda j: jnp.minimum(j, n_w - 1)),
    ]
    args = [hb, rs, w]
    if w_tail is not None:
        in_specs.append(w_spec(lambda j: 0))
        args.append(w_tail)
    body = (functools.partial(_proj_gated_kernel, transposed, w_tail is not None) if gated
            else functools.partial(_proj_kernel, transposed))
    return pl.pallas_call(
        body,
        grid=(rows // tm, n // tn),
        in_specs=in_specs,
        out_specs=pl.BlockSpec((tm, tn), lambda i, j: (i, j)),
        out_shape=jax.ShapeDtypeStruct((rows, n), bf16),
        compiler_params=pltpu.CompilerParams(
            dimension_semantics=("arbitrary", "arbitrary"), vmem_limit_bytes=VMEM_LIMIT),
        name="proj",
    )(*args)


def _out_proj_kernel(y_ref, w_ref, h_ref, *out_refs):
    _emit_stream(h_ref[...] + jnp.dot(y_ref[...], w_ref[...], preferred_element_type=f32), None, *out_refs)


def _out_proj_final_kernel(y_ref, w_ref, h_ref, g_ref, o_ref):
    x = h_ref[...] + jnp.dot(y_ref[...], w_ref[...], preferred_element_type=f32)
    var = jnp.mean(x * x, axis=-1, keepdims=True)
    o_ref[...] = x * lax.rsqrt(var + RMS_EPS) * g_ref[...]


def _out_proj(y, w, layer, h, final_g=None):
    _, k, d = w.shape
    if final_g is None:
        rows, tile, body, extra, extra_specs = h.shape[0], ROW_TILE, _out_proj_kernel, (), []
        _, _, out_specs, out_shape = _stream_specs(rows, d, tile, None)
    else:
        rows, tile, body = ROWS_X, OUT_TILE, _out_proj_final_kernel
        extra, extra_specs = (final_g.reshape(1, d),), [pl.BlockSpec((1, d), lambda i: (0, 0))]
        out_specs, out_shape = pl.BlockSpec((tile, d), lambda i: (i, 0)), jax.ShapeDtypeStruct((rows, d), f32)
    return pl.pallas_call(
        body,
        grid=(rows // tile,),
        in_specs=[
            pl.BlockSpec((tile, k), lambda i: (i, 0)),
            pl.BlockSpec((None, k, d), lambda i: (layer, 0, 0)),
            pl.BlockSpec((tile, d), lambda i: (i, 0)),
        ] + extra_specs,
        out_specs=out_specs,
        out_shape=out_shape,
        compiler_params=pltpu.CompilerParams(
            dimension_semantics=("arbitrary",), vmem_limit_bytes=VMEM_LIMIT),
        name="out_proj" if final_g is None else "out_proj_final",
    )(y, w, h, *extra)


def _gla_out_kernel(o_ref, gate_ref, w_ref, h_ref, wn_ref, *out_refs):
    acc = h_ref[...]
    for hd in range(GLA_HEADS):
        cols = slice(hd * GLA_HEAD_V, (hd + 1) * GLA_HEAD_V)
        o = o_ref[:, cols].astype(f32)
        var = jnp.mean(o * o, axis=-1, keepdims=True)
        y = o * lax.rsqrt(var + RMS_EPS) * gate_ref[:, cols].astype(f32)
        acc = acc + jnp.dot(y.astype(bf16), w_ref[cols, :], preferred_element_type=f32)
    _emit_stream(acc, wn_ref, *out_refs)


def _gla_out_proj(o, proj, w, layer, h, narrow):
    _, k, d = w.shape
    rows = h.shape[0]
    w_specs, w_args, out_specs, out_shape = _stream_specs(rows, d, ROW_TILE, narrow)
    return pl.pallas_call(
        _gla_out_kernel,
        grid=(rows // ROW_TILE,),
        in_specs=[
            pl.BlockSpec((ROW_TILE, k), lambda i: (i, 0)),
            pl.BlockSpec((ROW_TILE, k), lambda i: (i, proj.shape[1] // k - 1)),
            pl.BlockSpec((None, k, d), lambda i: (layer, 0, 0)),
            pl.BlockSpec((ROW_TILE, d), lambda i: (i, 0)),
        ] + w_specs,
        out_specs=out_specs,
        out_shape=out_shape,
        compiler_params=pltpu.CompilerParams(
            dimension_semantics=("arbitrary",), vmem_limit_bytes=VMEM_LIMIT),
        name="gla_out_proj",
    )(o, proj, w, h, *w_args)


def _gla_masks(c):
    i = lax.broadcasted_iota(jnp.int32, (c, c), 0)
    j = lax.broadcasted_iota(jnp.int32, (c, c), 1)
    base = min(c, GLA_BASE)
    uses = [((i & -base) == (j & -base)) & (j <= i)]
    size = 2 * base
    while size <= c:
        half = size // 2
        uses.append(((i & -size) == (j & -size)) & ((i & half) != 0) & ((j & half) == 0))
        size *= 2
    return (j <= i).astype(bf16), uses


def _gla_group(nc, c, masks, q, k, v, log2_a, state_t, emit):
    n, dk = q.shape
    nt = (((1,), (1,)), ((), ()))
    tn = (((0,), (0,)), ((), ()))
    tril, uses = masks

    b = jnp.concatenate([_split_dot(tril, log2_a[ci * c:(ci + 1) * c], 2, split_rhs=True)
                         for ci in range(nc)], axis=0)

    base = min(c, GLA_BASE)
    nb = n // base
    b_blk = b.reshape(nb, base, dk)
    anchor = b_blk[:, base // 2 - 1:base // 2, :]
    e0 = b_blk - anchor
    qf = q.reshape(nb, base, dk) * jnp.exp2(e0)
    kf = k.reshape(nb, base, dk) * jnp.exp2(-e0)

    def per_block(x, rows_each):
        m = x.shape[0]
        return jnp.broadcast_to(x[:, None], (m, rows_each // base, 1, dk)).reshape(nb, 1, dk)

    def flat(x):
        return x.reshape(n, dk).astype(bf16)

    operands = [(flat(qf), flat(kf))]
    blk_row = lax.broadcasted_iota(jnp.int32, (nb, 1, 1), 0) * base
    size = 2 * base
    while size <= c:
        mid = per_block(b.reshape(n // size, size, dk)[:, size // 2 - 1:size // 2, :], size)
        qk = flat(jnp.where((blk_row & (size // 2)) != 0, qf, kf) * jnp.exp2(-jnp.abs(anchor - mid)))
        operands.append((qk, qk))
        size *= 2

    b_last = b.reshape(nc, c, dk)[:, c - 1:c, :]
    q_dec = flat(qf * jnp.exp2(anchor))
    k_rem = flat(kf * jnp.exp2(per_block(b_last, c) - anchor))
    decay = jnp.exp2(b_last)

    for ci in range(nc):
        rows = slice(ci * c, (ci + 1) * c)
        attn = None
        for (lhs, rhs), use in zip(operands, uses):
            p = lax.dot_general(lhs[rows], rhs[rows], nt, preferred_element_type=f32)
            attn = jnp.where(use, p, 0.0 if attn is None else attn)
        o_intra = jnp.dot(attn.astype(bf16), v[rows], preferred_element_type=f32)
        update = lax.dot_general(v[rows], k_rem[rows], tn, preferred_element_type=f32)
        emit(ci, o_intra + lax.dot_general(q_dec[rows], state_t.astype(bf16), nt,
                                           preferred_element_type=f32))
        state_t = state_t * decay[ci] + update
    return state_t


def _gla_kernel(q_ref, k_ref, v_ref, g_ref, wup_ref, bg_ref, o_ref, st0_ref):
    t = pl.program_id(1)
    wup = wup_ref[...]
    bg = bg_ref[...]

    def run_group(nc, c, state_t):
        rows = slice(0, nc * c)
        logits = jnp.dot(g_ref[rows, :].astype(bf16), wup, preferred_element_type=f32) + bg
        log2_a = _log2_sigmoid(logits) * (1.0 / GLA_GATE_NORMALIZER)

        def emit(ci, o):
            o_ref[ci * c:(ci + 1) * c, :] = o.astype(o_ref.dtype)

        return _gla_group(nc, c, _gla_masks(c), q_ref[rows, :].astype(f32), k_ref[rows, :].astype(f32),
                          v_ref[rows, :], log2_a, state_t, emit)

    @pl.when(t == 0)
    def _():
        st0_ref[...] = run_group(1, N_META, jnp.zeros((GLA_HEAD_V, GLA_HEAD_K), f32))
        o_ref[N_META:LANES, :] = jnp.zeros((LANES - N_META, GLA_HEAD_V), o_ref.dtype)

    @pl.when(t > 0)
    def _():
        run_group(SEQ // GLA_CHUNK, GLA_CHUNK, st0_ref[...])


def _gla_mix(proj, g_low, w_up, b_gate, layer):
    kq = GLA_DK_TOTAL // GLA_HEAD_K
    kv = 2 * GLA_DK_TOTAL // GLA_HEAD_V
    return pl.pallas_call(
        _gla_kernel,
        grid=(GLA_HEADS, BATCH + 1),
        in_specs=[
            pl.BlockSpec((SEQ, GLA_HEAD_K), lambda h, t: (_seq_block(t), h)),
            pl.BlockSpec((SEQ, GLA_HEAD_K), lambda h, t: (_seq_block(t), kq + h)),
            pl.BlockSpec((SEQ, GLA_HEAD_V), lambda h, t: (_seq_block(t), kv + h)),
            pl.BlockSpec((SEQ, LANES), lambda h, t: (_seq_block(t), 0)),
            pl.BlockSpec((None, LANES, GLA_HEAD_K), lambda h, t: (layer, 0, h)),
            pl.BlockSpec((None, 1, GLA_HEAD_K), lambda h, t: (layer, 0, h)),
        ],
        out_specs=pl.BlockSpec((SEQ, GLA_HEAD_V), lambda h, t: (_seq_block(t), h)),
        out_shape=jax.ShapeDtypeStruct((ROWS, D_INNER), bf16),
        scratch_shapes=[pltpu.VMEM((GLA_HEAD_V, GLA_HEAD_K), f32)],
        compiler_params=pltpu.CompilerParams(
            dimension_semantics=("arbitrary", "arbitrary"), vmem_limit_bytes=VMEM_LIMIT),
        name="gla_mix",
    )(proj, proj, proj, g_low, w_up, b_gate)


def _bias_kernel(f_ref, b_ref, o_ref, meta_ref):
    t = pl.program_id(0)
    i = lax.broadcasted_iota(jnp.int32, (LANES, LANES), 0)
    j = lax.broadcasted_iota(jnp.int32, (LANES, LANES), 1)
    upper = (i <= j).astype(bf16)

    def block_cumsum(c, carry):
        log2_f = _log2_sigmoid(f_ref[c * LANES:(c + 1) * LANES, :] + b_ref[...])
        return _split_dot(log2_f.T, upper, 3, split_rhs=False) + carry

    @pl.when(t == 0)
    def _():
        meta_ref[...] = block_cumsum(0, jnp.zeros((LANES, 1), f32))

    @pl.when(t > 0)
    def _():
        meta = meta_ref[...]
        o_ref[:, 0, 0, 0:N_META] = -meta[0:FOX_HEADS, 0:N_META]
        carry = meta[:, N_META - 1:N_META]
        for c in range(SEQ // LANES):
            cs = block_cumsum(c, carry)
            o_ref[:, 0, 0, N_META + c * LANES:N_META + (c + 1) * LANES] = -cs[0:FOX_HEADS, :]
            carry = cs[:, LANES - 1:LANES]
        o_ref[:, 0, 0, N_META + SEQ:KEYS] = jnp.zeros((FOX_HEADS, KEYS - N_META - SEQ), f32)


def _forget_bias(f_logit, b_forget):
    return pl.pallas_call(
        _bias_kernel,
        grid=(BATCH + 1,),
        in_specs=[
            pl.BlockSpec((SEQ, LANES), lambda t: (_seq_block(t), 0)),
            pl.BlockSpec((1, LANES), lambda t: (0, 0)),
        ],
        out_specs=pl.BlockSpec((FOX_HEADS, 1, 1, KEYS), lambda t: (0, jnp.maximum(t - 1, 0), 0, 0)),
        out_shape=jax.ShapeDtypeStruct((FOX_HEADS, BATCH, 1, KEYS), f32),
        scratch_shapes=[pltpu.VMEM((LANES, LANES), f32)],
        compiler_params=pltpu.CompilerParams(dimension_semantics=("arbitrary",)),
        name="forget_bias",
    )(f_logit, b_forget)


def _scores(q, k, bias, mask):
    s = lax.dot_general(q, k, (((1,), (1,)), ((), ())), preferred_element_type=f32) + bias
    return s if mask is None else jnp.where(mask, s, NEG_INF)


def _softmax_pv(scores, values):
    m = functools.reduce(jnp.maximum, [jnp.max(s, axis=-1, keepdims=True) for s in scores])
    acc = None
    for s, v in zip(scores, values):
        pv = jnp.dot(jnp.exp2(s - m).astype(bf16), v, preferred_element_type=f32)
        acc = pv if acc is None else acc + pv
    return acc[:, 0:FOX_HEAD_DIM] * (1.0 / acc[:, FOX_HEAD_DIM:])


def _fox_kernel(q_ref, r_ref, k_ref, v_ref, b_ref, y_ref, kc_ref, vc_ref):
    t = pl.program_id(1)
    dh = FOX_HEAD_DIM

    def lower_tri(nq, nk, shift):
        i = lax.broadcasted_iota(jnp.int32, (nq, nk), 0)
        j = lax.broadcasted_iota(jnp.int32, (nq, nk), 1)
        return j <= i + shift

    def finish(rows, o):
        y_ref[rows, :] = (o * _silu(r_ref[rows, :].astype(f32))).astype(y_ref.dtype)

    @pl.when(t == 0)
    def _():
        vc_ref[:, dh:] = jnp.ones((KEYS, dh), bf16)
        kc_ref[0:N_META, :] = k_ref[0:N_META, :]
        vc_ref[0:N_META, 0:dh] = v_ref[0:N_META, :]
        sc = _scores(q_ref[0:N_META, :], kc_ref[0:N_META, :], b_ref[0, 0, :, 0:N_META],
                     lower_tri(N_META, N_META, 0))
        finish(pl.ds(0, N_META), _softmax_pv([sc], [vc_ref[0:N_META, :]]))
        y_ref[N_META:LANES, :] = jnp.zeros((LANES - N_META, dh), y_ref.dtype)

    @pl.when(t > 0)
    def _():
        n_keys = N_META + SEQ
        kc_ref[N_META:n_keys, :] = k_ref[...]
        vc_ref[N_META:n_keys, 0:dh] = v_ref[...]

        def tile_scores(ti):
            p0 = ti * FOX_TQ
            k1 = min(p0 + FOX_TQ, n_keys)
            q0 = max(p0 - N_META, 0)
            q1 = k1 - N_META
            q = q_ref[q0:q1, :]
            k0 = p0 // MXU_COLS * MXU_COLS
            scores, values = [], []
            if k0 > 0:
                scores.append(_scores(q, kc_ref[0:k0, :], b_ref[0, 0, :, 0:k0], None))
                values.append(vc_ref[0:k0, :])
            shift = q0 + N_META - k0
            scores.append(_scores(q, kc_ref[k0:k1, :], b_ref[0, 0, :, k0:k1],
                                  lower_tri(q1 - q0, k1 - k0, shift)))
            values.append(vc_ref[k0:k1, :])
            return pl.ds(q0, q1 - q0), scores, values

        n_tiles = -(-n_keys // FOX_TQ)
        pending = tile_scores(0)
        for ti in range(n_tiles):
            rows, scores, values = pending
            if ti + 1 < n_tiles:
                pending = tile_scores(ti + 1)
            finish(rows, _softmax_pv(scores, values))


def _fox_mix(qr, kv, bias):
    nh = FOX_HEADS
    return pl.pallas_call(
        _fox_kernel,
        grid=(FOX_HEADS, BATCH + 1),
        in_specs=[
            pl.BlockSpec((SEQ, FOX_HEAD_DIM), lambda h, t: (_seq_block(t), h)),
            pl.BlockSpec((SEQ, FOX_HEAD_DIM), lambda h, t: (_seq_block(t), nh + h)),
            pl.BlockSpec((SEQ, FOX_HEAD_DIM), lambda h, t: (_seq_block(t), h)),
            pl.BlockSpec((SEQ, FOX_HEAD_DIM), lambda h, t: (_seq_block(t), nh + h)),
            pl.BlockSpec((1, 1, 1, KEYS), lambda h, t: (h, jnp.maximum(t - 1, 0), 0, 0)),
        ],
        out_specs=pl.BlockSpec((SEQ, FOX_HEAD_DIM), lambda h, t: (_seq_block(t), h)),
        out_shape=jax.ShapeDtypeStruct((ROWS, D_INNER), bf16),
        scratch_shapes=[pltpu.VMEM((KEYS, FOX_HEAD_DIM), bf16),
                        pltpu.VMEM((KEYS, 2 * FOX_HEAD_DIM), bf16)],
        compiler_params=pltpu.CompilerParams(
            dimension_semantics=("arbitrary", "arbitrary"), vmem_limit_bytes=VMEM_LIMIT),
        name="fox_mix",
    )(qr, qr, kv, kv, bias)


def _prep_kernel(outputs, w_ref, g_ref, *out_refs):
    g = g_ref[...]
    for (width, pieces), o_ref in zip(outputs, out_refs):
        if sum(p[1] for p in pieces) < width:
            o_ref[...] = jnp.zeros(o_ref.shape, o_ref.dtype)
        for src, cols, dst, scale in pieces:
            o_ref[:, dst:dst + cols] = (w_ref[:, src:src + cols] * (g * scale)).astype(o_ref.dtype)


def _prep_weight(w, row_gain, outputs):
    layers, k, n = w.shape
    return pl.pallas_call(
        functools.partial(_prep_kernel, outputs),
        grid=(layers, k // PREP_TILE),
        in_specs=[
            pl.BlockSpec((None, PREP_TILE, n), lambda l, i: (l, i, 0)),
            pl.BlockSpec((None, PREP_TILE, 1), lambda l, i: (l, i, 0)),
        ],
        out_specs=[pl.BlockSpec((None, PREP_TILE, width), lambda l, i: (l, i, 0)) for width, _ in outputs],
        out_shape=[jax.ShapeDtypeStruct((layers, k, width), bf16) for width, _ in outputs],
        compiler_params=pltpu.CompilerParams(
            dimension_semantics=("arbitrary", "arbitrary"), vmem_limit_bytes=VMEM_LIMIT),
        name="prep_weight",
    )(w, row_gain.reshape(layers, k, 1))


def _prep_rows_kernel(scaled_tiles, scale, w_ref, g_ref, o_ref):
    s = jnp.where(pl.program_id(1) < scaled_tiles, scale, 1.0)
    o_ref[...] = (w_ref[0] * (g_ref[...] * s)).astype(o_ref.dtype)


def _prep_rows(w_t, col_gain, first_row, n_rows, scaled_rows, scale):
    layers, _, k = w_t.shape
    tile = PREP_ROWS_TILE
    assert n_rows % tile == 0 and scaled_rows % tile == 0 and first_row % BF16_ROWS == 0
    return pl.pallas_call(
        functools.partial(_prep_rows_kernel, scaled_rows // tile, scale),
        grid=(layers, n_rows // tile),
        in_specs=[
            pl.BlockSpec((pl.Element(1), pl.Element(tile), pl.Element(k)),
                         lambda l, i: (l, pl.multiple_of(first_row + i * tile, BF16_ROWS), 0)),
            pl.BlockSpec((None, 1, k), lambda l, i: (l, 0, 0)),
        ],
        out_specs=pl.BlockSpec((None, tile, k), lambda l, i: (l, i, 0)),
        out_shape=jax.ShapeDtypeStruct((layers, n_rows, k), bf16),
        compiler_params=pltpu.CompilerParams(
            dimension_semantics=("arbitrary", "arbitrary"), vmem_limit_bytes=VMEM_LIMIT),
        name="prep_rows",
    )(w_t, col_gain.reshape(layers, 1, k))


def _pad_last(w, axis):
    pad = [(0, 0)] * w.ndim
    pad[axis] = (0, LANES - w.shape[axis])
    return jnp.pad(w, pad)


def kernel(x, meta_tokens, norm_g, gla_w_in, gla_w_gate_up, gla_b_gate, gla_head_norm_g, gla_w_out,
           kv_norm_g, fox_w_kv, fox_b_forget, fox_w_in, fox_w_out, final_norm_g):
    n_gla = gla_w_in.shape[0]
    n_fox = fox_w_in.shape[0]

    o_g = 2 * GLA_DK_TOTAL + D_INNER
    o_r = o_g + GLA_GATE_RANK
    gla_g = norm_g[:n_gla]
    gla_w_t = jnp.swapaxes(gla_w_in, 1, 2)
    gla_w_qkv = _prep_rows(gla_w_t, gla_g, 0, o_g, GLA_DK_TOTAL, GLA_HEAD_K ** -0.5)
    gla_w_r = _prep_rows(gla_w_t, gla_g, o_r, D_INNER, 0, 1.0)
    gla_w_low = _pad_last(gla_w_t[:, o_g:o_r] * gla_g[:, None, :], 1).astype(bf16)
    gla_w_up = _pad_last(gla_w_gate_up, 1).astype(bf16)
    gla_w_o, = _prep_weight(gla_w_out, jnp.tile(gla_head_norm_g, (1, GLA_HEADS)), [(D_MODEL, [(0, D_MODEL, 0, 1.0)])])
    kv_w_t = fox_w_kv.T[None]
    kv_w = _prep_rows(kv_w_t, kv_norm_g[None], 0, 2 * D_INNER, 0, 1.0)
    kv_w_f = _pad_last(kv_w_t[:, 2 * D_INNER:] * kv_norm_g[None, None, :], 1).astype(bf16)
    fox_w_qr, = _prep_weight(fox_w_in, norm_g[n_gla:], [
        (2 * D_INNER, [(0, D_INNER, 0, FOX_HEAD_DIM ** -0.5 * LOG2E), (D_INNER, D_INNER, D_INNER, 1.0)])])
    fox_w_o, = _prep_weight(fox_w_out, jnp.ones(fox_w_out.shape[:2], f32), [(D_MODEL, [(0, D_MODEL, 0, 1.0)])])

    h, hb, rs, narrow = _embed(x.reshape(ROWS_X, D_MODEL), meta_tokens, (gla_w_low, 0))
    for layer in range(n_gla):
        proj = _proj(hb, rs, gla_w_qkv, layer, o_g + D_INNER, w_tail=gla_w_r, transposed=True, gated=True)
        o = _gla_mix(proj, narrow, gla_w_up, gla_b_gate.reshape(n_gla, 1, -1), layer)
        next_narrow = (gla_w_low, layer + 1) if layer + 1 < n_gla else (kv_w_f, 0)
        h, hb, rs, narrow = _gla_out_proj(o, proj, gla_w_o, layer, h, next_narrow)

    kv = _proj(hb, rs, kv_w, 0, 2 * D_INNER, transposed=True)
    bias = _forget_bias(narrow, _pad_last(fox_b_forget.reshape(1, -1), 1))

    for j in range(n_fox):
        qr = _proj(hb, rs, fox_w_qr, j, 2 * D_INNER)
        y = _fox_mix(qr, kv, bias)
        if j + 1 < n_fox:
            h, hb, rs = _out_proj(y, fox_w_o, j, h)
        else:
            out = _out_proj(y, fox_w_o, j, h, final_g=final_norm_g)
    return out.reshape(BATCH, SEQ, D_MODEL)
```
